```python
import math
import jax
import jax.numpy as jnp
from jax import lax
import numpy as np

D_MODEL = 1024
BATCH = 8
SEQ = 4096
DEPTH = 2

CHUNK = 64
Q_BLOCK = 128
EPS = 1e-6
NEG_BIG = -1e30
EXP_CLIP = 60.0

D_MIX = D_MODEL
N_MIXERS = 4
GROUP_W = D_MIX // N_MIXERS

S5_CH = 16
S5_GROUPS = GROUP_W // S5_CH
S5_STATE = 64
S5_DT_MIN = 1e-3
S5_DT_MAX = 1e-1

HG_HEADS = 4
HG_DH = GROUP_W // HG_HEADS

FOX_HEADS = 4
FOX_DH = GROUP_W // FOX_HEADS

ML_HEADS = 4
ML_DH = GROUP_W // ML_HEADS
ML_CONV = 4

D_FF = int(math.ceil(8 * D_MODEL / 3 / 256)) * 256

COL_WIDTHS = (GROUP_W,) * 8 + (FOX_HEADS,) + (GROUP_W,) * 4 + (ML_HEADS, ML_HEADS)
D_IN = sum(COL_WIDTHS)
SPLIT_IDX = tuple(int(c) for c in np.cumsum(COL_WIDTHS)[:-1])

kernel_name = 'hybrid_s5_hgrn2_fox_mlstm_block'

F32 = jnp.float32


def rms_norm(x, gain):
    xf = x.astype(F32)
    y = xf * lax.rsqrt(jnp.mean(xf * xf, axis=-1, keepdims=True) + EPS)
    return (y * gain.astype(F32)).astype(x.dtype)


def headwise_rms_norm(x, gain, head_dim):
    shp = x.shape
    xf = x.astype(F32).reshape(shp[:-1] + (shp[-1] // head_dim, head_dim))
    y = xf * lax.rsqrt(jnp.mean(xf * xf, axis=-1, keepdims=True) + EPS)
    return (y.reshape(shp) * gain.astype(F32)).astype(x.dtype)


def causal_depthwise_conv(x, w):
    k_w, ch = w.shape
    return lax.conv_general_dilated(
        x, w[:, None, :].astype(x.dtype), window_strides=(1,), padding=[(k_w - 1, 0)],
        dimension_numbers=('NWC', 'WIO', 'NWC'), feature_group_count=ch)


def to_chunks(t):
    b, s, h, d = t.shape
    return t.reshape(b, s // CHUNK, CHUNK, h, d).transpose(1, 0, 3, 2, 4)


def from_chunks(t):
    nc, b, h, l, d = t.shape
    return t.transpose(1, 0, 3, 2, 4).reshape(b, nc * l, h, d)


def _complex_affine_combine(e1, e2):
    a1r, a1i, b1r, b1i = e1
    a2r, a2i, b2r, b2i = e2
    return (a2r * a1r - a2i * a1i, a2r * a1i + a2i * a1r,
            a2r * b1r - a2i * b1i + b2r, a2r * b1i + a2i * b1r + b2i)


def s5_mixer(u, lam_re, lam_im, b_re, b_im, c_re, c_im, d_skip, log_dt, w_glu, gain):
    bsz, seq, _ = u.shape
    uf = u.astype(F32)
    ug = uf.reshape(bsz, seq, S5_GROUPS, S5_CH)
    lr = jnp.minimum(lam_re.astype(F32), -1e-4)
    li = lam_im.astype(F32)
    dt = jnp.exp(log_dt.astype(F32))[:, None]
    mag = jnp.exp(lr * dt)
    ab_re = mag * jnp.cos(li * dt)
    ab_im = mag * jnp.sin(li * dt)
    den = lr * lr + li * li
    cf_re = ((ab_re - 1.0) * lr + ab_im * li) / den
    cf_im = (ab_im * lr - (ab_re - 1.0) * li) / den
    bre = b_re.astype(F32)
    bim = b_im.astype(F32)
    bb_re = cf_re[..., None] * bre - cf_im[..., None] * bim
    bb_im = cf_re[..., None] * bim + cf_im[..., None] * bre
    bu_re = jnp.einsum('bsgp,gnp->bsgn', ug, bb_re)
    bu_im = jnp.einsum('bsgp,gnp->bsgn', ug, bb_im)
    a_re = jnp.broadcast_to(ab_re, bu_re.shape)
    a_im = jnp.broadcast_to(ab_im, bu_im.shape)
    _, _, x_re, x_im = lax.associative_scan(
        _complex_affine_combine, (a_re, a_im, bu_re, bu_im), axis=1)
    y = (jnp.einsum('bsgn,gpn->bsgp', x_re, c_re.astype(F32))
         - jnp.einsum('bsgn,gpn->bsgp', x_im, c_im.astype(F32))).reshape(bsz, seq, GROUP_W)
    y = y + d_skip.astype(F32) * uf
    g = jax.nn.gelu(y)
    y = g * jax.nn.sigmoid(g @ w_glu.astype(F32))
    return rms_norm(y, gain).astype(u.dtype)


def _hgrn2_chunk(state, inp):
    q, k, v, lf = inp
    l = q.shape[2]
    b = jnp.cumsum(lf, axis=2)
    o_inter = jnp.einsum('bhtk,bhkv->bhtv', q * jnp.exp(b), state)
    causal = jnp.tril(jnp.ones((l, l), dtype=bool))
    diff = b[:, :, :, None, :] - b[:, :, None, :, :]
    decay = jnp.exp(jnp.where(causal[:, :, None], diff, NEG_BIG))
    scores = jnp.einsum('bhtk,bhsk,bhtsk->bhts', q, k, decay)
    o_intra = jnp.einsum('bhts,bhsv->bhtv', scores, v)
    b_last = b[:, :, -1:, :]
    new_state = (jnp.exp(b_last[:, :, 0, :])[..., None] * state
                 + jnp.einsum('bhsk,bhsv->bhkv', k * jnp.exp(b_last - b), v))
    return new_state, o_inter + o_intra


def hgrn2_mixer(q, fz, i, g, lb, gain):
    bsz, seq, _ = q.shape
    z = fz.astype(F32)
    logf = jax.nn.log_sigmoid(z) + jnp.log1p(lb * jnp.exp(jnp.minimum(-z, EXP_CLIP)))
    k = (1.0 - lb) * jax.nn.sigmoid(-z)

    def heads(t):
        return to_chunks(t.astype(F32).reshape(bsz, seq, HG_HEADS, HG_DH))

    s0 = jnp.zeros((bsz, HG_HEADS, HG_DH, HG_DH), F32)
    _, o = lax.scan(_hgrn2_chunk, s0, (heads(q), heads(k), heads(i), heads(logf)))
    o = from_chunks(o).reshape(bsz, seq, GROUP_W)
    o = headwise_rms_norm(o, gain, HG_DH) * jax.nn.silu(g.astype(F32))
    return o.astype(q.dtype)


def fox_mixer(q, k, v, fz, gain):
    bsz, seq, _ = q.shape

    def heads(t):
        return t.reshape(bsz, seq, FOX_HEADS, FOX_DH).transpose(0, 2, 1, 3)

    qh, kh, vh = heads(q), heads(k), heads(v)
    cum_logf = jnp.cumsum(jax.nn.log_sigmoid(fz.astype(F32)), axis=1).transpose(0, 2, 1)
    n_blk = seq // Q_BLOCK
    q_blocks = qh.reshape(bsz, FOX_HEADS, n_blk, Q_BLOCK, FOX_DH).transpose(2, 0, 1, 3, 4)
    f_blocks = cum_logf.reshape(bsz, FOX_HEADS, n_blk, Q_BLOCK).transpose(2, 0, 1, 3)
    k_pos = jnp.arange(seq)
    scale = FOX_DH ** -0.5

    def attend(args):
        blk, qb, fq = args
        q_pos = blk * Q_BLOCK + jnp.arange(Q_BLOCK)
        s = (jnp.einsum('bhqd,bhkd->bhqk', qb, kh).astype(F32) * scale
             + fq[..., None] - cum_logf[:, :, None, :])
        s = jnp.where(k_pos[None, :] <= q_pos[:, None], s, NEG_BIG)
        p = jax.nn.softmax(s, axis=-1)
        return jnp.einsum('bhqk,bhkd->bhqd', p.astype(vh.dtype), vh)

    out = lax.map(attend, (jnp.arange(n_blk), q_blocks, f_blocks))
    out = out.transpose(1, 0, 3, 2, 4).reshape(bsz, seq, GROUP_W)
    return headwise_rms_norm(out, gain, FOX_DH)


def _mlstm_chunk(carry, inp):
    c_mat, n_vec, m_prev = carry
    q, k, v, ig, lf = inp
    l = q.shape[2]
    b = jnp.cumsum(lf, axis=-1)
    causal = jnp.tril(jnp.ones((l, l), dtype=bool))
    d_log = jnp.where(causal, b[..., :, None] - b[..., None, :] + ig[..., None, :], NEG_BIG)
    inter_log = b + m_prev[..., None]
    m_t = jnp.maximum(inter_log, jnp.max(d_log, axis=-1))
    w_inter = jnp.exp(inter_log - m_t)
    qk = jnp.einsum('bhtd,bhsd->bhts', q, k) * jnp.exp(d_log - m_t[..., None])
    num = (w_inter[..., None] * jnp.einsum('bhtk,bhkv->bhtv', q, c_mat)
           + jnp.einsum('bhts,bhsv->bhtv', qk, v))
    den = w_inter * jnp.einsum('bhtk,bhk->bht', q, n_vec) + jnp.sum(qk, axis=-1)
    h = num / jnp.maximum(jnp.abs(den), jnp.exp(-m_t))[..., None]
    b_last = b[..., -1]
    src_log = b_last[..., None] - b + ig
    m_new = jnp.maximum(b_last + m_prev, jnp.max(src_log, axis=-1))
    w_src = jnp.exp(src_log - m_new[..., None])
    decay = jnp.exp(b_last + m_prev - m_new)
    c_new = decay[..., None, None] * c_mat + jnp.einsum('bhs,bhsk,bhsv->bhkv', w_src, k, v)
    n_new = decay[..., None] * n_vec + jnp.einsum('bhs,bhsk->bhk', w_src, k)
    return (c_new, n_new, m_new), h


def mlstm_mixer(q, k, v, o, ig, fz, conv_w, gain):
    bsz, seq, _ = q.shape
    qk = jax.nn.silu(causal_depthwise_conv(jnp.concatenate([q, k], axis=-1), conv_w))
    q_c, k_c = qk[..., :GROUP_W], qk[..., GROUP_W:]

    def heads(t):
        return to_chunks(t.astype(F32).reshape(bsz, seq, ML_HEADS, ML_DH))

    def gates(t):
        return to_chunks(t.astype(F32)[..., None])[..., 0]

    lf = jax.nn.log_sigmoid(fz.astype(F32))
    init = (jnp.zeros((bsz, ML_HEADS, ML_DH, ML_DH), F32),
            jnp.zeros((bsz, ML_HEADS, ML_DH), F32),
            jnp.zeros((bsz, ML_HEADS), F32))
    _, h = lax.scan(_mlstm_chunk, init,
                    (heads(q_c), heads(k_c * ML_DH ** -0.5), heads(v), gates(ig), gates(lf)))
    h = from_chunks(h).reshape(bsz, seq, GROUP_W)
    h = headwise_rms_norm(h, gain, ML_DH) * jax.nn.sigmoid(o.astype(F32))
    return h.astype(q.dtype)


def setup_inputs(seed: int = 0) -> dict:
    key = jax.random.key(seed)
    ks = jax.random.split(key, 26)

    def nrm(k, shape, scale):
        return scale * jax.random.normal(k, shape, F32)

    x = nrm(ks[0], (BATCH, SEQ, D_MODEL), 1.0)
    w_in = nrm(ks[1], (DEPTH, D_MODEL, D_IN), D_MODEL ** -0.5)
    fox_fb = 2.0 + nrm(ks[2], (DEPTH, FOX_HEADS), 0.1)
    ml_ib = nrm(ks[3], (DEPTH, ML_HEADS), 0.1)
    ml_fb = jnp.linspace(3.0, 6.0, ML_HEADS, dtype=F32)[None, :] + nrm(ks[4], (DEPTH, ML_HEADS), 0.1)
    gate_bias = jnp.concatenate([fox_fb, ml_ib, ml_fb], axis=-1)
    n_idx = jnp.arange(S5_STATE, dtype=F32)
    s5_lambda_re = -0.5 + nrm(ks[5], (DEPTH, S5_GROUPS, S5_STATE), 0.01)
    s5_lambda_im = math.pi * n_idx + nrm(ks[6], (DEPTH, S5_GROUPS, S5_STATE), 0.01)
    s5_b_re = nrm(ks[7], (DEPTH, S5_GROUPS, S5_STATE, S5_CH), (2 * S5_CH) ** -0.5)
    s5_b_im = nrm(ks[8], (DEPTH, S5_GROUPS, S5_STATE, S5_CH), (2 * S5_CH) ** -0.5)
    s5_c_re = nrm(ks[9], (DEPTH, S5_GROUPS, S5_CH, S5_STATE), (2 * S5_STATE) ** -0.5)
    s5_c_im = nrm(ks[10], (DEPTH, S5_GROUPS, S5_CH, S5_STATE), (2 * S5_STATE) ** -0.5)
    s5_d = nrm(ks[11], (DEPTH, GROUP_W), 1.0)
    s5_log_dt = jax.random.uniform(ks[12], (DEPTH, S5_GROUPS), F32,
                                   math.log(S5_DT_MIN), math.log(S5_DT_MAX))
    s5_w_glu = nrm(ks[13], (DEPTH, GROUP_W, GROUP_W), GROUP_W ** -0.5)
    hgrn_lb_logits = nrm(ks[14], (DEPTH, GROUP_W), 0.5)
    mlstm_conv_w = nrm(ks[15], (DEPTH, ML_CONV, 2 * GROUP_W), ML_CONV ** -0.5)
    mix_gain = 1.0 + nrm(ks[16], (DEPTH, D_MIX), 0.02)
    w_out = nrm(ks[17], (DEPTH, D_MIX, D_MODEL), D_MIX ** -0.5)
    ln_mix_pre = 1.0 + nrm(ks[18], (DEPTH, D_MODEL), 0.02)
    ln_mix_post = 1.0 + nrm(ks[19], (DEPTH, D_MODEL), 0.02)
    ln_ffn_pre = 1.0 + nrm(ks[20], (DEPTH, D_MODEL), 0.02)
    ln_ffn_post = 1.0 + nrm(ks[21], (DEPTH, D_MODEL), 0.02)
    w_ffn_gate = nrm(ks[22], (DEPTH, D_MODEL, D_FF), D_MODEL ** -0.5)
    w_ffn_up = nrm(ks[23], (DEPTH, D_MODEL, D_FF), D_MODEL ** -0.5)
    w_ffn_down = nrm(ks[24], (DEPTH, D_FF, D_MODEL), D_FF ** -0.5)
    return {'x': x, 'w_in': w_in, 'gate_bias': gate_bias,
            's5_lambda_re': s5_lambda_re, 's5_lambda_im': s5_lambda_im,
            's5_b_re': s5_b_re, 's5_b_im': s5_b_im, 's5_c_re': s5_c_re, 's5_c_im': s5_c_im,
            's5_d': s5_d, 's5_log_dt': s5_log_dt, 's5_w_glu': s5_w_glu,
            'hgrn_lb_logits': hgrn_lb_logits, 'mlstm_conv_w': mlstm_conv_w,
            'mix_gain': mix_gain, 'w_out': w_out,
            'ln_mix_pre': ln_mix_pre, 'ln_mix_post': ln_mix_post,
            'ln_ffn_pre': ln_ffn_pre, 'ln_ffn_post': ln_ffn_post,
            'w_ffn_gate': w_ffn_gate, 'w_ffn_up': w_ffn_up, 'w_ffn_down': w_ffn_down}


def reference(x, w_in, gate_bias, s5_lambda_re, s5_lambda_im, s5_b_re, s5_b_im, s5_c_re, s5_c_im,
              s5_d, s5_log_dt, s5_w_glu, hgrn_lb_logits, mlstm_conv_w, mix_gain, w_out,
              ln_mix_pre, ln_mix_post, ln_ffn_pre, ln_ffn_post, w_ffn_gate, w_ffn_up, w_ffn_down):
    lb_prob = jax.nn.softmax(hgrn_lb_logits.astype(F32), axis=0)
    lb_all = jnp.maximum(jnp.cumsum(lb_prob, axis=0) - lb_prob[0:1], 0.0)

    h = x
    for l in range(DEPTH):
        a = rms_norm(h, ln_mix_pre[l])
        proj = a @ w_in[l]
        (s5_u, hg_q, hg_f, hg_i, hg_g, fox_q, fox_k, fox_v, fox_f,
         ml_q, ml_k, ml_v, ml_o, ml_i, ml_f) = jnp.split(proj, SPLIT_IDX, axis=-1)
        gb = gate_bias[l]
        gain = mix_gain[l]
        y_a = s5_mixer(s5_u, s5_lambda_re[l], s5_lambda_im[l], s5_b_re[l], s5_b_im[l],
                       s5_c_re[l], s5_c_im[l], s5_d[l], s5_log_dt[l], s5_w_glu[l],
                       gain[0:GROUP_W])
        y_b = hgrn2_mixer(hg_q, hg_f, hg_i, hg_g, lb_all[l], gain[GROUP_W:2 * GROUP_W])
        y_c = fox_mixer(fox_q, fox_k, fox_v, fox_f + gb[0:FOX_HEADS],
                        gain[2 * GROUP_W:3 * GROUP_W])
        y_d = mlstm_mixer(ml_q, ml_k, ml_v, ml_o,
                          ml_i + gb[FOX_HEADS:FOX_HEADS + ML_HEADS],
                          ml_f + gb[FOX_HEADS + ML_HEADS:FOX_HEADS + 2 * ML_HEADS],
                          mlstm_conv_w[l], gain[3 * GROUP_W:4 * GROUP_W])
        mix = jnp.concatenate([y_a, y_b, y_c, y_d], axis=-1) @ w_out[l]
        h = h + rms_norm(mix, ln_mix_post[l])

        a = rms_norm(h, ln_ffn_pre[l])
        ff = (jax.nn.silu(a @ w_ffn_gate[l]) * (a @ w_ffn_up[l])) @ w_ffn_down[l]
        h = h + rms_norm(ff, ln_ffn_post[l])
    return h
```

```python
import functools
import math

import numpy as np
import jax
import jax.numpy as jnp
from jax import lax
from jax.experimental import pallas as pl
from jax.experimental.pallas import tpu as pltpu

F32 = jnp.float32
BF16 = jnp.bfloat16

EPS = 1e-6
NEG_BIG = -1e30
EXP_CLIP = 60.0

GROUP_W = 256
HEADS = 4
DH = GROUP_W // HEADS
S5_G, S5_P, S5_N = 16, 16, 64
ML_CONV = 4
CHUNK = 64
HG_LEVELS = (32, 16, 8, 4, 2, 1)
GATE_W = 128

VMEM_LIMIT_BYTES = 56 * 1024 * 1024


def _cparams(*sem):
    return pltpu.CompilerParams(dimension_semantics=sem, vmem_limit_bytes=VMEM_LIMIT_BYTES)


def _dot(a, b):
    return jnp.dot(a.astype(BF16), b.astype(BF16), preferred_element_type=F32)


def _dot_nt(a, b):
    return lax.dot_general(a.astype(BF16), b.astype(BF16), (((1,), (1,)), ((), ())),
                           preferred_element_type=F32)


def _dot_tn(a, b):
    return lax.dot_general(a.astype(BF16), b.astype(BF16), (((0,), (0,)), ((), ())),
                           preferred_element_type=F32)


def _split(x, n):
    parts, r = [], x
    for i in range(n):
        p = r.astype(BF16)
        parts.append(p)
        if i + 1 < n:
            r = r - p.astype(F32)
    return parts


def _sel_dot(m01, x, n=3):
    out = None
    for p in _split(x, n):
        t = jnp.dot(m01, p, preferred_element_type=F32)
        out = t if out is None else out + t
    return out


def _dot_sel(x, m01, n=3):
    out = None
    for p in _split(x, n):
        t = jnp.dot(p, m01, preferred_element_type=F32)
        out = t if out is None else out + t
    return out


def _log_sigmoid(z):
    return jnp.minimum(z, 0.0) - jnp.log(1.0 + jnp.exp(-jnp.abs(z)))


def _sigmoid(z):
    return 1.0 / (1.0 + jnp.exp(-z))


def _rms(x, gain):
    ms = jnp.mean(x * x, axis=-1, keepdims=True)
    return x * lax.rsqrt(ms + EPS) * gain


def _head_rms(o, mh, gain):
    ms = _dot_sel(o * o, mh, 2)
    return o * lax.rsqrt(ms + EPS) * gain


def _block_diag_mask(n):
    r = lax.broadcasted_iota(jnp.int32, (n, n), 0)
    c = lax.broadcasted_iota(jnp.int32, (n, n), 1)
    return (r >> 6) == (c >> 6)


def _tile_heads(x, bd):
    return jnp.where(bd, jnp.concatenate([x, x, x, x], axis=0), 0.0)


def _hgrn_level_mats():
    L = CHUNK
    t = np.arange(L)[:, None]
    j = np.arange(L)[None, :]
    blocks = [j <= t]
    for m in HG_LEVELS:
        ref = (t // (2 * m)) * 2 * m + m - 1
        blocks.append((j > ref) & (j <= t) & ((t & m) != 0))
    for m in HG_LEVELS:
        ref = (t // (2 * m)) * 2 * m + m - 1
        blocks.append((j > t) & (j <= ref) & ((t & m) == 0))
    blocks.append(j > t)
    return np.concatenate(blocks, axis=0).astype(np.float32)


def _head_block(value):
    i = np.arange(GROUP_W)
    return np.where((i[:, None] // DH) == (i[None, :] // DH), value, 0.0).astype(np.float32)


def _gate_expand(col0):
    e = np.zeros((GATE_W, GROUP_W), np.float32)
    for h in range(HEADS):
        e[col0 + h, h * DH:(h + 1) * DH] = 1.0
    return e


def _lb_kernel(logit_ref, o_ref):
    x = logit_ref[...]
    depth = x.shape[0]
    m = x[0:1]
    for l in range(1, depth):
        m = jnp.maximum(m, x[l:l + 1])
    e = [jnp.exp(x[l:l + 1] - m) for l in range(depth)]
    tot = e[0]
    for l in range(1, depth):
        tot = tot + e[l]
    p = [el / tot for el in e]
    c = None
    for l in range(depth):
        c = p[l] if c is None else c + p[l]
        o_ref[l:l + 1, :] = jnp.maximum(c - p[0], 0.0)


def _s5_param_kernel(lr_ref, li_ref, ldt_ref, bre_ref, bim_ref, abr_ref, abi_ref, bbr_ref, bbi_ref):
    lr = jnp.minimum(lr_ref[...], -1e-4)
    li = li_ref[...]
    dt = jnp.exp(ldt_ref[...])
    mag = jnp.exp(lr * dt)
    ab_re = mag * jnp.cos(li * dt)
    ab_im = mag * jnp.sin(li * dt)
    den = lr * lr + li * li
    cf_re = ((ab_re - 1.0) * lr + ab_im * li) / den
    cf_im = (ab_im * lr - (ab_re - 1.0) * li) / den
    bre = bre_ref[...]
    bim = bim_ref[...]
    abr_ref[...] = ab_re
    abi_ref[...] = ab_im
    bbr_ref[...] = cf_re * bre - cf_im * bim
    bbi_ref[...] = cf_re * bim + cf_im * bre


def _s5_params(lam_re, lam_im, b_re, b_im, c_re, c_im, log_dt):
    gp = S5_G * S5_P
    rep = lambda a: jnp.repeat(a, S5_P, axis=0)
    ldt = jnp.broadcast_to(rep(log_dt[:, None]), (gp, S5_N))
    bt = lambda a: a.transpose(0, 2, 1).reshape(gp, S5_N)
    shp = jax.ShapeDtypeStruct((gp, S5_N), F32)
    ab_re, ab_im, bb_re, bb_im = pl.pallas_call(
        _s5_param_kernel, out_shape=(shp, shp, shp, shp), name="s5_params",
    )(rep(lam_re), rep(lam_im), ldt, bt(b_re), bt(b_im))
    lam = jnp.stack([ab_re[::S5_P].reshape(-1), ab_im[::S5_P].reshape(-1)])
    own_gp = jnp.asarray(np.eye(S5_G, dtype=np.float32).repeat(S5_P, axis=0))
    own_gn = jnp.asarray(np.eye(S5_G, dtype=np.float32).repeat(S5_N, axis=0))
    wide = lambda bb: (bb[:, None, :] * own_gp[:, :, None]).reshape(gp, S5_G * S5_N)
    wb = jnp.concatenate([wide(bb_re), wide(bb_im)], axis=1).astype(BF16)
    tall = lambda c: (c.transpose(0, 2, 1).reshape(S5_G * S5_N, S5_P)[:, None, :]
                      * own_gn[:, :, None]).reshape(S5_G * S5_N, gp)
    cm = jnp.concatenate([tall(c_re), -tall(c_im)], axis=0).astype(BF16)
    return lam, wb, cm


IN_COLS = (GROUP_W, 4 * GROUP_W, 3 * GROUP_W, 4 * GROUP_W, GATE_W)


def _in_proj_kernel(x_ref, g_ref, w_ref, bias_ref, o_u5, o_hg, o_fx, o_ml, o_gt):
    a = _rms(x_ref[...], g_ref[...]).astype(BF16)
    outs = (o_u5, o_hg, o_fx, o_ml, o_gt)
    c0 = 0
    for o_ref, width in zip(outs, IN_COLS):
        r = jnp.dot(a, w_ref[:, c0:c0 + width], preferred_element_type=F32)
        if o_ref is o_gt:
            r = r + bias_ref[...]
        o_ref[...] = r
        c0 += width


def _in_proj(h, gain, w_in_l, gate_bias_l, tm):
    t_rows, d = h.shape
    gw = GROUP_W
    o_fox_f = 8 * gw
    o_ml = o_fox_f + HEADS
    o_ml_i = o_ml + 4 * gw
    o_ml_f = o_ml_i + HEADS
    gates = jnp.concatenate([w_in_l[:, o_fox_f:o_fox_f + HEADS], w_in_l[:, o_ml_i:o_ml_i + HEADS],
                             w_in_l[:, o_ml_f:o_ml_f + HEADS],
                             jnp.zeros((d, GATE_W - 3 * HEADS), w_in_l.dtype)], axis=1)
    w = jnp.concatenate([w_in_l[:, :8 * gw], w_in_l[:, o_ml:o_ml + 4 * gw], gates], axis=1).astype(BF16)
    bias = jnp.concatenate([gate_bias_l, jnp.zeros((GATE_W - 3 * HEADS,), F32)])[None, :]
    n_tot = sum(IN_COLS)
    row = lambda i: (i, 0)
    fixed = lambda i: (0, 0)
    return pl.pallas_call(
        _in_proj_kernel,
        grid=(t_rows // tm,),
        in_specs=[pl.BlockSpec((tm, d), row), pl.BlockSpec((1, d), fixed),
                  pl.BlockSpec((d, n_tot), fixed), pl.BlockSpec((1, GATE_W), fixed)],
        out_specs=[pl.BlockSpec((tm, c), row) for c in IN_COLS],
        out_shape=[jax.ShapeDtypeStruct((t_rows, c), F32) for c in IN_COLS],
        compiler_params=_cparams("parallel"),
        name="in_proj",
    )(h, gain[None, :], w, bias)


def _s5_kernel(u_ref, wb_ref, lam_ref, cm_ref, d_ref, wglu_ref, gain_ref, o_ref, xs_ref, st_ref, *, lb):
    nb = u_ref.shape[1]
    ns = S5_G * S5_N

    @pl.when(pl.program_id(0) == 0)
    def _():
        st_ref[...] = jnp.zeros_like(st_ref)

    u = u_ref[...].reshape(lb * nb, GROUP_W)
    xs_ref[...] = jnp.dot(u.astype(BF16), wb_ref[...], preferred_element_type=F32)
    ar = jnp.broadcast_to(lam_ref[0:1, :], (nb, ns))
    ai = jnp.broadcast_to(lam_ref[1:2, :], (nb, ns))

    def step(t, carry):
        xr, xi = carry
        r0 = pl.multiple_of(t * nb, nb)
        nr = ar * xr - ai * xi + xs_ref[pl.ds(r0, nb), 0:ns]
        ni = ar * xi + ai * xr + xs_ref[pl.ds(r0, nb), ns:2 * ns]
        xs_ref[pl.ds(r0, nb), 0:ns] = nr
        xs_ref[pl.ds(r0, nb), ns:2 * ns] = ni
        return nr, ni

    xr, xi = lax.fori_loop(0, lb, step, (st_ref[:, 0:ns], st_ref[:, ns:2 * ns]), unroll=8)
    st_ref[:, 0:ns] = xr
    st_ref[:, ns:2 * ns] = xi

    y = jnp.dot(xs_ref[...].astype(BF16), cm_ref[...], preferred_element_type=F32) + d_ref[...] * u
    g = jax.nn.gelu(y)
    y = g * _sigmoid(_dot(g, wglu_ref[...]))
    o_ref[...] = _rms(y, gain_ref[...]).reshape(lb, nb, GROUP_W).astype(o_ref.dtype)


def _s5_mixer(u_t, lam, wb, cm, d_skip, w_glu, gain, lb):
    s, nb, gw = u_t.shape
    ns2 = 2 * S5_G * S5_N
    fixed = lambda i: (0, 0)
    return pl.pallas_call(
        functools.partial(_s5_kernel, lb=lb),
        grid=(s // lb,),
        in_specs=[pl.BlockSpec((lb, nb, gw), lambda i: (i, 0, 0)),
                  pl.BlockSpec((gw, ns2), fixed), pl.BlockSpec((2, ns2 // 2), fixed),
                  pl.BlockSpec((ns2, gw), fixed), pl.BlockSpec((1, gw), fixed),
                  pl.BlockSpec((gw, gw), fixed), pl.BlockSpec((1, gw), fixed)],
        out_specs=pl.BlockSpec((lb, nb, gw), lambda i: (i, 0, 0)),
        out_shape=jax.ShapeDtypeStruct((s, nb, gw), BF16),
        scratch_shapes=[pltpu.VMEM((lb * nb, ns2), F32), pltpu.VMEM((nb, ns2), F32)],
        compiler_params=_cparams("arbitrary"),
        name="s5_mixer",
    )(u_t, wb, lam, cm, d_skip[None, :], w_glu.astype(BF16), gain[None, :])


def _hgrn_kernel(x_ref, lb_ref, gain_ref, mall_ref, mh_ref, o_ref, st_ref, *, nchunk):
    L = CHUNK
    gw = GROUP_W

    @pl.when(pl.program_id(1) == 0)
    def _():
        st_ref[...] = jnp.zeros_like(st_ref)

    lb = lb_ref[...]
    gain = gain_ref[...]
    bd = _block_diag_mask(gw)
    row = lax.broadcasted_iota(jnp.int32, (L, gw), 0)
    col = lax.broadcasted_iota(jnp.int32, (L, gw), 1) & (DH - 1)

    def chunk(ci, carry):
        r0 = pl.multiple_of(ci * L, L)
        q = x_ref[0, pl.ds(r0, L), 0:gw]
        z = x_ref[0, pl.ds(r0, L), gw:2 * gw]
        v = x_ref[0, pl.ds(r0, L), 2 * gw:3 * gw]
        gg = x_ref[0, pl.ds(r0, L), 3 * gw:4 * gw]
        logf = _log_sigmoid(z) + jnp.log(1.0 + lb * jnp.exp(jnp.minimum(-z, EXP_CLIP)))
        kk = (1.0 - lb) * _sigmoid(-z)
        e_all = _sel_dot(mall_ref[...], logf)
        b = e_all[0:L]
        st = st_ref[...]
        o = _dot_nt(q * jnp.exp(b), st)
        a = jnp.where(row == col, _dot_nt(q, _tile_heads(kk, bd)), 0.0)
        nl = len(HG_LEVELS)
        for li, m in enumerate(HG_LEVELS):
            eq = e_all[(1 + li) * L:(2 + li) * L]
            ek = e_all[(1 + nl + li) * L:(2 + nl + li) * L]
            upper = (row & m) != 0
            ql = jnp.where(upper, q * jnp.exp(eq), 0.0)
            kl = jnp.where(upper, 0.0, kk * jnp.exp(ek))
            sh = int(math.log2(2 * m))
            same = (row >> sh) == (col >> sh)
            a = a + jnp.where(same, _dot_nt(ql, _tile_heads(kl, bd)), 0.0)
        o = o + _dot(a, _tile_heads(v, bd))
        kdec = kk * jnp.exp(e_all[(1 + 2 * nl) * L:(2 + 2 * nl) * L])
        st_ref[...] = st * jnp.exp(b[L - 1:L]) + jnp.where(bd, _dot_tn(v, kdec), 0.0)
        out = _head_rms(o, mh_ref[...], gain) * (gg * _sigmoid(gg))
        o_ref[0, pl.ds(r0, L), :] = out.astype(o_ref.dtype)
        return carry

    lax.fori_loop(0, nchunk, chunk, 0)


def _hgrn_mixer(hg, lb, gain, lg):
    b, s, w = hg.shape
    gw = GROUP_W
    mall = jnp.asarray(_hgrn_level_mats(), BF16)
    mh = jnp.asarray(_head_block(1.0 / DH), BF16)
    fixed = lambda i, j: (0, 0)
    return pl.pallas_call(
        functools.partial(_hgrn_kernel, nchunk=lg // CHUNK),
        grid=(b, s // lg),
        in_specs=[pl.BlockSpec((1, lg, w), lambda i, j: (i, j, 0)),
                  pl.BlockSpec((1, gw), fixed), pl.BlockSpec((1, gw), fixed),
                  pl.BlockSpec(mall.shape, fixed), pl.BlockSpec((gw, gw), fixed)],
        out_specs=pl.BlockSpec((1, lg, gw), lambda i, j: (i, j, 0)),
        out_shape=jax.ShapeDtypeStruct((b, s, gw), BF16),
        scratch_shapes=[pltpu.VMEM((gw, gw), F32)],
        compiler_params=_cparams("parallel", "arbitrary"),
        name="hgrn2_mixer",
    )(hg, lb[None, :], gain[None, :], mall, mh)


def _fox_gate_kernel(g_ref, tri_ref, col_ref, rowo_ref, carry_ref):
    @pl.when(pl.program_id(1) == 0)
    def _():
        carry_ref[...] = jnp.zeros_like(carry_ref)

    lf = _log_sigmoid(g_ref[0])
    cs = _sel_dot(tri_ref[...], lf) + carry_ref[...]
    n = cs.shape[0]
    carry_ref[...] = cs[n - 1:n]
    col_ref[0] = cs
    rowo_ref[0] = cs.T[0:8]


def _fox_gates(gt, lg):
    b, s, w = gt.shape
    tri = jnp.asarray(np.tril(np.ones((lg, lg), np.float32)), BF16)
    return pl.pallas_call(
        _fox_gate_kernel,
        grid=(b, s // lg),
        in_specs=[pl.BlockSpec((1, lg, w), lambda i, j: (i, j, 0)),
                  pl.BlockSpec((lg, lg), lambda i, j: (0, 0))],
        out_specs=[pl.BlockSpec((1, lg, w), lambda i, j: (i, j, 0)),
                   pl.BlockSpec((1, 8, lg), lambda i, j: (i, 0, j))],
        out_shape=[jax.ShapeDtypeStruct((b, s, w), F32), jax.ShapeDtypeStruct((b, 8, s), F32)],
        scratch_shapes=[pltpu.VMEM((1, w), F32)],
        compiler_params=_cparams("parallel", "arbitrary"),
        name="fox_gates",
    )(gt, tri)


def _fox_kernel(q_ref, k_ref, v_ref, cq_ref, ck_ref, gain_ref, mh_ref, o_ref, m_ref, l_ref, acc_ref, *, tq):
    qi = pl.program_id(1)
    kj = pl.program_id(2)
    scale = DH ** -0.5

    @pl.when(kj == 0)
    def _():
        m_ref[...] = jnp.full_like(m_ref, NEG_BIG)
        l_ref[...] = jnp.zeros_like(l_ref)
        acc_ref[...] = jnp.zeros_like(acc_ref)

    @pl.when(kj <= qi)
    def _():
        q_pos = qi * tq + lax.broadcasted_iota(jnp.int32, (tq, tq), 0)
        k_pos = kj * tq + lax.broadcasted_iota(jnp.int32, (tq, tq), 1)
        causal = k_pos <= q_pos
        for h in range(HEADS):
            hs = slice(h * DH, (h + 1) * DH)
            s = _dot_nt(q_ref[0, :, hs] * scale, k_ref[0, :, hs])
            s = s + cq_ref[0, :, h:h + 1] - ck_ref[0, h:h + 1, :]
            s = jnp.where(causal, s, NEG_BIG)
            m_old = m_ref[h]
            m_new = jnp.maximum(m_old, jnp.max(s, axis=-1, keepdims=True))
            p = jnp.exp(s - m_new)
            alpha = jnp.exp(m_old - m_new)
            l_ref[h] = alpha * l_ref[h] + jnp.sum(p, axis=-1, keepdims=True)
            acc_ref[h] = alpha * acc_ref[h] + _dot(p, v_ref[0, :, hs])
            m_ref[h] = m_new

    @pl.when(kj == qi)
    def _():
        out = jnp.concatenate([acc_ref[h] / l_ref[h] for h in range(HEADS)], axis=-1)
        o_ref[0] = _head_rms(out, mh_ref[...], gain_ref[...]).astype(o_ref.dtype)


def _fox_mixer(fx, cf_col, cf_row, gain, tq):
    b, s, _ = fx.shape
    gw = GROUP_W
    mh = jnp.asarray(_head_block(1.0 / DH), BF16)
    nq = s // tq
    kv = lambda col: (lambda i, qi, kj: (i, jnp.minimum(kj, qi), col))
    return pl.pallas_call(
        functools.partial(_fox_kernel, tq=tq),
        grid=(b, nq, nq),
        in_specs=[pl.BlockSpec((1, tq, gw), lambda i, qi, kj: (i, qi, 0)),
                  pl.BlockSpec((1, tq, gw), kv(1)), pl.BlockSpec((1, tq, gw), kv(2)),
                  pl.BlockSpec((1, tq, GATE_W), lambda i, qi, kj: (i, qi, 0)),
                  pl.BlockSpec((1, 8, tq), lambda i, qi, kj: (i, 0, jnp.minimum(kj, qi))),
                  pl.BlockSpec((1, gw), lambda i, qi, kj: (0, 0)),
                  pl.BlockSpec((gw, gw), lambda i, qi, kj: (0, 0))],
        out_specs=pl.BlockSpec((1, tq, gw), lambda i, qi, kj: (i, qi, 0)),
        out_shape=jax.ShapeDtypeStruct((b, s, gw), BF16),
        scratch_shapes=[pltpu.VMEM((HEADS, tq, 1), F32), pltpu.VMEM((HEADS, tq, 1), F32),
                        pltpu.VMEM((HEADS, tq, DH), F32)],
        compiler_params=_cparams("parallel", "parallel", "arbitrary"),
        name="fox_mixer",
    )(fx, fx, fx, cf_col, cf_row, gain[None, :], mh)


def _head_lane_max(x):
    rows = x.shape[0]
    parts = [jnp.broadcast_to(jnp.max(x[:, h * DH:(h + 1) * DH], axis=-1, keepdims=True), (rows, DH))
             for h in range(HEADS)]
    return jnp.concatenate(parts, axis=-1)


def _mlstm_kernel(x_ref, gt_ref, cw_ref, gain_ref, tri_ref, eb_ref, ei_ref, ones_ref, mh_ref, o_ref,
                  cbuf, qk_s, ct_ref, n_ref, m_ref, *, nchunk, lg):
    L = CHUNK
    gw = GROUP_W

    @pl.when(pl.program_id(1) == 0)
    def _():
        cbuf[0:8, :] = jnp.zeros((8, 2 * gw), F32)
        ct_ref[...] = jnp.zeros_like(ct_ref)
        n_ref[...] = jnp.zeros_like(n_ref)
        m_ref[...] = jnp.zeros_like(m_ref)

    cbuf[8:8 + lg, :] = x_ref[0, :, 0:2 * gw]
    acc = None
    for j in range(ML_CONV):
        term = cbuf[pl.ds(8 - (ML_CONV - 1) + j, lg), :] * cw_ref[j:j + 1, :]
        acc = term if acc is None else acc + term
    cbuf[0:8, :] = cbuf[lg:lg + 8, :]
    qk_s[...] = acc * _sigmoid(acc)

    bd = _block_diag_mask(gw)
    row = lax.broadcasted_iota(jnp.int32, (L, gw), 0)
    col = lax.broadcasted_iota(jnp.int32, (L, gw), 1) & (DH - 1)
    causal = col <= row
    diag = col == row

    def chunk(ci, carry):
        r0 = pl.multiple_of(ci * L, L)
        q = qk_s[pl.ds(r0, L), 0:gw]
        k = qk_s[pl.ds(r0, L), gw:2 * gw] * (DH ** -0.5)
        v = x_ref[0, pl.ds(r0, L), 2 * gw:3 * gw]
        og = x_ref[0, pl.ds(r0, L), 3 * gw:4 * gw]
        g = gt_ref[0, pl.ds(r0, L), :]
        cs = _sel_dot(tri_ref[...], _log_sigmoid(g))
        b_exp = _dot_sel(cs, eb_ref[...])
        imb = _dot_sel(g, ei_ref[...]) - b_exp
        imb_row = jnp.sum(jnp.where(diag, imb, 0.0), axis=0, keepdims=True)
        m_prev = m_ref[...]
        d_log = jnp.where(causal, b_exp + imb_row, NEG_BIG)
        inter = b_exp + m_prev
        m_t = jnp.maximum(inter, _head_lane_max(d_log))
        w_inter = jnp.exp(inter - m_t)
        qk = _dot_nt(q, _tile_heads(k, bd)) * jnp.exp(d_log - m_t)
        ct = ct_ref[...]
        n_row = n_ref[...]
        num = w_inter * _dot(q, ct) + _dot(qk, _tile_heads(v, bd))
        den = w_inter * _dot_sel(q * n_row, ones_ref[...], 2) + _dot_sel(qk, ones_ref[...], 2)
        hh = num / jnp.maximum(jnp.abs(den), jnp.exp(-m_t))
        out = _head_rms(hh, mh_ref[...], gain_ref[...]) * _sigmoid(og)
        o_ref[0, pl.ds(r0, L), :] = out.astype(o_ref.dtype)
        b_last = b_exp[L - 1:L]
        src = b_last + imb
        m_new = jnp.maximum(b_last + m_prev, jnp.max(src, axis=0, keepdims=True))
        decay = jnp.exp(b_last + m_prev - m_new)
        kw = k * jnp.exp(src - m_new)
        ct_ref[...] = ct * decay + jnp.where(bd, _dot_tn(kw, v), 0.0)
        n_ref[...] = n_row * decay + jnp.sum(kw, axis=0, keepdims=True)
        m_ref[...] = m_new
        return carry

    lax.fori_loop(0, nchunk, chunk, 0)


def _mlstm_mixer(ml, gt, conv_w, gain, lg):
    b, s, w = ml.shape
    gw = GROUP_W
    tri = jnp.asarray(np.tril(np.ones((CHUNK, CHUNK), np.float32)), BF16)
    eb = jnp.asarray(_gate_expand(2 * HEADS), BF16)
    ei = jnp.asarray(_gate_expand(HEADS), BF16)
    ones = jnp.asarray(_head_block(1.0), BF16)
    mh = jnp.asarray(_head_block(1.0 / DH), BF16)
    fixed = lambda i, j: (0, 0)
    blk = lambda i, j: (i, j, 0)
    return pl.pallas_call(
        functools.partial(_mlstm_kernel, nchunk=lg // CHUNK, lg=lg),
        grid=(b, s // lg),
        in_specs=[pl.BlockSpec((1, lg, w), blk), pl.BlockSpec((1, lg, GATE_W), blk),
                  pl.BlockSpec((ML_CONV, 2 * gw), fixed), pl.BlockSpec((1, gw), fixed),
                  pl.BlockSpec((CHUNK, CHUNK), fixed), pl.BlockSpec((GATE_W, gw), fixed),
                  pl.BlockSpec((GATE_W, gw), fixed), pl.BlockSpec((gw, gw), fixed),
                  pl.BlockSpec((gw, gw), fixed)],
        out_specs=pl.BlockSpec((1, lg, gw), blk),
        out_shape=jax.ShapeDtypeStruct((b, s, gw), BF16),
        scratch_shapes=[pltpu.VMEM((lg + 8, 2 * gw), F32), pltpu.VMEM((lg, 2 * gw), F32),
                        pltpu.VMEM((gw, gw), F32), pltpu.VMEM((1, gw), F32), pltpu.VMEM((1, gw), F32)],
        compiler_params=_cparams("parallel", "arbitrary"),
        name="mlstm_mixer",
    )(ml, gt, conv_w, gain[None, :], tri, eb, ei, ones, mh)


def _post_kernel(ya_ref, yb_ref, yc_ref, yd_ref, h_ref, wo_ref, gpost_ref, gpre_ref, wg_ref, wu_ref, wd_ref,
                 gffn_ref, o_ref, *, ff_chunk):
    gw = GROUP_W
    mix = None
    for i, y_ref in enumerate((ya_ref, yb_ref, yc_ref, yd_ref)):
        t = jnp.dot(y_ref[...], wo_ref[i * gw:(i + 1) * gw, :], preferred_element_type=F32)
        mix = t if mix is None else mix + t
    h1 = h_ref[...] + _rms(mix, gpost_ref[...])
    a = _rms(h1, gpre_ref[...]).astype(BF16)
    d_ff = wg_ref.shape[1]
    ff = None
    for c0 in range(0, d_ff, ff_chunk):
        g = jnp.dot(a, wg_ref[:, c0:c0 + ff_chunk], preferred_element_type=F32)
        u = jnp.dot(a, wu_ref[:, c0:c0 + ff_chunk], preferred_element_type=F32)
        act = (g * _sigmoid(g) * u).astype(BF16)
        t = jnp.dot(act, wd_ref[c0:c0 + ff_chunk, :], preferred_element_type=F32)
        ff = t if ff is None else ff + t
    o_ref[...] = h1 + _rms(ff, gffn_ref[...])


def _post(ya, yb, yc, yd, h, w_out, g_post, g_pre, w_gate, w_up, w_down, g_ffn, tm):
    t_rows, d = h.shape
    gw = GROUP_W
    d_ff = w_gate.shape[1]
    ff_chunk = d_ff // 2 if (d_ff // 2) % 128 == 0 else d_ff
    row = lambda i: (i, 0)
    fixed = lambda i: (0, 0)
    once = pl.Buffered(1)
    wspec = lambda shape: pl.BlockSpec(shape, fixed, pipeline_mode=once)
    gspec = pl.BlockSpec((1, d), fixed)
    return pl.pallas_call(
        functools.partial(_post_kernel, ff_chunk=ff_chunk),
        grid=(t_rows // tm,),
        in_specs=[pl.BlockSpec((tm, gw), row)] * 4
        + [pl.BlockSpec((tm, d), row), wspec((d, d)), gspec, gspec,
           wspec((d, d_ff)), wspec((d, d_ff)), wspec((d_ff, d)), gspec],
        out_specs=pl.BlockSpec((tm, d), row),
        out_shape=jax.ShapeDtypeStruct((t_rows, d), F32),
        compiler_params=_cparams("parallel"),
        name="out_proj_ffn",
    )(ya, yb, yc, yd, h, w_out.astype(BF16), g_post[None, :], g_pre[None, :],
      w_gate.astype(BF16), w_up.astype(BF16), w_down.astype(BF16), g_ffn[None, :])


def kernel(x, w_in, gate_bias, s5_lambda_re, s5_lambda_im, s5_b_re, s5_b_im, s5_c_re, s5_c_im, s5_d, s5_log_dt,
           s5_w_glu, hgrn_lb_logits, mlstm_conv_w, mix_gain, w_out, ln_mix_pre, ln_mix_post, ln_ffn_pre,
           ln_ffn_post, w_ffn_gate, w_ffn_up, w_ffn_down):
    bsz, seq, d = x.shape
    depth = w_in.shape[0]
    gw = GROUP_W
    tm = min(512, seq)
    lg = min(512, seq)
    s5_lb = min(64, seq)

    lb_all = pl.pallas_call(_lb_kernel, out_shape=jax.ShapeDtypeStruct(hgrn_lb_logits.shape, F32),
                            name="hgrn_lower_bounds")(hgrn_lb_logits)

    h = x.reshape(bsz * seq, d)
    for l in range(depth):
        gain = mix_gain[l]
        u5, hg, fx, ml, gt = _in_proj(h, ln_mix_pre[l], w_in[l], gate_bias[l], tm)
        lam, wb, cm = _s5_params(s5_lambda_re[l], s5_lambda_im[l], s5_b_re[l], s5_b_im[l],
                                 s5_c_re[l], s5_c_im[l], s5_log_dt[l])
        u_t = u5.reshape(bsz, seq, gw).transpose(1, 0, 2)
        ya_t = _s5_mixer(u_t, lam, wb, cm, s5_d[l], s5_w_glu[l], gain[0:gw], s5_lb)
        ya = ya_t.transpose(1, 0, 2).reshape(bsz * seq, gw)
        yb = _hgrn_mixer(hg.reshape(bsz, seq, 4 * gw), lb_all[l], gain[gw:2 * gw], lg)
        gt3 = gt.reshape(bsz, seq, GATE_W)
        cf_col, cf_row = _fox_gates(gt3, lg)
        yc = _fox_mixer(fx.reshape(bsz, seq, 3 * gw), cf_col, cf_row, gain[2 * gw:3 * gw], lg)
        yd = _mlstm_mixer(ml.reshape(bsz, seq, 4 * gw), gt3, mlstm_conv_w[l], gain[3 * gw:4 * gw], lg)
        h = _post(ya, yb.reshape(bsz * seq, gw), yc.reshape(bsz * seq, gw), yd.reshape(bsz * seq, gw), h,
                  w_out[l], ln_mix_post[l], ln_ffn_pre[l], w_ffn_gate[l], w_ffn_up[l], w_ffn_down[l],
                  ln_ffn_post[l], tm)
    return h.reshape(bsz, seq, d)
```

```python
import functools
import math

import numpy as np
import jax
import jax.numpy as jnp
from jax import lax
from jax.experimental import pallas as pl
from jax.experimental.pallas import tpu as pltpu

F32 = jnp.float32
BF16 = jnp.bfloat16

EPS = 1e-6
NEG_BIG = -1e30
EXP_CLIP = 60.0

GROUP_W = 256
HEADS = 4
DH = GROUP_W // HEADS
S5_G, S5_P, S5_N = 16, 16, 64
ML_CONV = 4
CHUNK = 64
HG_LEVELS = (32, 16, 8, 4, 2, 1)
HG_SMALL_LEVELS = (4, 2, 1)
GATE_W = 128

VMEM_LIMIT_BYTES = 56 * 1024 * 1024


def _cparams(*sem):
    return pltpu.CompilerParams(dimension_semantics=sem, vmem_limit_bytes=VMEM_LIMIT_BYTES)


def _dot(a, b):
    return jnp.dot(a.astype(BF16), b.astype(BF16), preferred_element_type=F32)


def _dot_nt(a, b):
    return lax.dot_general(a.astype(BF16), b.astype(BF16), (((1,), (1,)), ((), ())),
                           preferred_element_type=F32)


def _dot_tn(a, b):
    return lax.dot_general(a.astype(BF16), b.astype(BF16), (((0,), (0,)), ((), ())),
                           preferred_element_type=F32)


def _split(x, n):
    parts, r = [], x
    for i in range(n):
        p = r.astype(BF16)
        parts.append(p)
        if i + 1 < n:
            r = r - p.astype(F32)
    return parts


def _sel_dot(m01, x, n=3):
    out = None
    for p in _split(x, n):
        t = jnp.dot(m01, p, preferred_element_type=F32)
        out = t if out is None else out + t
    return out


def _dot_sel(x, m01, n=3):
    out = None
    for p in _split(x, n):
        t = jnp.dot(p, m01, preferred_element_type=F32)
        out = t if out is None else out + t
    return out


def _log_sigmoid(z):
    return jnp.minimum(z, 0.0) - jnp.log(1.0 + jnp.exp(-jnp.abs(z)))


def _sigmoid(z):
    return 1.0 / (1.0 + jnp.exp(-z))


def _rms(x, gain):
    ms = jnp.mean(x * x, axis=-1, keepdims=True)
    return x * lax.rsqrt(ms + EPS) * gain


def _head_rms(o, mh, gain):
    ms = _dot_sel(o * o, mh, 2)
    return o * lax.rsqrt(ms + EPS) * gain


def _block_diag_mask(n):
    r = lax.broadcasted_iota(jnp.int32, (n, n), 0)
    c = lax.broadcasted_iota(jnp.int32, (n, n), 1)
    return (r >> 6) == (c >> 6)


def _tile_heads(x, bd):
    return jnp.where(bd, jnp.concatenate([x, x, x, x], axis=0), 0.0)


def _hgrn_level_mats():
    L = CHUNK
    t = np.arange(L)[:, None]
    j = np.arange(L)[None, :]
    blocks = [j <= t]
    for m in HG_SMALL_LEVELS:
        ref = (t // (2 * m)) * 2 * m + m - 1
        blocks.append(j <= ref)
    return np.concatenate(blocks, axis=0).astype(np.float32)


def _head_block(value):
    i = np.arange(GROUP_W)
    return np.where((i[:, None] // DH) == (i[None, :] // DH), value, 0.0).astype(np.float32)


def _gate_expand(col0):
    e = np.zeros((GATE_W, GROUP_W), np.float32)
    for h in range(HEADS):
        e[col0 + h, h * DH:(h + 1) * DH] = 1.0
    return e


def _lb_kernel(logit_ref, o_ref):
    x = logit_ref[...]
    depth = x.shape[0]
    m = x[0:1]
    for l in range(1, depth):
        m = jnp.maximum(m, x[l:l + 1])
    e = [jnp.exp(x[l:l + 1] - m) for l in range(depth)]
    tot = e[0]
    for l in range(1, depth):
        tot = tot + e[l]
    p = [el / tot for el in e]
    c = None
    for l in range(depth):
        c = p[l] if c is None else c + p[l]
        o_ref[l:l + 1, :] = jnp.maximum(c - p[0], 0.0)


def _s5_param_kernel(lr_ref, li_ref, ldt_ref, bre_ref, bim_ref, abr_ref, abi_ref, bbr_ref, bbi_ref):
    lr = jnp.minimum(lr_ref[...], -1e-4)
    li = li_ref[...]
    dt = jnp.exp(ldt_ref[...])
    mag = jnp.exp(lr * dt)
    ab_re = mag * jnp.cos(li * dt)
    ab_im = mag * jnp.sin(li * dt)
    den = lr * lr + li * li
    cf_re = ((ab_re - 1.0) * lr + ab_im * li) / den
    cf_im = (ab_im * lr - (ab_re - 1.0) * li) / den
    bre = bre_ref[...]
    bim = bim_ref[...]
    abr_ref[...] = ab_re
    abi_ref[...] = ab_im
    bbr_ref[...] = cf_re * bre - cf_im * bim
    bbi_ref[...] = cf_re * bim + cf_im * bre


def _s5_params(lam_re, lam_im, b_re, b_im, c_re, c_im, log_dt):
    gp = S5_G * S5_P
    rep = lambda a: jnp.repeat(a, S5_P, axis=0)
    ldt = jnp.broadcast_to(rep(log_dt[:, None]), (gp, S5_N))
    bt = lambda a: a.transpose(0, 2, 1).reshape(gp, S5_N)
    shp = jax.ShapeDtypeStruct((gp, S5_N), F32)
    ab_re, ab_im, bb_re, bb_im = pl.pallas_call(
        _s5_param_kernel, out_shape=(shp, shp, shp, shp), name="s5_params",
    )(rep(lam_re), rep(lam_im), ldt, bt(b_re), bt(b_im))
    lam = jnp.stack([ab_re[::S5_P].reshape(-1), ab_im[::S5_P].reshape(-1)])
    own_gp = jnp.asarray(np.eye(S5_G, dtype=np.float32).repeat(S5_P, axis=0))
    own_gn = jnp.asarray(np.eye(S5_G, dtype=np.float32).repeat(S5_N, axis=0))
    wide = lambda bb: (bb[:, None, :] * own_gp[:, :, None]).reshape(gp, S5_G * S5_N)
    wb = jnp.concatenate([wide(bb_re), wide(bb_im)], axis=1).astype(BF16)
    tall = lambda c: (c.transpose(0, 2, 1).reshape(S5_G * S5_N, S5_P)[:, None, :]
                      * own_gn[:, :, None]).reshape(S5_G * S5_N, gp)
    cm = jnp.concatenate([tall(c_re), -tall(c_im)], axis=0).astype(BF16)
    return lam, wb, cm


IN_COLS = (GROUP_W, 4 * GROUP_W, GROUP_W, 4 * GROUP_W, GATE_W)
IN_DTYPES = (F32, F32, BF16, F32, F32)


def _in_proj_kernel(x_ref, g_ref, w_ref, wt_ref, bias_ref, o_u5, o_hg, o_fk, o_ml, o_gt, o_fqt, o_fvt):
    gw = GROUP_W
    a = _rms(x_ref[...], g_ref[...]).astype(BF16)
    outs = (o_u5, o_hg, o_fk, o_ml, o_gt)
    c0 = 0
    for o_ref, width in zip(outs, IN_COLS):
        r = jnp.dot(a, w_ref[:, c0:c0 + width], preferred_element_type=F32)
        if o_ref is o_gt:
            r = r + bias_ref[...]
        o_ref[...] = r.astype(o_ref.dtype)
        c0 += width
    nt = (((1,), (1,)), ((), ()))
    o_fqt[...] = lax.dot_general(wt_ref[0:gw, :], a, nt, preferred_element_type=F32)
    o_fvt[...] = lax.dot_general(wt_ref[gw:2 * gw, :], a, nt, preferred_element_type=F32).astype(o_fvt.dtype)


def _in_proj(h, gain, w_in_l, gate_bias_l, tm):
    t_rows, d = h.shape
    gw = GROUP_W
    o_fox_f = 8 * gw
    o_ml = o_fox_f + HEADS
    o_ml_i = o_ml + 4 * gw
    o_ml_f = o_ml_i + HEADS
    gates = jnp.concatenate([w_in_l[:, o_fox_f:o_fox_f + HEADS], w_in_l[:, o_ml_i:o_ml_i + HEADS],
                             w_in_l[:, o_ml_f:o_ml_f + HEADS],
                             jnp.zeros((d, GATE_W - 3 * HEADS), w_in_l.dtype)], axis=1)
    w = jnp.concatenate([w_in_l[:, :5 * gw], w_in_l[:, 6 * gw:7 * gw], w_in_l[:, o_ml:o_ml + 4 * gw], gates],
                        axis=1).astype(BF16)
    wt = jnp.concatenate([w_in_l[:, 5 * gw:6 * gw], w_in_l[:, 7 * gw:8 * gw]], axis=1).T.astype(BF16)
    bias = jnp.concatenate([gate_bias_l, jnp.zeros((GATE_W - 3 * HEADS,), F32)])[None, :]
    n_tot = sum(IN_COLS)
    row = lambda i: (i, 0)
    colb = lambda i: (0, i)
    fixed = lambda i: (0, 0)
    return pl.pallas_call(
        _in_proj_kernel,
        grid=(t_rows // tm,),
        in_specs=[pl.BlockSpec((tm, d), row), pl.BlockSpec((1, d), fixed),
                  pl.BlockSpec((d, n_tot), fixed), pl.BlockSpec((2 * gw, d), fixed),
                  pl.BlockSpec((1, GATE_W), fixed)],
        out_specs=[pl.BlockSpec((tm, c), row) for c in IN_COLS]
        + [pl.BlockSpec((gw, tm), colb), pl.BlockSpec((gw, tm), colb)],
        out_shape=[jax.ShapeDtypeStruct((t_rows, c), dt) for c, dt in zip(IN_COLS, IN_DTYPES)]
        + [jax.ShapeDtypeStruct((gw, t_rows), F32), jax.ShapeDtypeStruct((gw, t_rows), BF16)],
        compiler_params=_cparams("parallel"),
        name="in_proj",
    )(h, gain[None, :], w, wt, bias)


def _s5_kernel(u_ref, wb_ref, lam_ref, cm_ref, d_ref, wglu_ref, gain_ref, o_ref, xs_ref, st_ref, *, lb):
    nb = u_ref.shape[1]
    ns = S5_G * S5_N

    @pl.when(pl.program_id(0) == 0)
    def _():
        st_ref[...] = jnp.zeros_like(st_ref)

    u = u_ref[...].reshape(lb * nb, GROUP_W)
    xs_ref[...] = jnp.dot(u.astype(BF16), wb_ref[...], preferred_element_type=F32)
    ar = jnp.broadcast_to(lam_ref[0:1, :], (nb, ns))
    ai = jnp.broadcast_to(lam_ref[1:2, :], (nb, ns))

    def step(t, carry):
        xr, xi = carry
        r0 = pl.multiple_of(t * nb, nb)
        nr = ar * xr - ai * xi + xs_ref[pl.ds(r0, nb), 0:ns]
        ni = ar * xi + ai * xr + xs_ref[pl.ds(r0, nb), ns:2 * ns]
        xs_ref[pl.ds(r0, nb), 0:ns] = nr
        xs_ref[pl.ds(r0, nb), ns:2 * ns] = ni
        return nr, ni

    xr, xi = lax.fori_loop(0, lb, step, (st_ref[:, 0:ns], st_ref[:, ns:2 * ns]), unroll=8)
    st_ref[:, 0:ns] = xr
    st_ref[:, ns:2 * ns] = xi

    y = jnp.dot(xs_ref[...].astype(BF16), cm_ref[...], preferred_element_type=F32) + d_ref[...] * u
    g = jax.nn.gelu(y)
    y = g * _sigmoid(_dot(g, wglu_ref[...]))
    o_ref[...] = _rms(y, gain_ref[...]).reshape(lb, nb, GROUP_W).astype(o_ref.dtype)


def _s5_mixer(u_t, lam, wb, cm, d_skip, w_glu, gain, lb):
    s, nb, gw = u_t.shape
    ns2 = 2 * S5_G * S5_N
    fixed = lambda i: (0, 0)
    return pl.pallas_call(
        functools.partial(_s5_kernel, lb=lb),
        grid=(s // lb,),
        in_specs=[pl.BlockSpec((lb, nb, gw), lambda i: (i, 0, 0)),
                  pl.BlockSpec((gw, ns2), fixed), pl.BlockSpec((2, ns2 // 2), fixed),
                  pl.BlockSpec((ns2, gw), fixed), pl.BlockSpec((1, gw), fixed),
                  pl.BlockSpec((gw, gw), fixed), pl.BlockSpec((1, gw), fixed)],
        out_specs=pl.BlockSpec((lb, nb, gw), lambda i: (i, 0, 0)),
        out_shape=jax.ShapeDtypeStruct((s, nb, gw), BF16),
        scratch_shapes=[pltpu.VMEM((lb * nb, ns2), F32), pltpu.VMEM((nb, ns2), F32)],
        compiler_params=_cparams("arbitrary"),
        name="s5_mixer",
    )(u_t, wb, lam, cm, d_skip[None, :], w_glu.astype(BF16), gain[None, :])


def _hgrn_kernel(x_ref, lb_ref, gain_ref, mall_ref, mh_ref, o_ref, st_ref, *, nchunk):
    L = CHUNK
    gw = GROUP_W

    @pl.when(pl.program_id(1) == 0)
    def _():
        st_ref[...] = jnp.zeros_like(st_ref)

    lb = lb_ref[...]
    gain = gain_ref[...]
    bd = _block_diag_mask(gw)
    row = lax.broadcasted_iota(jnp.int32, (L, gw), 0)
    col = lax.broadcasted_iota(jnp.int32, (L, gw), 1) & (DH - 1)

    def chunk(ci, carry):
        r0 = pl.multiple_of(ci * L, L)
        q = x_ref[0, pl.ds(r0, L), 0:gw]
        z = x_ref[0, pl.ds(r0, L), gw:2 * gw]
        v = x_ref[0, pl.ds(r0, L), 2 * gw:3 * gw]
        gg = x_ref[0, pl.ds(r0, L), 3 * gw:4 * gw]
        logf = _log_sigmoid(z) + jnp.log(1.0 + lb * jnp.exp(jnp.minimum(-z, EXP_CLIP)))
        kk = (1.0 - lb) * _sigmoid(-z)
        cums = _sel_dot(mall_ref[...], logf)
        b = cums[0:L]
        st = st_ref[...]
        o = _dot_nt(q * jnp.exp(b), st)
        a = jnp.where(row == col, _dot_nt(q, _tile_heads(kk, bd)), 0.0)
        for m in HG_LEVELS:
            if m in HG_SMALL_LEVELS:
                i = 1 + HG_SMALL_LEVELS.index(m)
                b_ref = cums[i * L:(i + 1) * L]
            else:
                b_ref = jnp.concatenate([jnp.broadcast_to(b[c + m - 1:c + m], (2 * m, gw))
                                         for c in range(0, L, 2 * m)], axis=0)
            upper = (row & m) != 0
            w = jnp.where(upper, q, kk) * jnp.exp(-jnp.abs(b - b_ref))
            ql = jnp.where(upper, w, 0.0)
            kl = jnp.where(upper, 0.0, w)
            sh = int(math.log2(2 * m))
            same = (row >> sh) == (col >> sh)
            a = a + jnp.where(same, _dot_nt(ql, _tile_heads(kl, bd)), 0.0)
        o = o + _dot(a, _tile_heads(v, bd))
        b_last = b[L - 1:L]
        kdec = kk * jnp.exp(b_last - b)
        st_ref[...] = st * jnp.exp(b_last) + jnp.where(bd, _dot_tn(v, kdec), 0.0)
        out = _head_rms(o, mh_ref[...], gain) * (gg * _sigmoid(gg))
        o_ref[0, pl.ds(r0, L), :] = out.astype(o_ref.dtype)
        return carry

    lax.fori_loop(0, nchunk, chunk, 0, unroll=2)


def _hgrn_mixer(hg, lb, gain, lg):
    b, s, w = hg.shape
    gw = GROUP_W
    mall = jnp.asarray(_hgrn_level_mats(), BF16)
    mh = jnp.asarray(_head_block(1.0 / DH), BF16)
    fixed = lambda i, j: (0, 0)
    return pl.pallas_call(
        functools.partial(_hgrn_kernel, nchunk=lg // CHUNK),
        grid=(b, s // lg),
        in_specs=[pl.BlockSpec((1, lg, w), lambda i, j: (i, j, 0)),
                  pl.BlockSpec((1, gw), fixed), pl.BlockSpec((1, gw), fixed),
                  pl.BlockSpec(mall.shape, fixed), pl.BlockSpec((gw, gw), fixed)],
        out_specs=pl.BlockSpec((1, lg, gw), lambda i, j: (i, j, 0)),
        out_shape=jax.ShapeDtypeStruct((b, s, gw), BF16),
        scratch_shapes=[pltpu.VMEM((gw, gw), F32)],
        compiler_params=_cparams("parallel", "arbitrary"),
        name="hgrn2_mixer",
    )(hg, lb[None, :], gain[None, :], mall, mh)


def _fox_gate_kernel(g_ref, tri_ref, col_ref, rowo_ref, carry_ref):
    @pl.when(pl.program_id(1) == 0)
    def _():
        carry_ref[...] = jnp.zeros_like(carry_ref)

    lf = _log_sigmoid(g_ref[0])
    cs = _sel_dot(tri_ref[...], lf) + carry_ref[...]
    n = cs.shape[0]
    carry_ref[...] = cs[n - 1:n]
    col_ref[0] = cs
    rowo_ref[0] = cs.T[0:8]


def _fox_gates(gt, lg):
    b, s, w = gt.shape
    tri = jnp.asarray(np.tril(np.ones((lg, lg), np.float32)), BF16)
    return pl.pallas_call(
        _fox_gate_kernel,
        grid=(b, s // lg),
        in_specs=[pl.BlockSpec((1, lg, w), lambda i, j: (i, j, 0)),
                  pl.BlockSpec((lg, lg), lambda i, j: (0, 0))],
        out_specs=[pl.BlockSpec((1, lg, w), lambda i, j: (i, j, 0)),
                   pl.BlockSpec((1, 8, lg), lambda i, j: (i, 0, j))],
        out_shape=[jax.ShapeDtypeStruct((b, s, w), F32), jax.ShapeDtypeStruct((b, 8, s), F32)],
        scratch_shapes=[pltpu.VMEM((1, w), F32)],
        compiler_params=_cparams("parallel", "arbitrary"),
        name="fox_gates",
    )(gt, tri)


def _fox_kernel(qt_ref, k_ref, vt_ref, ccol_ref, crow_ref, gain_ref, o_ref, acc_ref, mu_ref, l_ref, *, tq):
    qi = pl.program_id(1)
    gw = GROUP_W
    log2e = math.log2(math.e)
    qt = (qt_ref[...] * (DH ** -0.5 * log2e)).astype(BF16)
    head_row = lax.broadcasted_iota(jnp.int32, (gw, tq), 0) >> 6
    q_heads = [jnp.where(head_row == h, qt, jnp.zeros_like(qt)) for h in range(HEADS)]
    key_i = lax.broadcasted_iota(jnp.int32, (tq, tq), 0)
    qry_i = lax.broadcasted_iota(jnp.int32, (tq, tq), 1)
    acc_ref[...] = jnp.zeros_like(acc_ref)
    mu_ref[...] = jnp.full_like(mu_ref, NEG_BIG)
    l_ref[...] = jnp.zeros_like(l_ref)

    def tile(j, on_diagonal):
        r0 = pl.multiple_of(j * tq, tq)
        kt = k_ref[pl.ds(r0, tq), :]
        for h in range(HEADS):
            hr = slice(h, h + 1)
            hd = slice(h * DH, (h + 1) * DH)
            st = jnp.dot(kt, q_heads[h], preferred_element_type=F32)
            u = st - ccol_ref[0, pl.ds(r0, tq), hr] * log2e
            if on_diagonal:
                u = jnp.where(key_i <= qry_i, u, NEG_BIG)
            cq = crow_ref[0, hr, :] * log2e
            m_old = mu_ref[hr, :]
            m_new = jnp.maximum(m_old, jnp.max(u, axis=0, keepdims=True) + cq)
            alpha = jnp.exp2(m_old - m_new)
            p = jnp.exp2(u + (cq - m_new))
            l_ref[hr, :] = alpha * l_ref[hr, :] + jnp.sum(p, axis=0, keepdims=True)
            pv = jnp.dot(vt_ref[hd, pl.ds(r0, tq)], p.astype(BF16), preferred_element_type=F32)
            acc_ref[hd, :] = alpha * acc_ref[hd, :] + pv
            mu_ref[hr, :] = m_new

    def body(j, carry):
        tile(j, False)
        return carry

    lax.fori_loop(0, qi, body, 0)
    tile(qi, True)

    outs = []
    for h in range(HEADS):
        o = acc_ref[h * DH:(h + 1) * DH, :] / l_ref[h:h + 1, :]
        ms = jnp.mean(o * o, axis=0, keepdims=True)
        outs.append(o * lax.rsqrt(ms + EPS))
    out = jnp.concatenate(outs, axis=0) * gain_ref[...]
    o_ref[...] = out.T.astype(o_ref.dtype)


def _fox_mixer(fqt, fk, fvt, cf_col, cf_row, gain, bsz, tq):
    gw, t_rows = fqt.shape
    s = t_rows // bsz
    nq = s // tq
    return pl.pallas_call(
        functools.partial(_fox_kernel, tq=tq),
        grid=(bsz, nq),
        in_specs=[pl.BlockSpec((gw, tq), lambda i, qi: (0, i * nq + qi)),
                  pl.BlockSpec((s, gw), lambda i, qi: (i, 0)),
                  pl.BlockSpec((gw, s), lambda i, qi: (0, i)),
                  pl.BlockSpec((1, s, GATE_W), lambda i, qi: (i, 0, 0)),
                  pl.BlockSpec((1, 8, tq), lambda i, qi: (i, 0, qi)),
                  pl.BlockSpec((gw, 1), lambda i, qi: (0, 0))],
        out_specs=pl.BlockSpec((tq, gw), lambda i, qi: (i * nq + qi, 0)),
        out_shape=jax.ShapeDtypeStruct((t_rows, gw), BF16),
        scratch_shapes=[pltpu.VMEM((gw, tq), F32), pltpu.VMEM((8, tq), F32), pltpu.VMEM((8, tq), F32)],
        compiler_params=_cparams("parallel", "arbitrary"),
        name="fox_mixer",
    )(fqt, fk, fvt, cf_col, cf_row, gain[:, None])


def _head_lane_max(x):
    rows = x.shape[0]
    parts = [jnp.broadcast_to(jnp.max(x[:, h * DH:(h + 1) * DH], axis=-1, keepdims=True), (rows, DH))
             for h in range(HEADS)]
    return jnp.concatenate(parts, axis=-1)


def _mlstm_kernel(x_ref, gt_ref, cw_ref, gain_ref, tri_ref, eb_ref, ei_ref, ones_ref, mh_ref, o_ref,
                  cbuf, qk_s, ct_ref, n_ref, m_ref, *, nchunk, lg):
    L = CHUNK
    gw = GROUP_W

    @pl.when(pl.program_id(1) == 0)
    def _():
        cbuf[0:8, :] = jnp.zeros((8, 2 * gw), F32)
        ct_ref[...] = jnp.zeros_like(ct_ref)
        n_ref[...] = jnp.zeros_like(n_ref)
        m_ref[...] = jnp.zeros_like(m_ref)

    cbuf[8:8 + lg, :] = x_ref[0, :, 0:2 * gw]
    acc = None
    for j in range(ML_CONV):
        term = cbuf[pl.ds(8 - (ML_CONV - 1) + j, lg), :] * cw_ref[j:j + 1, :]
        acc = term if acc is None else acc + term
    cbuf[0:8, :] = cbuf[lg:lg + 8, :]
    qk_s[...] = acc * _sigmoid(acc)

    bd = _block_diag_mask(gw)
    row = lax.broadcasted_iota(jnp.int32, (L, gw), 0)
    col = lax.broadcasted_iota(jnp.int32, (L, gw), 1) & (DH - 1)
    causal = col <= row
    diag = col == row

    def chunk(ci, carry):
        r0 = pl.multiple_of(ci * L, L)
        q = qk_s[pl.ds(r0, L), 0:gw]
        k = qk_s[pl.ds(r0, L), gw:2 * gw] * (DH ** -0.5)
        v = x_ref[0, pl.ds(r0, L), 2 * gw:3 * gw]
        og = x_ref[0, pl.ds(r0, L), 3 * gw:4 * gw]
        g = gt_ref[0, pl.ds(r0, L), :]
        cs = _sel_dot(tri_ref[...], _log_sigmoid(g))
        b_exp = _dot_sel(cs, eb_ref[...])
        imb = _dot_sel(g, ei_ref[...]) - b_exp
        imb_row = jnp.sum(jnp.where(diag, imb, 0.0), axis=0, keepdims=True)
        m_prev = m_ref[...]
        d_log = jnp.where(causal, b_exp + imb_row, NEG_BIG)
        inter = b_exp + m_prev
        m_t = jnp.maximum(inter, _head_lane_max(d_log))
        w_inter = jnp.exp(inter - m_t)
        qk = _dot_nt(q, _tile_heads(k, bd)) * jnp.exp(d_log - m_t)
        ct = ct_ref[...]
        n_row = n_ref[...]
        num = w_inter * _dot(q, ct) + _dot(qk, _tile_heads(v, bd))
        den = w_inter * _dot_sel(q * n_row, ones_ref[...], 2) + _dot_sel(qk, ones_ref[...], 2)
        hh = num / jnp.maximum(jnp.abs(den), jnp.exp(-m_t))
        out = _head_rms(hh, mh_ref[...], gain_ref[...]) * _sigmoid(og)
        o_ref[0, pl.ds(r0, L), :] = out.astype(o_ref.dtype)
        b_last = b_exp[L - 1:L]
        src = b_last + imb
        m_new = jnp.maximum(b_last + m_prev, jnp.max(src, axis=0, keepdims=True))
        decay = jnp.exp(b_last + m_prev - m_new)
        kw = k * jnp.exp(src - m_new)
        ct_ref[...] = ct * decay + jnp.where(bd, _dot_tn(kw, v), 0.0)
        n_ref[...] = n_row * decay + jnp.sum(kw, axis=0, keepdims=True)
        m_ref[...] = m_new
        return carry

    lax.fori_loop(0, nchunk, chunk, 0, unroll=2)


def _mlstm_mixer(ml, gt, conv_w, gain, lg):
    b, s, w = ml.shape
    gw = GROUP_W
    tri = jnp.asarray(np.tril(np.ones((CHUNK, CHUNK), np.float32)), BF16)
    eb = jnp.asarray(_gate_expand(2 * HEADS), BF16)
    ei = jnp.asarray(_gate_expand(HEADS), BF16)
    ones = jnp.asarray(_head_block(1.0), BF16)
    mh = jnp.asarray(_head_block(1.0 / DH), BF16)
    fixed = lambda i, j: (0, 0)
    blk = lambda i, j: (i, j, 0)
    return pl.pallas_call(
        functools.partial(_mlstm_kernel, nchunk=lg // CHUNK, lg=lg),
        grid=(b, s // lg),
        in_specs=[pl.BlockSpec((1, lg, w), blk), pl.BlockSpec((1, lg, GATE_W), blk),
                  pl.BlockSpec((ML_CONV, 2 * gw), fixed), pl.BlockSpec((1, gw), fixed),
                  pl.BlockSpec((CHUNK, CHUNK), fixed), pl.BlockSpec((GATE_W, gw), fixed),
                  pl.BlockSpec((GATE_W, gw), fixed), pl.BlockSpec((gw, gw), fixed),
                  pl.BlockSpec((gw, gw), fixed)],
        out_specs=pl.BlockSpec((1, lg, gw), blk),
        out_shape=jax.ShapeDtypeStruct((b, s, gw), BF16),
        scratch_shapes=[pltpu.VMEM((lg + 8, 2 * gw), F32), pltpu.VMEM((lg, 2 * gw), F32),
                        pltpu.VMEM((gw, gw), F32), pltpu.VMEM((1, gw), F32), pltpu.VMEM((1, gw), F32)],
        compiler_params=_cparams("parallel", "arbitrary"),
        name="mlstm_mixer",
    )(ml, gt, conv_w, gain[None, :], tri, eb, ei, ones, mh)


def _post_kernel(ya_ref, yb_ref, yc_ref, yd_ref, h_ref, wo_ref, gpost_ref, gpre_ref, wg_ref, wu_ref, wd_ref,
                 gffn_ref, o_ref, *, ff_chunk):
    gw = GROUP_W
    mix = None
    for i, y_ref in enumerate((ya_ref, yb_ref, yc_ref, yd_ref)):
        t = jnp.dot(y_ref[...], wo_ref[i * gw:(i + 1) * gw, :], preferred_element_type=F32)
        mix = t if mix is None else mix + t
    h1 = h_ref[...] + _rms(mix, gpost_ref[...])
    a = _rms(h1, gpre_ref[...]).astype(BF16)
    d_ff = wg_ref.shape[1]
    ff = None
    for c0 in range(0, d_ff, ff_chunk):
        g = jnp.dot(a, wg_ref[:, c0:c0 + ff_chunk], preferred_element_type=F32)
        u = jnp.dot(a, wu_ref[:, c0:c0 + ff_chunk], preferred_element_type=F32)
        act = (g * _sigmoid(g) * u).astype(BF16)
        t = jnp.dot(act, wd_ref[c0:c0 + ff_chunk, :], preferred_element_type=F32)
        ff = t if ff is None else ff + t
    o_ref[...] = h1 + _rms(ff, gffn_ref[...])


def _post(ya, yb, yc, yd, h, w_out, g_post, g_pre, w_gate, w_up, w_down, g_ffn, tm):
    t_rows, d = h.shape
    gw = GROUP_W
    d_ff = w_gate.shape[1]
    ff_chunk = d_ff // 2 if (d_ff // 2) % 128 == 0 else d_ff
    row = lambda i: (i, 0)
    fixed = lambda i: (0, 0)
    once = pl.Buffered(1)
    wspec = lambda shape: pl.BlockSpec(shape, fixed, pipeline_mode=once)
    gspec = pl.BlockSpec((1, d), fixed)
    return pl.pallas_call(
        functools.partial(_post_kernel, ff_chunk=ff_chunk),
        grid=(t_rows // tm,),
        in_specs=[pl.BlockSpec((tm, gw), row)] * 4
        + [pl.BlockSpec((tm, d), row), wspec((d, d)), gspec, gspec,
           wspec((d, d_ff)), wspec((d, d_ff)), wspec((d_ff, d)), gspec],
        out_specs=pl.BlockSpec((tm, d), row),
        out_shape=jax.ShapeDtypeStruct((t_rows, d), F32),
        compiler_params=_cparams("parallel"),
        name="out_proj_ffn",
    )(ya, yb, yc, yd, h, w_out.astype(BF16), g_post[None, :], g_pre[None, :],
      w_gate.astype(BF16), w_up.astype(BF16), w_down.astype(BF16), g_ffn[None, :])


def kernel(x, w_in, gate_bias, s5_lambda_re, s5_lambda_im, s5_b_re, s5_b_im, s5_c_re, s5_c_im, s5_d, s5_log_dt,
           s5_w_glu, hgrn_lb_logits, mlstm_conv_w, mix_gain, w_out, ln_mix_pre, ln_mix_post, ln_ffn_pre,
           ln_ffn_post, w_ffn_gate, w_ffn_up, w_ffn_down):
    bsz, seq, d = x.shape
    depth = w_in.shape[0]
    gw = GROUP_W
    tm = min(512, seq)
    lg = min(512, seq)
    s5_lb = min(64, seq)

    lb_all = pl.pallas_call(_lb_kernel, out_shape=jax.ShapeDtypeStruct(hgrn_lb_logits.shape, F32),
                            name="hgrn_lower_bounds")(hgrn_lb_logits)

    h = x.reshape(bsz * seq, d)
    for l in range(depth):
        gain = mix_gain[l]
        u5, hg, fk, ml, gt, fqt, fvt = _in_proj(h, ln_mix_pre[l], w_in[l], gate_bias[l], tm)
        lam, wb, cm = _s5_params(s5_lambda_re[l], s5_lambda_im[l], s5_b_re[l], s5_b_im[l],
                                 s5_c_re[l], s5_c_im[l], s5_log_dt[l])
        u_t = u5.reshape(bsz, seq, gw).transpose(1, 0, 2)
        ya_t = _s5_mixer(u_t, lam, wb, cm, s5_d[l], s5_w_glu[l], gain[0:gw], s5_lb)
        ya = ya_t.transpose(1, 0, 2).reshape(bsz * seq, gw)
        yb = _hgrn_mixer(hg.reshape(bsz, seq, 4 * gw), lb_all[l], gain[gw:2 * gw], lg)
        gt3 = gt.reshape(bsz, seq, GATE_W)
        cf_col, cf_row = _fox_gates(gt3, lg)
        yc = _fox_mixer(fqt, fk, fvt, cf_col, cf_row, gain[2 * gw:3 * gw], bsz, lg)
        yd = _mlstm_mixer(ml.reshape(bsz, seq, 4 * gw), gt3, mlstm_conv_w[l], gain[3 * gw:4 * gw], lg)
        h = _post(ya, yb.reshape(bsz * seq, gw), yc, yd.reshape(bsz * seq, gw), h,
                  w_out[l], ln_mix_post[l], ln_ffn_pre[l], w_ffn_gate[l], w_ffn_up[l], w_ffn_down[l],
                  ln_ffn_post[l], tm)
    return h.reshape(bsz, seq, d)
```

```python
import functools
import math

import numpy as np
import jax
import jax.numpy as jnp
from jax import lax
from jax.experimental import pallas as pl
from jax.experimental.pallas import tpu as pltpu

F32 = jnp.float32
BF16 = jnp.bfloat16

EPS = 1e-6
NEG_BIG = -1e30
EXP_CLIP = 60.0

GROUP_W = 256
HEADS = 4
DH = GROUP_W // HEADS
S5_G, S5_P, S5_N = 16, 16, 64
ML_CONV = 4
CHUNK = 64
HG_LEVELS = (32, 16, 8, 4, 2, 1)
HG_SMALL_LEVELS = (4, 2, 1)
GATE_W = 128

VMEM_LIMIT_BYTES = 56 * 1024 * 1024
MXU_TILE = 256


def _cparams(*sem):
    return pltpu.CompilerParams(dimension_semantics=sem, vmem_limit_bytes=VMEM_LIMIT_BYTES)


def _dot(a, b):
    return jnp.dot(a.astype(BF16), b.astype(BF16), preferred_element_type=F32)


def _dot_nt(a, b):
    return lax.dot_general(a.astype(BF16), b.astype(BF16), (((1,), (1,)), ((), ())),
                           preferred_element_type=F32)


def _dot_tn(a, b):
    return lax.dot_general(a.astype(BF16), b.astype(BF16), (((0,), (0,)), ((), ())),
                           preferred_element_type=F32)


def _split(x, n):
    parts, r = [], x
    for i in range(n):
        p = r.astype(BF16)
        parts.append(p)
        if i + 1 < n:
            r = r - p.astype(F32)
    return parts


def _sel_dot(m01, x, n=3):
    out = None
    for p in _split(x, n):
        t = jnp.dot(m01, p, preferred_element_type=F32)
        out = t if out is None else out + t
    return out


def _dot_sel(x, m01, n=3):
    out = None
    for p in _split(x, n):
        t = jnp.dot(p, m01, preferred_element_type=F32)
        out = t if out is None else out + t
    return out


def _log_sigmoid(z):
    return jnp.minimum(z, 0.0) - jnp.log(1.0 + jnp.exp(-jnp.abs(z)))


def _sigmoid(z):
    return 1.0 / (1.0 + jnp.exp(-z))


def _rms(x, gain):
    ms = jnp.mean(x * x, axis=-1, keepdims=True)
    return x * lax.rsqrt(ms + EPS) * gain


def _head_rms(o, mh, gain):
    ms = _dot_sel(o * o, mh, 2)
    return o * lax.rsqrt(ms + EPS) * gain


def _block_diag_mask(n):
    r = lax.broadcasted_iota(jnp.int32, (n, n), 0)
    c = lax.broadcasted_iota(jnp.int32, (n, n), 1)
    return (r >> 6) == (c >> 6)


def _tile_heads(x, bd):
    return jnp.where(bd, jnp.concatenate([x, x, x, x], axis=0), 0.0)


def _hgrn_level_mats():
    L = CHUNK
    t = np.arange(L)[:, None]
    j = np.arange(L)[None, :]
    blocks = [j <= t]
    for m in HG_SMALL_LEVELS:
        ref = (t // (2 * m)) * 2 * m + m - 1
        blocks.append(j <= ref)
    return np.concatenate(blocks, axis=0).astype(np.float32)


def _head_block(value):
    i = np.arange(GROUP_W)
    return np.where((i[:, None] // DH) == (i[None, :] // DH), value, 0.0).astype(np.float32)


def _gate_expand(col0):
    e = np.zeros((GATE_W, GROUP_W), np.float32)
    for h in range(HEADS):
        e[col0 + h, h * DH:(h + 1) * DH] = 1.0
    return e


def _lb_kernel(logit_ref, o_ref):
    x = logit_ref[...]
    depth = x.shape[0]
    m = x[0:1]
    for l in range(1, depth):
        m = jnp.maximum(m, x[l:l + 1])
    e = [jnp.exp(x[l:l + 1] - m) for l in range(depth)]
    tot = e[0]
    for l in range(1, depth):
        tot = tot + e[l]
    p = [el / tot for el in e]
    c = None
    for l in range(depth):
        c = p[l] if c is None else c + p[l]
        o_ref[l:l + 1, :] = jnp.maximum(c - p[0], 0.0)


def _s5_param_kernel(lr_ref, li_ref, ldt_ref, bre_ref, bim_ref, abr_ref, abi_ref, bbr_ref, bbi_ref):
    lr = jnp.minimum(lr_ref[...], -1e-4)
    li = li_ref[...]
    dt = jnp.exp(ldt_ref[...])
    mag = jnp.exp(lr * dt)
    ab_re = mag * jnp.cos(li * dt)
    ab_im = mag * jnp.sin(li * dt)
    den = lr * lr + li * li
    cf_re = ((ab_re - 1.0) * lr + ab_im * li) / den
    cf_im = (ab_im * lr - (ab_re - 1.0) * li) / den
    bre = bre_ref[...]
    bim = bim_ref[...]
    abr_ref[...] = ab_re
    abi_ref[...] = ab_im
    bbr_ref[...] = cf_re * bre - cf_im * bim
    bbi_ref[...] = cf_re * bim + cf_im * bre


def _s5_params(lam_re, lam_im, b_re, b_im, c_re, c_im, log_dt):
    gp = S5_G * S5_P
    rep = lambda a: jnp.repeat(a, S5_P, axis=0)
    ldt = jnp.broadcast_to(rep(log_dt[:, None]), (gp, S5_N))
    bt = lambda a: a.transpose(0, 2, 1).reshape(gp, S5_N)
    shp = jax.ShapeDtypeStruct((gp, S5_N), F32)
    ab_re, ab_im, bb_re, bb_im = pl.pallas_call(
        _s5_param_kernel, out_shape=(shp, shp, shp, shp), name="s5_params",
    )(rep(lam_re), rep(lam_im), ldt, bt(b_re), bt(b_im))
    lam = jnp.stack([ab_re[::S5_P].reshape(-1), ab_im[::S5_P].reshape(-1)])
    own_gp = jnp.asarray(np.eye(S5_G, dtype=np.float32).repeat(S5_P, axis=0))
    own_gn = jnp.asarray(np.eye(S5_G, dtype=np.float32).repeat(S5_N, axis=0))
    wide = lambda bb: (bb[:, None, :] * own_gp[:, :, None]).reshape(gp, S5_G * S5_N)
    wb = jnp.concatenate([wide(bb_re), wide(bb_im)], axis=1).astype(BF16)
    tall = lambda c: (c.transpose(0, 2, 1).reshape(S5_G * S5_N, S5_P)[:, None, :]
                      * own_gn[:, :, None]).reshape(S5_G * S5_N, gp)
    cm = jnp.concatenate([tall(c_re), -tall(c_im)], axis=0).astype(BF16)
    return lam, wb, cm


IN_COLS = (GROUP_W, 4 * GROUP_W, GROUP_W, 4 * GROUP_W, GATE_W)
IN_DTYPES = (F32, F32, BF16, F32, F32)


def _in_proj_kernel(x_ref, g_ref, w_ref, wt_ref, bias_ref, o_u5, o_hg, o_fk, o_ml, o_gt, o_fqt, o_fvt):
    gw = GROUP_W
    a = _rms(x_ref[...], g_ref[...]).astype(BF16)
    outs = (o_u5, o_hg, o_fk, o_ml, o_gt)
    c0 = 0
    for o_ref, width in zip(outs, IN_COLS):
        r = jnp.dot(a, w_ref[:, c0:c0 + width], preferred_element_type=F32)
        if o_ref is o_gt:
            r = r + bias_ref[...]
        o_ref[...] = r.astype(o_ref.dtype)
        c0 += width
    nt = (((1,), (1,)), ((), ()))
    o_fqt[...] = lax.dot_general(wt_ref[0:gw, :], a, nt, preferred_element_type=F32)
    o_fvt[...] = lax.dot_general(wt_ref[gw:2 * gw, :], a, nt, preferred_element_type=F32).astype(o_fvt.dtype)


def _in_proj(h, gain, w_in_l, gate_bias_l, tm):
    t_rows, d = h.shape
    gw = GROUP_W
    o_fox_f = 8 * gw
    o_ml = o_fox_f + HEADS
    o_ml_i = o_ml + 4 * gw
    o_ml_f = o_ml_i + HEADS
    gates = jnp.concatenate([w_in_l[:, o_fox_f:o_fox_f + HEADS], w_in_l[:, o_ml_i:o_ml_i + HEADS],
                             w_in_l[:, o_ml_f:o_ml_f + HEADS],
                             jnp.zeros((d, GATE_W - 3 * HEADS), w_in_l.dtype)], axis=1)
    w = jnp.concatenate([w_in_l[:, :5 * gw], w_in_l[:, 6 * gw:7 * gw], w_in_l[:, o_ml:o_ml + 4 * gw], gates],
                        axis=1).astype(BF16)
    wt = jnp.concatenate([w_in_l[:, 5 * gw:6 * gw], w_in_l[:, 7 * gw:8 * gw]], axis=1).T.astype(BF16)
    bias = jnp.concatenate([gate_bias_l, jnp.zeros((GATE_W - 3 * HEADS,), F32)])[None, :]
    n_tot = sum(IN_COLS)
    row = lambda i: (i, 0)
    colb = lambda i: (0, i)
    fixed = lambda i: (0, 0)
    return pl.pallas_call(
        _in_proj_kernel,
        grid=(t_rows // tm,),
        in_specs=[pl.BlockSpec((tm, d), row), pl.BlockSpec((1, d), fixed),
                  pl.BlockSpec((d, n_tot), fixed), pl.BlockSpec((2 * gw, d), fixed),
                  pl.BlockSpec((1, GATE_W), fixed)],
        out_specs=[pl.BlockSpec((tm, c), row) for c in IN_COLS]
        + [pl.BlockSpec((gw, tm), colb), pl.BlockSpec((gw, tm), colb)],
        out_shape=[jax.ShapeDtypeStruct((t_rows, c), dt) for c, dt in zip(IN_COLS, IN_DTYPES)]
        + [jax.ShapeDtypeStruct((gw, t_rows), F32), jax.ShapeDtypeStruct((gw, t_rows), BF16)],
        compiler_params=_cparams("parallel"),
        name="in_proj",
    )(h, gain[None, :], w, wt, bias)


def _s5_kernel(u_ref, wb_ref, lam_ref, cm_ref, d_ref, wglu_ref, gain_ref, o_ref, xs_ref, st_ref, *, lb):
    nb = u_ref.shape[1]
    ns = S5_G * S5_N

    @pl.when(pl.program_id(0) == 0)
    def _():
        st_ref[...] = jnp.zeros_like(st_ref)

    u = u_ref[...].reshape(lb * nb, GROUP_W)
    xs_ref[...] = jnp.dot(u.astype(BF16), wb_ref[...], preferred_element_type=F32)
    ar = jnp.broadcast_to(lam_ref[0:1, :], (nb, ns))
    ai = jnp.broadcast_to(lam_ref[1:2, :], (nb, ns))

    def step(t, carry):
        xr, xi = carry
        r0 = pl.multiple_of(t * nb, nb)
        nr = ar * xr - ai * xi + xs_ref[pl.ds(r0, nb), 0:ns]
        ni = ar * xi + ai * xr + xs_ref[pl.ds(r0, nb), ns:2 * ns]
        xs_ref[pl.ds(r0, nb), 0:ns] = nr
        xs_ref[pl.ds(r0, nb), ns:2 * ns] = ni
        return nr, ni

    xr, xi = lax.fori_loop(0, lb, step, (st_ref[:, 0:ns], st_ref[:, ns:2 * ns]), unroll=8)
    st_ref[:, 0:ns] = xr
    st_ref[:, ns:2 * ns] = xi

    half = (lb // 2) * nb
    y = jnp.concatenate([jnp.dot(xs_ref[r:r + half, :].astype(BF16), cm_ref[...], preferred_element_type=F32)
                         for r in (0, half)], axis=0) + d_ref[...] * u
    g = jax.nn.gelu(y)
    y = g * _sigmoid(_dot(g, wglu_ref[...]))
    o_ref[...] = _rms(y, gain_ref[...]).reshape(lb, nb, GROUP_W).astype(o_ref.dtype)


def _s5_mixer(u_t, lam, wb, cm, d_skip, w_glu, gain, lb):
    s, nb, gw = u_t.shape
    ns2 = 2 * S5_G * S5_N
    fixed = lambda i: (0, 0)
    return pl.pallas_call(
        functools.partial(_s5_kernel, lb=lb),
        grid=(s // lb,),
        in_specs=[pl.BlockSpec((lb, nb, gw), lambda i: (i, 0, 0)),
                  pl.BlockSpec((gw, ns2), fixed), pl.BlockSpec((2, ns2 // 2), fixed),
                  pl.BlockSpec((ns2, gw), fixed), pl.BlockSpec((1, gw), fixed),
                  pl.BlockSpec((gw, gw), fixed), pl.BlockSpec((1, gw), fixed)],
        out_specs=pl.BlockSpec((lb, nb, gw), lambda i: (i, 0, 0)),
        out_shape=jax.ShapeDtypeStruct((s, nb, gw), BF16),
        scratch_shapes=[pltpu.VMEM((lb * nb, ns2), F32), pltpu.VMEM((nb, ns2), F32)],
        compiler_params=_cparams("arbitrary"),
        name="s5_mixer",
    )(u_t, wb, lam, cm, d_skip[None, :], w_glu.astype(BF16), gain[None, :])


def _hgrn_kernel(x_ref, lb_ref, gain_ref, mall_ref, mh_ref, o_ref, st_ref, *, ngroup, nsub):
    L = CHUNK
    gw = GROUP_W
    R = nsub * L

    @pl.when(pl.program_id(1) == 0)
    def _():
        st_ref[...] = jnp.zeros_like(st_ref)

    lb = lb_ref[...]
    gain = gain_ref[...]
    bd = _block_diag_mask(gw)
    row = lax.broadcasted_iota(jnp.int32, (R, gw), 0) & (L - 1)
    row_c = lax.broadcasted_iota(jnp.int32, (L, gw), 0)
    col_c = lax.broadcasted_iota(jnp.int32, (L, gw), 1) & (DH - 1)
    chunks = [slice(c * L, (c + 1) * L) for c in range(nsub)]

    def group(gi, carry):
        r0 = pl.multiple_of(gi * R, R)
        q = x_ref[0, pl.ds(r0, R), 0:gw]
        z = x_ref[0, pl.ds(r0, R), gw:2 * gw]
        v = x_ref[0, pl.ds(r0, R), 2 * gw:3 * gw]
        gg = x_ref[0, pl.ds(r0, R), 3 * gw:4 * gw]
        logf = _log_sigmoid(z) + jnp.log(1.0 + lb * jnp.exp(jnp.minimum(-z, EXP_CLIP)))
        kk = (1.0 - lb) * _sigmoid(-z)
        cums = [_sel_dot(mall_ref[...], logf[c]) for c in chunks]
        b = jnp.concatenate([cm[0:L] for cm in cums], axis=0)
        a = [jnp.where(row_c == col_c, _dot_nt(q[c], _tile_heads(kk[c], bd)), 0.0) for c in chunks]
        for m in HG_LEVELS:
            if m in HG_SMALL_LEVELS:
                i = 1 + HG_SMALL_LEVELS.index(m)
                b_ref = jnp.concatenate([cm[i * L:(i + 1) * L] for cm in cums], axis=0)
            else:
                b_ref = jnp.concatenate([jnp.broadcast_to(b[c0 + m - 1:c0 + m], (2 * m, gw))
                                         for c0 in range(0, R, 2 * m)], axis=0)
            upper = (row & m) != 0
            w = jnp.where(upper, q, kk) * jnp.exp(-jnp.abs(b - b_ref))
            ql = jnp.where(upper, w, 0.0)
            kl = jnp.where(upper, 0.0, w)
            sh = int(math.log2(2 * m))
            same = (row_c >> sh) == (col_c >> sh)
            a = [a[n] + jnp.where(same, _dot_nt(ql[c], _tile_heads(kl[c], bd)), 0.0)
                 for n, c in enumerate(chunks)]
        o_intra = [_dot(a[n], _tile_heads(v[c], bd)) for n, c in enumerate(chunks)]
        b_last = [b[(n + 1) * L - 1:(n + 1) * L] for n in range(nsub)]
        kdec = kk * jnp.exp(jnp.concatenate([jnp.broadcast_to(bl, (L, gw)) for bl in b_last], axis=0) - b)
        d_st = [jnp.where(bd, _dot_tn(v[c], kdec[c]), 0.0) for c in chunks]
        qe = q * jnp.exp(b)
        st = st_ref[...]
        outs = []
        for n, c in enumerate(chunks):
            outs.append(o_intra[n] + _dot_nt(qe[c], st))
            st = st * jnp.exp(b_last[n]) + d_st[n]
        st_ref[...] = st
        o = jnp.concatenate(outs, axis=0)
        out = _head_rms(o, mh_ref[...], gain) * (gg * _sigmoid(gg))
        o_ref[0, pl.ds(r0, R), :] = out.astype(o_ref.dtype)
        return carry

    lax.fori_loop(0, ngroup, group, 0)


def _hgrn_mixer(hg, lb, gain, lg, nsub=8):
    b, s, w = hg.shape
    gw = GROUP_W
    assert lg % (CHUNK * nsub) == 0
    mall = jnp.asarray(_hgrn_level_mats(), BF16)
    mh = jnp.asarray(_head_block(1.0 / DH), BF16)
    fixed = lambda i, j: (0, 0)
    return pl.pallas_call(
        functools.partial(_hgrn_kernel, ngroup=lg // (CHUNK * nsub), nsub=nsub),
        grid=(b, s // lg),
        in_specs=[pl.BlockSpec((1, lg, w), lambda i, j: (i, j, 0)),
                  pl.BlockSpec((1, gw), fixed), pl.BlockSpec((1, gw), fixed),
                  pl.BlockSpec(mall.shape, fixed), pl.BlockSpec((gw, gw), fixed)],
        out_specs=pl.BlockSpec((1, lg, gw), lambda i, j: (i, j, 0)),
        out_shape=jax.ShapeDtypeStruct((b, s, gw), BF16),
        scratch_shapes=[pltpu.VMEM((gw, gw), F32)],
        compiler_params=_cparams("parallel", "arbitrary"),
        name="hgrn2_mixer",
    )(hg, lb[None, :], gain[None, :], mall, mh)


def _fox_gate_kernel(g_ref, tri_ref, col_ref, rowo_ref, carry_ref):
    @pl.when(pl.program_id(1) == 0)
    def _():
        carry_ref[...] = jnp.zeros_like(carry_ref)

    lf = _log_sigmoid(g_ref[0])
    cs = _sel_dot(tri_ref[...], lf) + carry_ref[...]
    n = cs.shape[0]
    carry_ref[...] = cs[n - 1:n]
    col_ref[0] = cs
    rowo_ref[0] = cs.T[0:8]


def _fox_gates(gt, lg):
    b, s, w = gt.shape
    tri = jnp.asarray(np.tril(np.ones((lg, lg), np.float32)), BF16)
    return pl.pallas_call(
        _fox_gate_kernel,
        grid=(b, s // lg),
        in_specs=[pl.BlockSpec((1, lg, w), lambda i, j: (i, j, 0)),
                  pl.BlockSpec((lg, lg), lambda i, j: (0, 0))],
        out_specs=[pl.BlockSpec((1, lg, w), lambda i, j: (i, j, 0)),
                   pl.BlockSpec((1, 8, lg), lambda i, j: (i, 0, j))],
        out_shape=[jax.ShapeDtypeStruct((b, s, w), F32), jax.ShapeDtypeStruct((b, 8, s), F32)],
        scratch_shapes=[pltpu.VMEM((1, w), F32)],
        compiler_params=_cparams("parallel", "arbitrary"),
        name="fox_gates",
    )(gt, tri)


def _fox_kernel(qt_ref, k_ref, vt_ref, ccol_ref, crow_ref, gain_ref, o_ref, *, tq, tk):
    qi = pl.program_id(1)
    gw = GROUP_W
    log2e = math.log2(math.e)
    qt = (qt_ref[...] * (DH ** -0.5 * log2e)).astype(BF16)
    head_row = lax.broadcasted_iota(jnp.int32, (gw, tq), 0) >> 6
    q_heads = [jnp.where(head_row == h, qt, jnp.zeros_like(qt)) for h in range(HEADS)]
    key_i = lax.broadcasted_iota(jnp.int32, (tk, tq), 0)
    qry_i = lax.broadcasted_iota(jnp.int32, (tk, tq), 1)
    cqs = [crow_ref[0, h:h + 1, :] * log2e for h in range(HEADS)]
    kv_per_q = tq // tk

    def run_tiles(tiles, state):
        state = list(state)

        def scores(j, diag_offset, h):
            r0 = pl.multiple_of(j * tk, tk)
            st = jnp.dot(k_ref[pl.ds(r0, tk), :], q_heads[h], preferred_element_type=F32)
            u = st - ccol_ref[0, pl.ds(r0, tk), h:h + 1] * log2e
            if diag_offset is not None:
                u = jnp.where(key_i + diag_offset <= qry_i, u, NEG_BIG)
            m_new = jnp.maximum(state[h][0], jnp.max(u, axis=0, keepdims=True) + cqs[h])
            return u, m_new

        def accumulate(j, h, u, m_new):
            r0 = pl.multiple_of(j * tk, tk)
            m_old, l_old, acc_old = state[h]
            alpha = jnp.exp2(m_old - m_new)
            p = jnp.exp2(u + (cqs[h] - m_new))
            l_new = alpha * l_old + jnp.sum(p, axis=0, keepdims=True)
            pv = jnp.dot(vt_ref[h * DH:(h + 1) * DH, pl.ds(r0, tk)], p.astype(BF16),
                         preferred_element_type=F32)
            state[h] = (m_new, l_new, alpha * acc_old + pv)

        items = [(j, off, h) for j, off in tiles for h in range(HEADS)]
        pending = scores(*items[0])
        for n, (j, off, h) in enumerate(items):
            if n + 1 < len(items) and items[n + 1][2] != h:
                nxt = scores(*items[n + 1])
                accumulate(j, h, *pending)
            else:
                accumulate(j, h, *pending)
                nxt = scores(*items[n + 1]) if n + 1 < len(items) else None
            pending = nxt
        return tuple(state)

    init = tuple((jnp.full((1, tq), NEG_BIG, F32), jnp.zeros((1, tq), F32), jnp.zeros((DH, tq), F32))
                 for _ in range(HEADS))
    n_full = qi * kv_per_q
    state = lax.fori_loop(0, n_full // 2, lambda i, s: run_tiles([(2 * i, None), (2 * i + 1, None)], s), init)
    diag = [(n_full + d, d * tk) for d in range(kv_per_q)]
    state = lax.cond(n_full % 2 == 1,
                     lambda s: run_tiles([(n_full - 1, None)] + diag, s),
                     lambda s: run_tiles(diag, s), state)

    outs = []
    for h in range(HEADS):
        _, l_fin, acc_fin = state[h]
        o = acc_fin / l_fin
        ms = jnp.mean(o * o, axis=0, keepdims=True)
        outs.append(o * lax.rsqrt(ms + EPS))
    out = jnp.concatenate(outs, axis=0) * gain_ref[...]
    o_ref[...] = out.T.astype(o_ref.dtype)


def _fox_mixer(fqt, fk, fvt, cf_col, cf_row, gain, bsz, tq, tk):
    gw, t_rows = fqt.shape
    s = t_rows // bsz
    nq = s // tq
    return pl.pallas_call(
        functools.partial(_fox_kernel, tq=tq, tk=tk),
        grid=(bsz, nq),
        in_specs=[pl.BlockSpec((gw, tq), lambda i, qi: (0, i * nq + qi)),
                  pl.BlockSpec((s, gw), lambda i, qi: (i, 0)),
                  pl.BlockSpec((gw, s), lambda i, qi: (0, i)),
                  pl.BlockSpec((1, s, GATE_W), lambda i, qi: (i, 0, 0)),
                  pl.BlockSpec((1, 8, tq), lambda i, qi: (i, 0, qi)),
                  pl.BlockSpec((gw, 1), lambda i, qi: (0, 0))],
        out_specs=pl.BlockSpec((tq, gw), lambda i, qi: (i * nq + qi, 0)),
        out_shape=jax.ShapeDtypeStruct((t_rows, gw), BF16),
        compiler_params=_cparams("parallel", "arbitrary"),
        name="fox_mixer",
    )(fqt, fk, fvt, cf_col, cf_row, gain[:, None])


def _head_lane_max(x):
    rows = x.shape[0]
    parts = [jnp.broadcast_to(jnp.max(x[:, h * DH:(h + 1) * DH], axis=-1, keepdims=True), (rows, DH))
             for h in range(HEADS)]
    return jnp.concatenate(parts, axis=-1)


def _mlstm_kernel(x_ref, gt_ref, cw_ref, gain_ref, tri_ref, eb_ref, ei_ref, mh_ref, o_ref,
                  cbuf, ct_ref, m_ref, *, nsub, lg):
    L = CHUNK
    gw = GROUP_W

    @pl.when(pl.program_id(1) == 0)
    def _():
        cbuf[0:8, :] = jnp.zeros((8, 2 * gw), F32)
        ct_ref[...] = jnp.zeros_like(ct_ref)
        m_ref[...] = jnp.zeros_like(m_ref)

    cbuf[8:8 + lg, :] = x_ref[0, :, 0:2 * gw]
    acc = None
    for j in range(ML_CONV):
        term = cbuf[pl.ds(8 - (ML_CONV - 1) + j, lg), :] * cw_ref[j:j + 1, :]
        acc = term if acc is None else acc + term
    cbuf[0:8, :] = cbuf[lg:lg + 8, :]
    qk = acc * _sigmoid(acc)
    q = qk[:, 0:gw]
    k = qk[:, gw:2 * gw] * (DH ** -0.5)
    v = x_ref[0, :, 2 * gw:3 * gw]
    og = x_ref[0, :, 3 * gw:4 * gw]
    g = gt_ref[0]

    bd = _block_diag_mask(gw)
    bd2 = jnp.concatenate([bd, bd], axis=1)
    row = lax.broadcasted_iota(jnp.int32, (L, gw), 0)
    col = lax.broadcasted_iota(jnp.int32, (L, gw), 1) & (DH - 1)
    causal = col <= row
    diag = col == row
    ones = jnp.ones((L, gw), F32)
    chunks = [slice(c * L, (c + 1) * L) for c in range(nsub)]

    def tile2(x):
        return jnp.where(bd2, jnp.concatenate([x, x, x, x], axis=0), 0.0)

    lsg = _log_sigmoid(g)
    cs = jnp.concatenate([_sel_dot(tri_ref[...], lsg[c]) for c in chunks], axis=0)
    b_exp = _dot_sel(cs, eb_ref[...])
    imb = _dot_sel(g, ei_ref[...]) - b_exp

    m_loc, nd_loc, b_last, m_src, d_ct = [], [], [], [], []
    for c in chunks:
        imb_row = jnp.sum(jnp.where(diag, imb[c], 0.0), axis=0, keepdims=True)
        d_log = jnp.where(causal, b_exp[c] + imb_row, NEG_BIG)
        ml = _head_lane_max(d_log)
        qk_loc = _dot_nt(q[c], _tile_heads(k[c], bd)) * jnp.exp(d_log - ml)
        v2 = jnp.concatenate([v[c], ones], axis=1)
        nd_loc.append(_dot(qk_loc, tile2(v2)))
        m_loc.append(ml)
        bl = b_exp[c][L - 1:L]
        src = bl + imb[c]
        ms = jnp.max(src, axis=0, keepdims=True)
        kw = k[c] * jnp.exp(src - ms)
        d_ct.append(jnp.where(bd2, _dot_tn(kw, v2), 0.0))
        b_last.append(bl)
        m_src.append(ms)

    ct = ct_ref[...]
    m_prev = m_ref[...]
    hs = []
    for n, c in enumerate(chunks):
        inter = b_exp[c] + m_prev
        m_t = jnp.maximum(inter, m_loc[n])
        w_inter = jnp.exp(inter - m_t)
        w_loc = jnp.exp(m_loc[n] - m_t)
        nd = (jnp.concatenate([w_inter, w_inter], axis=1) * _dot(q[c], ct)
              + jnp.concatenate([w_loc, w_loc], axis=1) * nd_loc[n])
        hs.append(nd[:, 0:gw] / jnp.maximum(jnp.abs(nd[:, gw:2 * gw]), jnp.exp(-m_t)))
        m_new = jnp.maximum(b_last[n] + m_prev, m_src[n])
        decay = jnp.exp(b_last[n] + m_prev - m_new)
        w_src = jnp.exp(m_src[n] - m_new)
        ct = (ct * jnp.concatenate([decay, decay], axis=1)
              + d_ct[n] * jnp.concatenate([w_src, w_src], axis=1))
        m_prev = m_new
    ct_ref[...] = ct
    m_ref[...] = m_prev
    hh = jnp.concatenate(hs, axis=0)
    o_ref[0] = (_head_rms(hh, mh_ref[...], gain_ref[...]) * _sigmoid(og)).astype(o_ref.dtype)


def _mlstm_mixer(ml, gt, conv_w, gain, lg):
    b, s, w = ml.shape
    gw = GROUP_W
    assert lg % CHUNK == 0
    tri = jnp.asarray(np.tril(np.ones((CHUNK, CHUNK), np.float32)), BF16)
    eb = jnp.asarray(_gate_expand(2 * HEADS), BF16)
    ei = jnp.asarray(_gate_expand(HEADS), BF16)
    mh = jnp.asarray(_head_block(1.0 / DH), BF16)
    fixed = lambda i, j: (0, 0)
    blk = lambda i, j: (i, j, 0)
    return pl.pallas_call(
        functools.partial(_mlstm_kernel, nsub=lg // CHUNK, lg=lg),
        grid=(b, s // lg),
        in_specs=[pl.BlockSpec((1, lg, w), blk), pl.BlockSpec((1, lg, GATE_W), blk),
                  pl.BlockSpec((ML_CONV, 2 * gw), fixed), pl.BlockSpec((1, gw), fixed),
                  pl.BlockSpec((CHUNK, CHUNK), fixed), pl.BlockSpec((GATE_W, gw), fixed),
                  pl.BlockSpec((GATE_W, gw), fixed), pl.BlockSpec((gw, gw), fixed)],
        out_specs=pl.BlockSpec((1, lg, gw), blk),
        out_shape=jax.ShapeDtypeStruct((b, s, gw), BF16),
        scratch_shapes=[pltpu.VMEM((lg + 8, 2 * gw), F32), pltpu.VMEM((gw, 2 * gw), F32),
                        pltpu.VMEM((1, gw), F32)],
        compiler_params=_cparams("parallel", "arbitrary"),
        name="mlstm_mixer",
    )(ml, gt, conv_w, gain[None, :], tri, eb, ei, mh)


def _post_kernel(ya_ref, yb_ref, yc_ref, yd_ref, h_ref, wo_ref, gpost_ref, gpre_ref, wg_ref, wu_ref, wd_ref,
                 gffn_ref, o_ref, *, ff_chunk):
    gw = GROUP_W
    mix = None
    for i, y_ref in enumerate((ya_ref, yb_ref, yc_ref, yd_ref)):
        t = jnp.dot(y_ref[...], wo_ref[i * gw:(i + 1) * gw, :], preferred_element_type=F32)
        mix = t if mix is None else mix + t
    h1 = h_ref[...] + _rms(mix, gpost_ref[...])
    a = _rms(h1, gpre_ref[...]).astype(BF16)
    d_ff = wg_ref.shape[1]
    ff = None
    for c0 in range(0, d_ff, ff_chunk):
        c1 = min(c0 + ff_chunk, d_ff)
        g = jnp.dot(a, wg_ref[:, c0:c1], preferred_element_type=F32)
        u = jnp.dot(a, wu_ref[:, c0:c1], preferred_element_type=F32)
        act = (g * _sigmoid(g) * u).astype(BF16)
        t = jnp.dot(act, wd_ref[c0:c1, :], preferred_element_type=F32)
        ff = t if ff is None else ff + t
    o_ref[...] = h1 + _rms(ff, gffn_ref[...])


def _post(ya, yb, yc, yd, h, w_out, g_post, g_pre, w_gate, w_up, w_down, g_ffn, tm):
    t_rows, d = h.shape
    gw = GROUP_W
    d_ff = w_gate.shape[1]
    ff_chunk = min(d_ff, -(-d_ff // (2 * MXU_TILE)) * MXU_TILE)
    row = lambda i: (i, 0)
    fixed = lambda i: (0, 0)
    once = pl.Buffered(1)
    wspec = lambda shape: pl.BlockSpec(shape, fixed, pipeline_mode=once)
    gspec = pl.BlockSpec((1, d), fixed)
    return pl.pallas_call(
        functools.partial(_post_kernel, ff_chunk=ff_chunk),
        grid=(t_rows // tm,),
        in_specs=[pl.BlockSpec((tm, gw), row)] * 4
        + [pl.BlockSpec((tm, d), row), wspec((d, d)), gspec, gspec,
           wspec((d, d_ff)), wspec((d, d_ff)), wspec((d_ff, d)), gspec],
        out_specs=pl.BlockSpec((tm, d), row),
        out_shape=jax.ShapeDtypeStruct((t_rows, d), F32),
        compiler_params=_cparams("parallel"),
        name="out_proj_ffn",
    )(ya, yb, yc, yd, h, w_out.astype(BF16), g_post[None, :], g_pre[None, :],
      w_gate.astype(BF16), w_up.astype(BF16), w_down.astype(BF16), g_ffn[None, :])


def kernel(x, w_in, gate_bias, s5_lambda_re, s5_lambda_im, s5_b_re, s5_b_im, s5_c_re, s5_c_im, s5_d, s5_log_dt,
           s5_w_glu, hgrn_lb_logits, mlstm_conv_w, mix_gain, w_out, ln_mix_pre, ln_mix_post, ln_ffn_pre,
           ln_ffn_post, w_ffn_gate, w_ffn_up, w_ffn_down):
    bsz, seq, d = x.shape
    depth = w_in.shape[0]
    gw = GROUP_W
    tm = min(512, seq)
    lg = min(512, seq)
    s5_lb = min(64, seq)
    fox_tq = min(512, seq)
    fox_tk = min(512, seq)

    lb_all = pl.pallas_call(_lb_kernel, out_shape=jax.ShapeDtypeStruct(hgrn_lb_logits.shape, F32),
                            name="hgrn_lower_bounds")(hgrn_lb_logits)

    h = x.reshape(bsz * seq, d)
    for l in range(depth):
        gain = mix_gain[l]
        u5, hg, fk, ml, gt, fqt, fvt = _in_proj(h, ln_mix_pre[l], w_in[l], gate_bias[l], tm)
        lam, wb, cm = _s5_params(s5_lambda_re[l], s5_lambda_im[l], s5_b_re[l], s5_b_im[l],
                                 s5_c_re[l], s5_c_im[l], s5_log_dt[l])
        u_t = u5.reshape(bsz, seq, gw).transpose(1, 0, 2)
        ya_t = _s5_mixer(u_t, lam, wb, cm, s5_d[l], s5_w_glu[l], gain[0:gw], s5_lb)
        ya = ya_t.transpose(1, 0, 2).reshape(bsz * seq, gw)
        yb = _hgrn_mixer(hg.reshape(bsz, seq, 4 * gw), lb_all[l], gain[gw:2 * gw], lg)
        gt3 = gt.reshape(bsz, seq, GATE_W)
        cf_col, cf_row = _fox_gates(gt3, lg)
        yc = _fox_mixer(fqt, fk, fvt, cf_col, cf_row, gain[2 * gw:3 * gw], bsz, fox_tq, fox_tk)
        yd = _mlstm_mixer(ml.reshape(bsz, seq, 4 * gw), gt3, mlstm_conv_w[l], gain[3 * gw:4 * gw], lg)
        h = _post(ya, yb.reshape(bsz * seq, gw), yc, yd.reshape(bsz * seq, gw), h,
                  w_out[l], ln_mix_post[l], ln_ffn_pre[l], w_ffn_gate[l], w_ffn_up[l], w_ffn_down[l],
                  ln_ffn_post[l], tm)
    return h.reshape(bsz, seq, d)
```

```python
import functools
import math

import numpy as np
import jax
import jax.numpy as jnp
from jax import lax
from jax.experimental import pallas as pl
from jax.experimental.pallas import tpu as pltpu

F32 = jnp.float32
BF16 = jnp.bfloat16

EPS = 1e-6
NEG_BIG = -1e30
EXP_CLIP = 60.0

GROUP_W = 256
HEADS = 4
DH = GROUP_W // HEADS
S5_G, S5_P, S5_N = 16, 16, 64
ML_CONV = 4
CHUNK = 64
HG_LEVELS = (32, 16, 8, 4, 2, 1)
HG_SMALL_LEVELS = (4, 2, 1)
GATE_W = 128
FOX_LOOKAHEAD = 1

VMEM_LIMIT_BYTES = 56 * 1024 * 1024
MXU_TILE = 256


def _cparams(*sem):
    return pltpu.CompilerParams(dimension_semantics=sem, vmem_limit_bytes=VMEM_LIMIT_BYTES)


def _dot(a, b):
    return jnp.dot(a.astype(BF16), b.astype(BF16), preferred_element_type=F32)


def _dot_nt(a, b):
    return lax.dot_general(a.astype(BF16), b.astype(BF16), (((1,), (1,)), ((), ())),
                           preferred_element_type=F32)


def _dot_tn(a, b):
    return lax.dot_general(a.astype(BF16), b.astype(BF16), (((0,), (0,)), ((), ())),
                           preferred_element_type=F32)


def _split(x, n):
    parts, r = [], x
    for i in range(n):
        p = r.astype(BF16)
        parts.append(p)
        if i + 1 < n:
            r = r - p.astype(F32)
    return parts


def _sel_dot(m01, x, n=3):
    out = None
    for p in _split(x, n):
        t = jnp.dot(m01, p, preferred_element_type=F32)
        out = t if out is None else out + t
    return out


def _dot_sel(x, m01, n=3):
    out = None
    for p in _split(x, n):
        t = jnp.dot(p, m01, preferred_element_type=F32)
        out = t if out is None else out + t
    return out


def _log_sigmoid(z):
    return jnp.minimum(z, 0.0) - jnp.log(1.0 + jnp.exp(-jnp.abs(z)))


def _sigmoid(z):
    return 1.0 / (1.0 + jnp.exp(-z))


def _rms(x, gain):
    ms = jnp.mean(x * x, axis=-1, keepdims=True)
    return x * lax.rsqrt(ms + EPS) * gain


def _head_rms(o, mh, gain):
    ms = _dot_sel(o * o, mh, 2)
    return o * lax.rsqrt(ms + EPS) * gain


def _block_diag_mask(n):
    r = lax.broadcasted_iota(jnp.int32, (n, n), 0)
    c = lax.broadcasted_iota(jnp.int32, (n, n), 1)
    return (r >> 6) == (c >> 6)


def _tile_heads(x, bd):
    return jnp.where(bd, jnp.concatenate([x, x, x, x], axis=0), 0.0).astype(BF16)


def _hgrn_level_mats():
    L = CHUNK
    t = np.arange(L)[:, None]
    j = np.arange(L)[None, :]
    blocks = [j <= t]
    for m in HG_SMALL_LEVELS:
        ref = (t // (2 * m)) * 2 * m + m - 1
        blocks.append(j <= ref)
    return np.concatenate(blocks, axis=0).astype(np.float32)


def _head_block(value):
    i = np.arange(GROUP_W)
    return np.where((i[:, None] // DH) == (i[None, :] // DH), value, 0.0).astype(np.float32)


def _gate_expand(col0):
    e = np.zeros((GATE_W, GROUP_W), np.float32)
    for h in range(HEADS):
        e[col0 + h, h * DH:(h + 1) * DH] = 1.0
    return e


def _lb_kernel(logit_ref, o_ref):
    x = logit_ref[...]
    depth = x.shape[0]
    m = x[0:1]
    for l in range(1, depth):
        m = jnp.maximum(m, x[l:l + 1])
    e = [jnp.exp(x[l:l + 1] - m) for l in range(depth)]
    tot = e[0]
    for l in range(1, depth):
        tot = tot + e[l]
    p = [el / tot for el in e]
    c = None
    for l in range(depth):
        c = p[l] if c is None else c + p[l]
        o_ref[l:l + 1, :] = jnp.maximum(c - p[0], 0.0)


def _s5_param_kernel(lr_ref, li_ref, ldt_ref, bre_ref, bim_ref, abr_ref, abi_ref, bbr_ref, bbi_ref):
    lr = jnp.minimum(lr_ref[...], -1e-4)
    li = li_ref[...]
    dt = jnp.exp(ldt_ref[...])
    mag = jnp.exp(lr * dt)
    ab_re = mag * jnp.cos(li * dt)
    ab_im = mag * jnp.sin(li * dt)
    den = lr * lr + li * li
    cf_re = ((ab_re - 1.0) * lr + ab_im * li) / den
    cf_im = (ab_im * lr - (ab_re - 1.0) * li) / den
    bre = bre_ref[...]
    bim = bim_ref[...]
    abr_ref[...] = ab_re
    abi_ref[...] = ab_im
    bbr_ref[...] = cf_re * bre - cf_im * bim
    bbi_ref[...] = cf_re * bim + cf_im * bre


def _s5_params(lam_re, lam_im, b_re, b_im, c_re, c_im, log_dt):
    gp = S5_G * S5_P
    rep = lambda a: jnp.repeat(a, S5_P, axis=0)
    ldt = jnp.broadcast_to(rep(log_dt[:, None]), (gp, S5_N))
    bt = lambda a: a.transpose(0, 2, 1).reshape(gp, S5_N)
    shp = jax.ShapeDtypeStruct((gp, S5_N), F32)
    ab_re, ab_im, bb_re, bb_im = pl.pallas_call(
        _s5_param_kernel, out_shape=(shp, shp, shp, shp), name="s5_params",
    )(rep(lam_re), rep(lam_im), ldt, bt(b_re), bt(b_im))
    lam = jnp.stack([ab_re[::S5_P].reshape(-1), ab_im[::S5_P].reshape(-1)])
    own_gp = jnp.asarray(np.eye(S5_G, dtype=np.float32).repeat(S5_P, axis=0))
    own_gn = jnp.asarray(np.eye(S5_G, dtype=np.float32).repeat(S5_N, axis=0))
    wide = lambda bb: (bb[:, None, :] * own_gp[:, :, None]).reshape(gp, S5_G * S5_N)
    wb = jnp.concatenate([wide(bb_re), wide(bb_im)], axis=1).astype(BF16)
    tall = lambda c: (c.transpose(0, 2, 1).reshape(S5_G * S5_N, S5_P)[:, None, :]
                      * own_gn[:, :, None]).reshape(S5_G * S5_N, gp)
    cm = jnp.concatenate([tall(c_re), -tall(c_im)], axis=0).astype(BF16)
    return lam, wb, cm


IN_COLS = (GROUP_W, 4 * GROUP_W, GROUP_W, 4 * GROUP_W, GATE_W)
IN_DTYPES = (F32, F32, BF16, F32, F32)


def _in_proj_kernel(x_ref, g_ref, w_ref, wt_ref, bias_ref, o_u5, o_hg, o_fk, o_ml, o_gt, o_fqt, o_fvt):
    gw = GROUP_W
    a = _rms(x_ref[...], g_ref[...]).astype(BF16)
    outs = (o_u5, o_hg, o_fk, o_ml, o_gt)
    c0 = 0
    for o_ref, width in zip(outs, IN_COLS):
        r = jnp.dot(a, w_ref[:, c0:c0 + width], preferred_element_type=F32)
        if o_ref is o_gt:
            r = r + bias_ref[...]
        o_ref[...] = r.astype(o_ref.dtype)
        c0 += width
    nt = (((1,), (1,)), ((), ()))
    o_fqt[...] = lax.dot_general(wt_ref[0:gw, :], a, nt, preferred_element_type=F32)
    o_fvt[...] = lax.dot_general(wt_ref[gw:2 * gw, :], a, nt, preferred_element_type=F32).astype(o_fvt.dtype)


def _time_major_map(tiles_per_seq):
    return lambda i: (i % tiles_per_seq, i // tiles_per_seq)


def _in_proj(h, gain, w_in_l, gate_bias_l, tm, seq):
    t_rows, d = h.shape
    assert seq % tm == 0 and t_rows % seq == 0
    gw = GROUP_W
    o_fox_f = 8 * gw
    o_ml = o_fox_f + HEADS
    o_ml_i = o_ml + 4 * gw
    o_ml_f = o_ml_i + HEADS
    gates = jnp.concatenate([w_in_l[:, o_fox_f:o_fox_f + HEADS], w_in_l[:, o_ml_i:o_ml_i + HEADS],
                             w_in_l[:, o_ml_f:o_ml_f + HEADS],
                             jnp.zeros((d, GATE_W - 3 * HEADS), w_in_l.dtype)], axis=1)
    w = jnp.concatenate([w_in_l[:, :5 * gw], w_in_l[:, 6 * gw:7 * gw], w_in_l[:, o_ml:o_ml + 4 * gw], gates],
                        axis=1).astype(BF16)
    wt = jnp.concatenate([w_in_l[:, 5 * gw:6 * gw], w_in_l[:, 7 * gw:8 * gw]], axis=1).T.astype(BF16)
    bias = jnp.concatenate([gate_bias_l, jnp.zeros((GATE_W - 3 * HEADS,), F32)])[None, :]
    n_tot = sum(IN_COLS)
    row = lambda i: (i, 0)
    colb = lambda i: (0, i)
    fixed = lambda i: (0, 0)
    return pl.pallas_call(
        _in_proj_kernel,
        grid=(t_rows // tm,),
        in_specs=[pl.BlockSpec((tm, d), row), pl.BlockSpec((1, d), fixed),
                  pl.BlockSpec((d, n_tot), fixed), pl.BlockSpec((2 * gw, d), fixed),
                  pl.BlockSpec((1, GATE_W), fixed)],
        out_specs=[pl.BlockSpec((tm, gw), _time_major_map(seq // tm))]
        + [pl.BlockSpec((tm, c), row) for c in IN_COLS[1:]]
        + [pl.BlockSpec((gw, tm), colb), pl.BlockSpec((gw, tm), colb)],
        out_shape=[jax.ShapeDtypeStruct((seq, (t_rows // seq) * gw), F32)]
        + [jax.ShapeDtypeStruct((t_rows, c), dt) for c, dt in zip(IN_COLS[1:], IN_DTYPES[1:])]
        + [jax.ShapeDtypeStruct((gw, t_rows), F32), jax.ShapeDtypeStruct((gw, t_rows), BF16)],
        compiler_params=_cparams("parallel"),
        name="in_proj",
    )(h, gain[None, :], w, wt, bias)


def _halves_dot(a, b):
    half = a.shape[0] // 2
    return jnp.concatenate([jnp.dot(a[0:half], b, preferred_element_type=F32),
                            jnp.dot(a[half:], b, preferred_element_type=F32)], axis=0)


def _s5_kernel(u_ref, perm_ref, permt_ref, wb_ref, lam_ref, cm_ref, d_ref, wglu_ref, gain_ref, o_ref,
               xs_ref, st_ref, *, lb, nb):
    gw = GROUP_W
    ns = S5_G * S5_N

    @pl.when(pl.program_id(0) == 0)
    def _():
        st_ref[...] = jnp.zeros_like(st_ref)

    u = jnp.concatenate([u_ref[:, b * gw:(b + 1) * gw] for b in range(nb)], axis=0)
    u_tb = _halves_dot(perm_ref[...], u.astype(BF16)).astype(BF16)
    xs_ref[...] = jnp.dot(u_tb, wb_ref[...], preferred_element_type=F32)
    ar = jnp.broadcast_to(lam_ref[0:1, :], (nb, ns))
    ai = jnp.broadcast_to(lam_ref[1:2, :], (nb, ns))

    def step(t, carry):
        xr, xi = carry
        r0 = pl.multiple_of(t * nb, nb)
        nr = ar * xr - ai * xi + xs_ref[pl.ds(r0, nb), 0:ns]
        ni = ar * xi + ai * xr + xs_ref[pl.ds(r0, nb), ns:2 * ns]
        xs_ref[pl.ds(r0, nb), 0:ns] = nr
        xs_ref[pl.ds(r0, nb), ns:2 * ns] = ni
        return nr, ni

    xr, xi = lax.fori_loop(0, lb, step, (st_ref[:, 0:ns], st_ref[:, ns:2 * ns]), unroll=8)
    st_ref[:, 0:ns] = xr
    st_ref[:, ns:2 * ns] = xi

    cx_tb = _halves_dot(xs_ref[...].astype(BF16), cm_ref[...])
    cx = None
    for part in _split(cx_tb, 2):
        t = _halves_dot(permt_ref[...], part)
        cx = t if cx is None else cx + t
    y = cx + d_ref[...] * u
    g = jax.nn.gelu(y)
    y = g * _sigmoid(_halves_dot(g.astype(BF16), wglu_ref[...]))
    out = _rms(y, gain_ref[...]).astype(o_ref.dtype)
    for b in range(nb):
        o_ref[:, b * gw:(b + 1) * gw] = out[b * lb:(b + 1) * lb]


def _s5_mixer(u2d, lam, wb, cm, d_skip, w_glu, gain, lb, nb):
    s, w = u2d.shape
    gw = GROUP_W
    assert w == nb * gw
    ns2 = 2 * S5_G * S5_N
    idx = np.arange(lb * nb)
    perm = np.zeros((lb * nb, lb * nb), np.float32)
    perm[idx, (idx % nb) * lb + idx // nb] = 1.0
    fixed = lambda i: (0, 0)
    return pl.pallas_call(
        functools.partial(_s5_kernel, lb=lb, nb=nb),
        grid=(s // lb,),
        in_specs=[pl.BlockSpec((lb, w), lambda i: (i, 0)),
                  pl.BlockSpec(perm.shape, fixed), pl.BlockSpec(perm.shape, fixed),
                  pl.BlockSpec((gw, ns2), fixed), pl.BlockSpec((2, ns2 // 2), fixed),
                  pl.BlockSpec((ns2, gw), fixed), pl.BlockSpec((1, gw), fixed),
                  pl.BlockSpec((gw, gw), fixed), pl.BlockSpec((1, gw), fixed)],
        out_specs=pl.BlockSpec((lb, w), lambda i: (i, 0)),
        out_shape=jax.ShapeDtypeStruct((s, w), BF16),
        scratch_shapes=[pltpu.VMEM((lb * nb, ns2), F32), pltpu.VMEM((nb, ns2), F32)],
        compiler_params=_cparams("arbitrary"),
        name="s5_mixer",
    )(u2d, jnp.asarray(perm, BF16), jnp.asarray(perm.T, BF16), wb, lam, cm, d_skip[None, :],
      w_glu.astype(BF16), gain[None, :])


def _hgrn_kernel(x_ref, lb_ref, gain_ref, mall_ref, mh_ref, o_ref, st_ref, *, ngroup, nsub):
    L = CHUNK
    gw = GROUP_W
    R = nsub * L

    @pl.when(pl.program_id(1) == 0)
    def _():
        st_ref[...] = jnp.zeros_like(st_ref)

    lb = lb_ref[...]
    gain = gain_ref[...]
    bd = _block_diag_mask(gw)
    row = lax.broadcasted_iota(jnp.int32, (R, gw), 0) & (L - 1)
    row_c = lax.broadcasted_iota(jnp.int32, (L, gw), 0)
    col_c = lax.broadcasted_iota(jnp.int32, (L, gw), 1) & (DH - 1)
    chunks = [slice(c * L, (c + 1) * L) for c in range(nsub)]

    def group(gi, carry):
        r0 = pl.multiple_of(gi * R, R)
        q = x_ref[0, pl.ds(r0, R), 0:gw]
        z = x_ref[0, pl.ds(r0, R), gw:2 * gw]
        v = x_ref[0, pl.ds(r0, R), 2 * gw:3 * gw]
        gg = x_ref[0, pl.ds(r0, R), 3 * gw:4 * gw]
        logf = _log_sigmoid(z) + jnp.log(1.0 + lb * jnp.exp(jnp.minimum(-z, EXP_CLIP)))
        kk = (1.0 - lb) * _sigmoid(-z)
        cums = [_sel_dot(mall_ref[...], logf[c]) for c in chunks]
        b = jnp.concatenate([cm[0:L] for cm in cums], axis=0)
        a = [jnp.where(row_c == col_c, _dot_nt(q[c], _tile_heads(kk[c], bd)), 0.0) for c in chunks]
        for m in HG_LEVELS:
            if m in HG_SMALL_LEVELS:
                i = 1 + HG_SMALL_LEVELS.index(m)
                b_ref = jnp.concatenate([cm[i * L:(i + 1) * L] for cm in cums], axis=0)
            else:
                b_ref = jnp.concatenate([jnp.broadcast_to(b[c0 + m - 1:c0 + m], (2 * m, gw))
                                         for c0 in range(0, R, 2 * m)], axis=0)
            upper = (row & m) != 0
            w = jnp.where(upper, q, kk) * jnp.exp(-jnp.abs(b - b_ref))
            ql = jnp.where(upper, w, 0.0)
            kl = jnp.where(upper, 0.0, w)
            sh = int(math.log2(2 * m))
            same = (row_c >> sh) == (col_c >> sh)
            a = [a[n] + jnp.where(same, _dot_nt(ql[c], _tile_heads(kl[c], bd)), 0.0)
                 for n, c in enumerate(chunks)]
        o_intra = [_dot(a[n], _tile_heads(v[c], bd)) for n, c in enumerate(chunks)]
        b_last = [b[(n + 1) * L - 1:(n + 1) * L] for n in range(nsub)]
        kdec = kk * jnp.exp(jnp.concatenate([jnp.broadcast_to(bl, (L, gw)) for bl in b_last], axis=0) - b)
        d_st = [jnp.where(bd, _dot_tn(v[c], kdec[c]), 0.0) for c in chunks]
        qe = q * jnp.exp(b)
        st = st_ref[...]
        outs = []
        for n, c in enumerate(chunks):
            outs.append(o_intra[n] + _dot_nt(qe[c], st))
            st = st * jnp.exp(b_last[n]) + d_st[n]
        st_ref[...] = st
        o = jnp.concatenate(outs, axis=0)
        out = _head_rms(o, mh_ref[...], gain) * (gg * _sigmoid(gg))
        o_ref[0, pl.ds(r0, R), :] = out.astype(o_ref.dtype)
        return carry

    lax.fori_loop(0, ngroup, group, 0)


def _hgrn_mixer(hg, lb, gain, lg, nsub=8):
    b, s, w = hg.shape
    gw = GROUP_W
    assert lg % (CHUNK * nsub) == 0
    mall = jnp.asarray(_hgrn_level_mats(), BF16)
    mh = jnp.asarray(_head_block(1.0 / DH), BF16)
    fixed = lambda i, j: (0, 0)
    return pl.pallas_call(
        functools.partial(_hgrn_kernel, ngroup=lg // (CHUNK * nsub), nsub=nsub),
        grid=(b, s // lg),
        in_specs=[pl.BlockSpec((1, lg, w), lambda i, j: (i, j, 0)),
                  pl.BlockSpec((1, gw), fixed), pl.BlockSpec((1, gw), fixed),
                  pl.BlockSpec(mall.shape, fixed), pl.BlockSpec((gw, gw), fixed)],
        out_specs=pl.BlockSpec((1, lg, gw), lambda i, j: (i, j, 0)),
        out_shape=jax.ShapeDtypeStruct((b, s, gw), BF16),
        scratch_shapes=[pltpu.VMEM((gw, gw), F32)],
        compiler_params=_cparams("parallel", "arbitrary"),
        name="hgrn2_mixer",
    )(hg, lb[None, :], gain[None, :], mall, mh)


def _fox_gate_kernel(g_ref, tri_ref, col_ref, rowo_ref, carry_ref):
    @pl.when(pl.program_id(1) == 0)
    def _():
        carry_ref[...] = jnp.zeros_like(carry_ref)

    lf = _log_sigmoid(g_ref[0])
    cs = _sel_dot(tri_ref[...], lf) + carry_ref[...]
    n = cs.shape[0]
    carry_ref[...] = cs[n - 1:n]
    col_ref[0] = cs
    rowo_ref[0] = cs.T[0:8]


def _fox_gates(gt, lg):
    b, s, w = gt.shape
    tri = jnp.asarray(np.tril(np.ones((lg, lg), np.float32)), BF16)
    return pl.pallas_call(
        _fox_gate_kernel,
        grid=(b, s // lg),
        in_specs=[pl.BlockSpec((1, lg, w), lambda i, j: (i, j, 0)),
                  pl.BlockSpec((lg, lg), lambda i, j: (0, 0))],
        out_specs=[pl.BlockSpec((1, lg, w), lambda i, j: (i, j, 0)),
                   pl.BlockSpec((1, 8, lg), lambda i, j: (i, 0, j))],
        out_shape=[jax.ShapeDtypeStruct((b, s, w), F32), jax.ShapeDtypeStruct((b, 8, s), F32)],
        scratch_shapes=[pltpu.VMEM((1, w), F32)],
        compiler_params=_cparams("parallel", "arbitrary"),
        name="fox_gates",
    )(gt, tri)


def _fox_kernel(qt_ref, k_ref, vt_ref, ccol_ref, crow_ref, gain_ref, o_ref, *, tq, tk):
    qi = pl.program_id(1)
    gw = GROUP_W
    log2e = math.log2(math.e)
    qt = (qt_ref[...] * (DH ** -0.5 * log2e)).astype(BF16)
    head_row = lax.broadcasted_iota(jnp.int32, (gw, tq), 0) >> 6
    q_heads = [jnp.where(head_row == h, qt, jnp.zeros_like(qt)) for h in range(HEADS)]
    key_i = lax.broadcasted_iota(jnp.int32, (tk, tq), 0)
    qry_i = lax.broadcasted_iota(jnp.int32, (tk, tq), 1)
    cqs = [crow_ref[0, h:h + 1, :] * log2e for h in range(HEADS)]
    kv_per_q = tq // tk
    ones_rows = jnp.ones((16, tk), BF16)

    def run_tiles(tiles, state):
        state = list(state)

        def scores(j, diag_offset, h):
            r0 = pl.multiple_of(j * tk, tk)
            st = jnp.dot(k_ref[pl.ds(r0, tk), :], q_heads[h], preferred_element_type=F32)
            u = st - ccol_ref[0, pl.ds(r0, tk), h:h + 1] * log2e
            if diag_offset is not None:
                u = jnp.where(key_i + diag_offset <= qry_i, u, NEG_BIG)
            m_new = jnp.maximum(state[h][0], jnp.max(u, axis=0, keepdims=True) + cqs[h])
            return u, m_new

        def accumulate(j, h, u, m_new):
            r0 = pl.multiple_of(j * tk, tk)
            m_old, l_old, acc_old = state[h]
            alpha = jnp.exp2(m_old - m_new)
            p = jnp.exp2(u + (cqs[h] - m_new))
            vt1 = jnp.concatenate([vt_ref[h * DH:(h + 1) * DH, pl.ds(r0, tk)], ones_rows], axis=0)
            pv = jnp.dot(vt1, p.astype(BF16), preferred_element_type=F32)
            l_new = alpha * l_old + pv[DH:DH + 1]
            state[h] = (m_new, l_new, alpha * acc_old + pv[0:DH])

        items = [(j, off, h) for j, off in tiles for h in range(HEADS)]
        ahead = min(FOX_LOOKAHEAD, HEADS - 1)
        queue = [scores(*it) for it in items[:ahead]]
        for n, (j, off, h) in enumerate(items):
            if n + ahead < len(items):
                queue.append(scores(*items[n + ahead]))
            accumulate(j, h, *queue.pop(0))
        return tuple(state)

    init = tuple((jnp.full((1, tq), NEG_BIG, F32), jnp.zeros((1, tq), F32), jnp.zeros((DH, tq), F32))
                 for _ in range(HEADS))
    n_full = qi * kv_per_q
    state = lax.fori_loop(0, n_full // 2, lambda i, s: run_tiles([(2 * i, None), (2 * i + 1, None)], s), init)
    diag = [(n_full + d, d * tk) for d in range(kv_per_q)]
    state = lax.cond(n_full % 2 == 1,
                     lambda s: run_tiles([(n_full - 1, None)] + diag, s),
                     lambda s: run_tiles(diag, s), state)

    outs = []
    for h in range(HEADS):
        _, l_fin, acc_fin = state[h]
        o = acc_fin / l_fin
        ms = jnp.mean(o * o, axis=0, keepdims=True)
        outs.append(o * lax.rsqrt(ms + EPS))
    out = jnp.concatenate(outs, axis=0) * gain_ref[...]
    o_ref[...] = out.T.astype(o_ref.dtype)


def _fox_mixer(fqt, fk, fvt, cf_col, cf_row, gain, bsz, tq, tk):
    gw, t_rows = fqt.shape
    s = t_rows // bsz
    nq = s // tq
    return pl.pallas_call(
        functools.partial(_fox_kernel, tq=tq, tk=tk),
        grid=(bsz, nq),
        in_specs=[pl.BlockSpec((gw, tq), lambda i, qi: (0, i * nq + qi)),
                  pl.BlockSpec((s, gw), lambda i, qi: (i, 0)),
                  pl.BlockSpec((gw, s), lambda i, qi: (0, i)),
                  pl.BlockSpec((1, s, GATE_W), lambda i, qi: (i, 0, 0)),
                  pl.BlockSpec((1, 8, tq), lambda i, qi: (i, 0, qi)),
                  pl.BlockSpec((gw, 1), lambda i, qi: (0, 0))],
        out_specs=pl.BlockSpec((tq, gw), lambda i, qi: (i * nq + qi, 0)),
        out_shape=jax.ShapeDtypeStruct((t_rows, gw), BF16),
        compiler_params=_cparams("parallel", "arbitrary"),
        name="fox_mixer",
    )(fqt, fk, fvt, cf_col, cf_row, gain[:, None])


def _head_lane_max(x):
    rows = x.shape[0]
    parts = [jnp.broadcast_to(jnp.max(x[:, h * DH:(h + 1) * DH], axis=-1, keepdims=True), (rows, DH))
             for h in range(HEADS)]
    return jnp.concatenate(parts, axis=-1)


def _mlstm_kernel(x_ref, gt_ref, cw_ref, gain_ref, tri_ref, eb_ref, ei_ref, mh_ref, o_ref,
                  cbuf, ct_ref, m_ref, *, nsub, lg):
    L = CHUNK
    gw = GROUP_W

    @pl.when(pl.program_id(1) == 0)
    def _():
        cbuf[0:8, :] = jnp.zeros((8, 2 * gw), F32)
        ct_ref[...] = jnp.zeros_like(ct_ref)
        m_ref[...] = jnp.zeros_like(m_ref)

    cbuf[8:8 + lg, :] = x_ref[0, :, 0:2 * gw]
    acc = None
    for j in range(ML_CONV):
        term = cbuf[pl.ds(8 - (ML_CONV - 1) + j, lg), :] * cw_ref[j:j + 1, :]
        acc = term if acc is None else acc + term
    cbuf[0:8, :] = cbuf[lg:lg + 8, :]
    qk = acc * _sigmoid(acc)
    q = qk[:, 0:gw]
    k = qk[:, gw:2 * gw] * (DH ** -0.5)
    v = x_ref[0, :, 2 * gw:3 * gw]
    og = x_ref[0, :, 3 * gw:4 * gw]
    g = gt_ref[0]

    bd = _block_diag_mask(gw)
    bd2 = jnp.concatenate([bd, bd], axis=1)
    row = lax.broadcasted_iota(jnp.int32, (L, gw), 0)
    col = lax.broadcasted_iota(jnp.int32, (L, gw), 1) & (DH - 1)
    causal = col <= row
    diag = col == row
    ones = jnp.ones((L, gw), F32)
    chunks = [slice(c * L, (c + 1) * L) for c in range(nsub)]

    ones_blocks = jnp.where(bd, 1.0, 0.0).astype(BF16)

    def tile2(x):
        return jnp.concatenate([_tile_heads(x, bd), ones_blocks], axis=1)

    lsg = _log_sigmoid(g)
    cs = jnp.concatenate([_sel_dot(tri_ref[...], lsg[c]) for c in chunks], axis=0)
    b_exp = _dot_sel(cs, eb_ref[...])
    imb = _dot_sel(g, ei_ref[...]) - b_exp

    m_loc, nd_loc, b_last, m_src, d_ct = [], [], [], [], []
    for c in chunks:
        imb_row = jnp.sum(jnp.where(diag, imb[c], 0.0), axis=0, keepdims=True)
        d_log = jnp.where(causal, b_exp[c] + imb_row, NEG_BIG)
        ml = _head_lane_max(d_log)
        qk_loc = _dot_nt(q[c], _tile_heads(k[c], bd)) * jnp.exp(d_log - ml)
        v2 = jnp.concatenate([v[c], ones], axis=1)
        nd_loc.append(_dot(qk_loc, tile2(v[c])))
        m_loc.append(ml)
        bl = b_exp[c][L - 1:L]
        src = bl + imb[c]
        ms = jnp.max(src, axis=0, keepdims=True)
        kw = k[c] * jnp.exp(src - ms)
        d_ct.append(jnp.where(bd2, _dot_tn(kw, v2), 0.0))
        b_last.append(bl)
        m_src.append(ms)

    ct = ct_ref[...]
    m_prev = m_ref[...]
    hs = []
    for n, c in enumerate(chunks):
        inter = b_exp[c] + m_prev
        m_t = jnp.maximum(inter, m_loc[n])
        w_inter = jnp.exp(inter - m_t)
        w_loc = jnp.exp(m_loc[n] - m_t)
        nd = (jnp.concatenate([w_inter, w_inter], axis=1) * _dot(q[c], ct)
              + jnp.concatenate([w_loc, w_loc], axis=1) * nd_loc[n])
        hs.append(nd[:, 0:gw] / jnp.maximum(jnp.abs(nd[:, gw:2 * gw]), jnp.exp(-m_t)))
        m_new = jnp.maximum(b_last[n] + m_prev, m_src[n])
        decay = jnp.exp(b_last[n] + m_prev - m_new)
        w_src = jnp.exp(m_src[n] - m_new)
        ct = (ct * jnp.concatenate([decay, decay], axis=1)
              + d_ct[n] * jnp.concatenate([w_src, w_src], axis=1))
        m_prev = m_new
    ct_ref[...] = ct
    m_ref[...] = m_prev
    hh = jnp.concatenate(hs, axis=0)
    o_ref[0] = (_head_rms(hh, mh_ref[...], gain_ref[...]) * _sigmoid(og)).astype(o_ref.dtype)


def _mlstm_mixer(ml, gt, conv_w, gain, lg):
    b, s, w = ml.shape
    gw = GROUP_W
    assert lg % CHUNK == 0
    tri = jnp.asarray(np.tril(np.ones((CHUNK, CHUNK), np.float32)), BF16)
    eb = jnp.asarray(_gate_expand(2 * HEADS), BF16)
    ei = jnp.asarray(_gate_expand(HEADS), BF16)
    mh = jnp.asarray(_head_block(1.0 / DH), BF16)
    fixed = lambda i, j: (0, 0)
    blk = lambda i, j: (i, j, 0)
    return pl.pallas_call(
        functools.partial(_mlstm_kernel, nsub=lg // CHUNK, lg=lg),
        grid=(b, s // lg),
        in_specs=[pl.BlockSpec((1, lg, w), blk), pl.BlockSpec((1, lg, GATE_W), blk),
                  pl.BlockSpec((ML_CONV, 2 * gw), fixed), pl.BlockSpec((1, gw), fixed),
                  pl.BlockSpec((CHUNK, CHUNK), fixed), pl.BlockSpec((GATE_W, gw), fixed),
                  pl.BlockSpec((GATE_W, gw), fixed), pl.BlockSpec((gw, gw), fixed)],
        out_specs=pl.BlockSpec((1, lg, gw), blk),
        out_shape=jax.ShapeDtypeStruct((b, s, gw), BF16),
        scratch_shapes=[pltpu.VMEM((lg + 8, 2 * gw), F32), pltpu.VMEM((gw, 2 * gw), F32),
                        pltpu.VMEM((1, gw), F32)],
        compiler_params=_cparams("parallel", "arbitrary"),
        name="mlstm_mixer",
    )(ml, gt, conv_w, gain[None, :], tri, eb, ei, mh)


def _post_kernel(ya_ref, yb_ref, yc_ref, yd_ref, h_ref, wo_ref, gpost_ref, gpre_ref, wg_ref, wu_ref, wd_ref,
                 gffn_ref, o_ref, *, ff_chunk):
    gw = GROUP_W
    tm = h_ref.shape[0]
    halves = [slice(0, tm // 2), slice(tm // 2, tm)]
    h1, a = [], []
    for r in halves:
        mix = None
        for i, y_ref in enumerate((ya_ref, yb_ref, yc_ref, yd_ref)):
            t = jnp.dot(y_ref[r, :], wo_ref[i * gw:(i + 1) * gw, :], preferred_element_type=F32)
            mix = t if mix is None else mix + t
        h1.append(h_ref[r, :] + _rms(mix, gpost_ref[...]))
    for n in range(len(halves)):
        a.append(_rms(h1[n], gpre_ref[...]).astype(BF16))
    d_ff = wg_ref.shape[1]
    ff = [None] * len(halves)
    for c0 in range(0, d_ff, ff_chunk):
        c1 = min(c0 + ff_chunk, d_ff)
        gu = [(jnp.dot(a[n], wg_ref[:, c0:c1], preferred_element_type=F32),
               jnp.dot(a[n], wu_ref[:, c0:c1], preferred_element_type=F32)) for n in range(len(halves))]
        for n, (g, u) in enumerate(gu):
            act = (g * _sigmoid(g) * u).astype(BF16)
            t = jnp.dot(act, wd_ref[c0:c1, :], preferred_element_type=F32)
            ff[n] = t if ff[n] is None else ff[n] + t
    for n, r in enumerate(halves):
        o_ref[r, :] = h1[n] + _rms(ff[n], gffn_ref[...])


def _post(ya, yb, yc, yd, h, w_out, g_post, g_pre, w_gate, w_up, w_down, g_ffn, tm):
    t_rows, d = h.shape
    gw = GROUP_W
    seq = ya.shape[0]
    assert seq % tm == 0
    d_ff = w_gate.shape[1]
    ff_chunk = min(d_ff, -(-d_ff // (2 * MXU_TILE)) * MXU_TILE)
    row = lambda i: (i, 0)
    fixed = lambda i: (0, 0)
    once = pl.Buffered(1)
    wspec = lambda shape: pl.BlockSpec(shape, fixed, pipeline_mode=once)
    gspec = pl.BlockSpec((1, d), fixed)
    return pl.pallas_call(
        functools.partial(_post_kernel, ff_chunk=ff_chunk),
        grid=(t_rows // tm,),
        in_specs=[pl.BlockSpec((tm, gw), _time_major_map(seq // tm))] + [pl.BlockSpec((tm, gw), row)] * 3
        + [pl.BlockSpec((tm, d), row), wspec((d, d)), gspec, gspec,
           wspec((d, d_ff)), wspec((d, d_ff)), wspec((d_ff, d)), gspec],
        out_specs=pl.BlockSpec((tm, d), row),
        out_shape=jax.ShapeDtypeStruct((t_rows, d), F32),
        compiler_params=_cparams("parallel"),
        name="out_proj_ffn",
    )(ya, yb, yc, yd, h, w_out.astype(BF16), g_post[None, :], g_pre[None, :],
      w_gate.astype(BF16), w_up.astype(BF16), w_down.astype(BF16), g_ffn[None, :])


def kernel(x, w_in, gate_bias, s5_lambda_re, s5_lambda_im, s5_b_re, s5_b_im, s5_c_re, s5_c_im, s5_d, s5_log_dt,
           s5_w_glu, hgrn_lb_logits, mlstm_conv_w, mix_gain, w_out, ln_mix_pre, ln_mix_post, ln_ffn_pre,
           ln_ffn_post, w_ffn_gate, w_ffn_up, w_ffn_down):
    bsz, seq, d = x.shape
    depth = w_in.shape[0]
    gw = GROUP_W
    tm = min(512, seq)
    lg = min(512, seq)
    s5_lb = min(64, seq)
    fox_tq = min(512, seq)
    fox_tk = min(512, seq)

    lb_all = pl.pallas_call(_lb_kernel, out_shape=jax.ShapeDtypeStruct(hgrn_lb_logits.shape, F32),
                            name="hgrn_lower_bounds")(hgrn_lb_logits)

    h = x.reshape(bsz * seq, d)
    for l in range(depth):
        gain = mix_gain[l]
        u5, hg, fk, ml, gt, fqt, fvt = _in_proj(h, ln_mix_pre[l], w_in[l], gate_bias[l], tm, seq)
        lam, wb, cm = _s5_params(s5_lambda_re[l], s5_lambda_im[l], s5_b_re[l], s5_b_im[l],
                                 s5_c_re[l], s5_c_im[l], s5_log_dt[l])
        ya = _s5_mixer(u5, lam, wb, cm, s5_d[l], s5_w_glu[l], gain[0:gw], s5_lb, bsz)
        yb = _hgrn_mixer(hg.reshape(bsz, seq, 4 * gw), lb_all[l], gain[gw:2 * gw], lg)
        gt3 = gt.reshape(bsz, seq, GATE_W)
        cf_col, cf_row = _fox_gates(gt3, lg)
        yc = _fox_mixer(fqt, fk, fvt, cf_col, cf_row, gain[2 * gw:3 * gw], bsz, fox_tq, fox_tk)
        yd = _mlstm_mixer(ml.reshape(bsz, seq, 4 * gw), gt3, mlstm_conv_w[l], gain[3 * gw:4 * gw], lg)
        h = _post(ya, yb.reshape(bsz * seq, gw), yc, yd.reshape(bsz * seq, gw), h,
                  w_out[l], ln_mix_post[l], ln_ffn_pre[l], w_ffn_gate[l], w_ffn_up[l], w_ffn_down[l],
                  ln_ffn_post[l], tm)
    return h.reshape(bsz, seq, d)
```

```python
import functools
import math

import numpy as np
import jax
import jax.numpy as jnp
from jax import lax
from jax.experimental import pallas as pl
from jax.experimental.pallas import tpu as pltpu

F32 = jnp.float32
BF16 = jnp.bfloat16

EPS = 1e-6
NEG_BIG = -1e30
EXP_CLIP = 60.0

GROUP_W = 256
HEADS = 4
DH = GROUP_W // HEADS
S5_G, S5_P, S5_N = 16, 16, 64
ML_CONV = 4
CHUNK = 64
HG_LEVELS = (32, 16, 8, 4, 2, 1)
HG_SMALL_LEVELS = (4, 2, 1)
GATE_W = 128
FOX_LOOKAHEAD = 1

VMEM_LIMIT_BYTES = 56 * 1024 * 1024
MXU_TILE = 256


def _cparams(*sem):
    return pltpu.CompilerParams(dimension_semantics=sem, vmem_limit_bytes=VMEM_LIMIT_BYTES)


def _dot(a, b):
    return jnp.dot(a.astype(BF16), b.astype(BF16), preferred_element_type=F32)


def _dot_nt(a, b):
    return lax.dot_general(a.astype(BF16), b.astype(BF16), (((1,), (1,)), ((), ())),
                           preferred_element_type=F32)


def _dot_tn(a, b):
    return lax.dot_general(a.astype(BF16), b.astype(BF16), (((0,), (0,)), ((), ())),
                           preferred_element_type=F32)


def _split(x, n):
    parts, r = [], x
    for i in range(n):
        p = r.astype(BF16)
        parts.append(p)
        if i + 1 < n:
            r = r - p.astype(F32)
    return parts


def _sel_dot(m01, x, n=3):
    out = None
    for p in _split(x, n):
        t = jnp.dot(m01, p, preferred_element_type=F32)
        out = t if out is None else out + t
    return out


def _dot_sel(x, m01, n=3):
    mm = _halves_dot if (x.shape[0] >= 2 * MXU_TILE and m01.shape[1] <= MXU_TILE) else (
        lambda a, b: jnp.dot(a, b, preferred_element_type=F32))
    out = None
    for p in _split(x, n):
        t = mm(p, m01)
        out = t if out is None else out + t
    return out


def _halves_dot(a, b):
    half = a.shape[0] // 2
    return jnp.concatenate([jnp.dot(a[0:half], b, preferred_element_type=F32),
                            jnp.dot(a[half:], b, preferred_element_type=F32)], axis=0)


def _log_sigmoid(z):
    return jnp.minimum(z, 0.0) - jnp.log(1.0 + jnp.exp(-jnp.abs(z)))


def _sigmoid(z):
    return 1.0 / (1.0 + jnp.exp(-z))


def _rms(x, gain):
    ms = jnp.mean(x * x, axis=-1, keepdims=True)
    return x * lax.rsqrt(ms + EPS) * gain


def _head_rms(o, mh, gain):
    ms = _dot_sel(o * o, mh, 2)
    return o * lax.rsqrt(ms + EPS) * gain


def _block_diag_mask(n):
    r = lax.broadcasted_iota(jnp.int32, (n, n), 0)
    c = lax.broadcasted_iota(jnp.int32, (n, n), 1)
    return (r >> 6) == (c >> 6)


def _tile_heads(x, bd):
    return jnp.where(bd, jnp.concatenate([x, x, x, x], axis=0), 0.0).astype(BF16)


LANES = 128
PAIRS = GROUP_W // LANES


def _pair_tile(x, bdp):
    return jnp.where(bdp, jnp.concatenate([x, x], axis=0), 0.0).astype(BF16)


def _pairs(x):
    return [x[:, p * LANES:(p + 1) * LANES] for p in range(PAIRS)]


def _heads_nt(a, x, bdp):
    return jnp.concatenate([_dot_nt(ap, _pair_tile(xp, bdp)) for ap, xp in zip(_pairs(a), _pairs(x))], axis=1)


def _heads_nn(a, x, bdp):
    return jnp.concatenate([_dot(ap, _pair_tile(xp, bdp)) for ap, xp in zip(_pairs(a), _pairs(x))], axis=1)


def _hgrn_level_mats():
    L = CHUNK
    t = np.arange(L)[:, None]
    j = np.arange(L)[None, :]
    blocks = [j <= t]
    for m in HG_SMALL_LEVELS:
        ref = (t // (2 * m)) * 2 * m + m - 1
        blocks.append(j <= ref)
    return np.concatenate(blocks, axis=0).astype(np.float32)


def _head_block(value):
    i = np.arange(GROUP_W)
    return np.where((i[:, None] // DH) == (i[None, :] // DH), value, 0.0).astype(np.float32)


def _gate_expand(col0):
    e = np.zeros((GATE_W, GROUP_W), np.float32)
    for h in range(HEADS):
        e[col0 + h, h * DH:(h + 1) * DH] = 1.0
    return e


def _lb_kernel(logit_ref, o_ref):
    x = logit_ref[...]
    depth = x.shape[0]
    m = x[0:1]
    for l in range(1, depth):
        m = jnp.maximum(m, x[l:l + 1])
    e = [jnp.exp(x[l:l + 1] - m) for l in range(depth)]
    tot = e[0]
    for l in range(1, depth):
        tot = tot + e[l]
    p = [el / tot for el in e]
    c = None
    for l in range(depth):
        c = p[l] if c is None else c + p[l]
        o_ref[l:l + 1, :] = jnp.maximum(c - p[0], 0.0)


def _s5_param_kernel(lr_ref, li_ref, ldt_ref, bre_ref, bim_ref, abr_ref, abi_ref, bbr_ref, bbi_ref):
    lr = jnp.minimum(lr_ref[...], -1e-4)
    li = li_ref[...]
    dt = jnp.exp(ldt_ref[...])
    mag = jnp.exp(lr * dt)
    ab_re = mag * jnp.cos(li * dt)
    ab_im = mag * jnp.sin(li * dt)
    den = lr * lr + li * li
    cf_re = ((ab_re - 1.0) * lr + ab_im * li) / den
    cf_im = (ab_im * lr - (ab_re - 1.0) * li) / den
    bre = bre_ref[...]
    bim = bim_ref[...]
    abr_ref[...] = ab_re
    abi_ref[...] = ab_im
    bbr_ref[...] = cf_re * bre - cf_im * bim
    bbi_ref[...] = cf_re * bim + cf_im * bre


def _s5_params(lam_re, lam_im, b_re, b_im, c_re, c_im, log_dt):
    gp = S5_G * S5_P
    rep = lambda a: jnp.repeat(a, S5_P, axis=0)
    ldt = jnp.broadcast_to(rep(log_dt[:, None]), (gp, S5_N))
    bt = lambda a: a.transpose(0, 2, 1).reshape(gp, S5_N)
    shp = jax.ShapeDtypeStruct((gp, S5_N), F32)
    ab_re, ab_im, bb_re, bb_im = pl.pallas_call(
        _s5_param_kernel, out_shape=(shp, shp, shp, shp), name="s5_params",
    )(rep(lam_re), rep(lam_im), ldt, bt(b_re), bt(b_im))
    lam = jnp.stack([ab_re[::S5_P].reshape(-1), ab_im[::S5_P].reshape(-1)])
    own_gp = jnp.asarray(np.eye(S5_G, dtype=np.float32).repeat(S5_P, axis=0))
    own_gn = jnp.asarray(np.eye(S5_G, dtype=np.float32).repeat(S5_N, axis=0))
    wide = lambda bb: (bb[:, None, :] * own_gp[:, :, None]).reshape(gp, S5_G * S5_N)
    wb = jnp.concatenate([wide(bb_re), wide(bb_im)], axis=1).astype(BF16)
    tall = lambda c: (c.transpose(0, 2, 1).reshape(S5_G * S5_N, S5_P)[:, None, :]
                      * own_gn[:, :, None]).reshape(S5_G * S5_N, gp)
    cm = jnp.concatenate([tall(c_re), -tall(c_im)], axis=0).astype(BF16)
    return lam, wb, cm


IN_COLS = (GROUP_W, 4 * GROUP_W, GROUP_W, 4 * GROUP_W)
IN_DTYPES = (F32, F32, BF16, F32)
GATE_ROWS = 16


def _in_proj_kernel(x_ref, g_ref, w_ref, wt_ref, o_u5, o_hg, o_fk, o_ml, o_fqt, o_fvt, o_gt):
    gw = GROUP_W
    a = _rms(x_ref[...], g_ref[...]).astype(BF16)
    outs = (o_u5, o_hg, o_fk, o_ml)
    c0 = 0
    for o_ref, width in zip(outs, IN_COLS):
        o_ref[...] = jnp.dot(a, w_ref[:, c0:c0 + width], preferred_element_type=F32).astype(o_ref.dtype)
        c0 += width
    r = lax.dot_general(wt_ref[...], a, (((1,), (1,)), ((), ())), preferred_element_type=F32)
    o_fqt[...] = r[0:gw]
    o_fvt[...] = r[gw:2 * gw].astype(o_fvt.dtype)
    o_gt[...] = r[2 * gw:2 * gw + GATE_ROWS]


def _time_major_map(tiles_per_seq):
    return lambda i: (i % tiles_per_seq, i // tiles_per_seq)


def _in_proj(h, gain, w_in_l, tm, seq):
    t_rows, d = h.shape
    assert seq % tm == 0 and t_rows % seq == 0
    gw = GROUP_W
    o_fox_f = 8 * gw
    o_ml = o_fox_f + HEADS
    o_ml_i = o_ml + 4 * gw
    o_ml_f = o_ml_i + HEADS
    gates = jnp.concatenate([w_in_l[:, o_fox_f:o_fox_f + HEADS], w_in_l[:, o_ml_i:o_ml_i + HEADS],
                             w_in_l[:, o_ml_f:o_ml_f + HEADS],
                             jnp.zeros((d, GATE_ROWS - 3 * HEADS), w_in_l.dtype)], axis=1)
    w = jnp.concatenate([w_in_l[:, :5 * gw], w_in_l[:, 6 * gw:7 * gw], w_in_l[:, o_ml:o_ml + 4 * gw]],
                        axis=1).astype(BF16)
    wt = jnp.concatenate([w_in_l[:, 5 * gw:6 * gw], w_in_l[:, 7 * gw:8 * gw], gates], axis=1).T.astype(BF16)
    n_tot = sum(IN_COLS)
    row = lambda i: (i, 0)
    colb = lambda i: (0, i)
    fixed = lambda i: (0, 0)
    return pl.pallas_call(
        _in_proj_kernel,
        grid=(t_rows // tm,),
        in_specs=[pl.BlockSpec((tm, d), row), pl.BlockSpec((1, d), fixed),
                  pl.BlockSpec((d, n_tot), fixed), pl.BlockSpec((2 * gw + GATE_ROWS, d), fixed)],
        out_specs=[pl.BlockSpec((tm, gw), _time_major_map(seq // tm))]
        + [pl.BlockSpec((tm, c), row) for c in IN_COLS[1:]]
        + [pl.BlockSpec((gw, tm), colb), pl.BlockSpec((gw, tm), colb), pl.BlockSpec((GATE_ROWS, tm), colb)],
        out_shape=[jax.ShapeDtypeStruct((seq, (t_rows // seq) * gw), F32)]
        + [jax.ShapeDtypeStruct((t_rows, c), dt) for c, dt in zip(IN_COLS[1:], IN_DTYPES[1:])]
        + [jax.ShapeDtypeStruct((gw, t_rows), F32), jax.ShapeDtypeStruct((gw, t_rows), BF16),
           jax.ShapeDtypeStruct((GATE_ROWS, t_rows), F32)],
        compiler_params=_cparams("parallel"),
        name="in_proj",
    )(h, gain[None, :], w, wt)


def _s5_kernel(u_ref, perm_ref, permt_ref, wb_ref, lam_ref, cm_ref, d_ref, wglu_ref, gain_ref, o_ref,
               xs0_ref, xs1_ref, st_ref, *, hb, nb):
    gw = GROUP_W
    ns = S5_G * S5_N
    xs_refs = (xs0_ref, xs1_ref)

    @pl.when(pl.program_id(0) == 0)
    def _():
        st_ref[...] = jnp.zeros_like(st_ref)

    ar = jnp.broadcast_to(lam_ref[0:1, :], (nb, ns))
    ai = jnp.broadcast_to(lam_ref[1:2, :], (nb, ns))

    def front(k):
        u = jnp.concatenate([u_ref[k * hb:(k + 1) * hb, b * gw:(b + 1) * gw] for b in range(nb)], axis=0)
        u_tb = _halves_dot(perm_ref[...], u.astype(BF16)).astype(BF16)
        xs_refs[k][...] = jnp.dot(u_tb, wb_ref[...], preferred_element_type=F32)
        return u

    def scan(k, xr, xi):
        xs = xs_refs[k]
        for t in range(hb):
            r = slice(t * nb, (t + 1) * nb)
            nr = ar * xr - ai * xi + xs[r, 0:ns]
            ni = ar * xi + ai * xr + xs[r, ns:2 * ns]
            xs[r, 0:ns] = nr
            xs[r, ns:2 * ns] = ni
            xr, xi = nr, ni
        return xr, xi

    def back(k, u):
        cx_tb = _halves_dot(xs_refs[k][...].astype(BF16), cm_ref[...])
        cx = None
        for part in _split(cx_tb, 2):
            t = _halves_dot(permt_ref[...], part)
            cx = t if cx is None else cx + t
        y = cx + d_ref[...] * u
        g = jax.nn.gelu(y)
        y = g * _sigmoid(_halves_dot(g.astype(BF16), wglu_ref[...]))
        out = _rms(y, gain_ref[...]).astype(o_ref.dtype)
        for b in range(nb):
            o_ref[k * hb:(k + 1) * hb, b * gw:(b + 1) * gw] = out[b * hb:(b + 1) * hb]

    u0 = front(0)
    u1 = front(1)
    xr, xi = scan(0, st_ref[:, 0:ns], st_ref[:, ns:2 * ns])
    back(0, u0)
    xr, xi = scan(1, xr, xi)
    st_ref[:, 0:ns] = xr
    st_ref[:, ns:2 * ns] = xi
    back(1, u1)


def _s5_mixer(u2d, lam, wb, cm, d_skip, w_glu, gain, lb, nb):
    s, w = u2d.shape
    gw = GROUP_W
    assert w == nb * gw
    ns2 = 2 * S5_G * S5_N
    assert lb % 2 == 0 and s % lb == 0
    hb = lb // 2
    idx = np.arange(hb * nb)
    perm = np.zeros((hb * nb, hb * nb), np.float32)
    perm[idx, (idx % nb) * hb + idx // nb] = 1.0
    fixed = lambda i: (0, 0)
    return pl.pallas_call(
        functools.partial(_s5_kernel, hb=hb, nb=nb),
        grid=(s // lb,),
        in_specs=[pl.BlockSpec((lb, w), lambda i: (i, 0)),
                  pl.BlockSpec(perm.shape, fixed), pl.BlockSpec(perm.shape, fixed),
                  pl.BlockSpec((gw, ns2), fixed), pl.BlockSpec((2, ns2 // 2), fixed),
                  pl.BlockSpec((ns2, gw), fixed), pl.BlockSpec((1, gw), fixed),
                  pl.BlockSpec((gw, gw), fixed), pl.BlockSpec((1, gw), fixed)],
        out_specs=pl.BlockSpec((lb, w), lambda i: (i, 0)),
        out_shape=jax.ShapeDtypeStruct((s, w), BF16),
        scratch_shapes=[pltpu.VMEM((hb * nb, ns2), F32), pltpu.VMEM((hb * nb, ns2), F32),
                        pltpu.VMEM((nb, ns2), F32)],
        compiler_params=_cparams("arbitrary"),
        name="s5_mixer",
    )(u2d, jnp.asarray(perm, BF16), jnp.asarray(perm.T, BF16), wb, lam, cm, d_skip[None, :],
      w_glu.astype(BF16), gain[None, :])


def _hgrn_kernel(x_ref, lb_ref, gain_ref, mall_ref, mh_ref, o_ref, st_ref, *, ngroup, nsub):
    L = CHUNK
    gw = GROUP_W
    R = nsub * L

    @pl.when(pl.program_id(1) == 0)
    def _():
        st_ref[...] = jnp.zeros_like(st_ref)

    lb = lb_ref[...]
    gain = gain_ref[...]
    bdp = _block_diag_mask(LANES)
    row = lax.broadcasted_iota(jnp.int32, (R, gw), 0) & (L - 1)
    row_c = lax.broadcasted_iota(jnp.int32, (L, gw), 0)
    col_c = lax.broadcasted_iota(jnp.int32, (L, gw), 1) & (DH - 1)
    chunks = [slice(c * L, (c + 1) * L) for c in range(nsub)]

    def group(gi, carry):
        r0 = pl.multiple_of(gi * R, R)
        q = x_ref[0, pl.ds(r0, R), 0:gw]
        z = x_ref[0, pl.ds(r0, R), gw:2 * gw]
        v = x_ref[0, pl.ds(r0, R), 2 * gw:3 * gw]
        gg = x_ref[0, pl.ds(r0, R), 3 * gw:4 * gw]
        logf = _log_sigmoid(z) + jnp.log(1.0 + lb * jnp.exp(jnp.minimum(-z, EXP_CLIP)))
        kk = (1.0 - lb) * _sigmoid(-z)
        cums = [_sel_dot(mall_ref[...], logf[c]) for c in chunks]
        b = jnp.concatenate([cm[0:L] for cm in cums], axis=0)
        a = [jnp.where(row_c == col_c, _heads_nt(q[c], kk[c], bdp), 0.0) for c in chunks]
        for m in HG_LEVELS:
            if m in HG_SMALL_LEVELS:
                i = 1 + HG_SMALL_LEVELS.index(m)
                b_ref = jnp.concatenate([cm[i * L:(i + 1) * L] for cm in cums], axis=0)
            else:
                b_ref = jnp.concatenate([jnp.broadcast_to(b[c0 + m - 1:c0 + m], (2 * m, gw))
                                         for c0 in range(0, R, 2 * m)], axis=0)
            upper = (row & m) != 0
            w = jnp.where(upper, q, kk) * jnp.exp(-jnp.abs(b - b_ref))
            ql = jnp.where(upper, w, 0.0)
            kl = jnp.where(upper, 0.0, w)
            sh = int(math.log2(2 * m))
            same = (row_c >> sh) == (col_c >> sh)
            a = [a[n] + jnp.where(same, _heads_nt(ql[c], kl[c], bdp), 0.0) for n, c in enumerate(chunks)]
        o_intra = [_heads_nn(a[n], v[c], bdp) for n, c in enumerate(chunks)]
        b_last = [b[(n + 1) * L - 1:(n + 1) * L] for n in range(nsub)]
        kdec = kk * jnp.exp(jnp.concatenate([jnp.broadcast_to(bl, (L, gw)) for bl in b_last], axis=0) - b)
        d_st = [[jnp.where(bdp, _dot_tn(vp, kp), 0.0) for vp, kp in zip(_pairs(v[c]), _pairs(kdec[c]))]
                for c in chunks]
        qe = q * jnp.exp(b)
        st = [st_ref[p] for p in range(PAIRS)]
        outs = []
        for n, c in enumerate(chunks):
            o_inter = jnp.concatenate([_dot_nt(qp, st[p]) for p, qp in enumerate(_pairs(qe[c]))], axis=1)
            outs.append(o_intra[n] + o_inter)
            decay = _pairs(jnp.exp(b_last[n]))
            st = [st[p] * decay[p] + d_st[n][p] for p in range(PAIRS)]
        for p in range(PAIRS):
            st_ref[p] = st[p]
        o = jnp.concatenate(outs, axis=0)
        out = _head_rms(o, mh_ref[...], gain) * (gg * _sigmoid(gg))
        o_ref[0, pl.ds(r0, R), :] = out.astype(o_ref.dtype)
        return carry

    lax.fori_loop(0, ngroup, group, 0)


def _hgrn_mixer(hg, lb, gain, lg, nsub=8):
    b, s, w = hg.shape
    gw = GROUP_W
    assert lg % (CHUNK * nsub) == 0
    mall = jnp.asarray(_hgrn_level_mats(), BF16)
    mh = jnp.asarray(_head_block(1.0 / DH), BF16)
    fixed = lambda i, j: (0, 0)
    return pl.pallas_call(
        functools.partial(_hgrn_kernel, ngroup=lg // (CHUNK * nsub), nsub=nsub),
        grid=(b, s // lg),
        in_specs=[pl.BlockSpec((1, lg, w), lambda i, j: (i, j, 0)),
                  pl.BlockSpec((1, gw), fixed), pl.BlockSpec((1, gw), fixed),
                  pl.BlockSpec(mall.shape, fixed), pl.BlockSpec((gw, gw), fixed)],
        out_specs=pl.BlockSpec((1, lg, gw), lambda i, j: (i, j, 0)),
        out_shape=jax.ShapeDtypeStruct((b, s, gw), BF16),
        scratch_shapes=[pltpu.VMEM((PAIRS, LANES, LANES), F32)],
        compiler_params=_cparams("parallel", "arbitrary"),
        name="hgrn2_mixer",
    )(hg, lb[None, :], gain[None, :], mall, mh)


def _gate_kernel(gt_ref, bias_ref, triu_ref, col_ref, rowo_ref, grow_ref, carry_ref):
    @pl.when(pl.program_id(1) == 0)
    def _():
        carry_ref[...] = jnp.zeros_like(carry_ref)

    g = gt_ref[...] + bias_ref[...]
    cs = _dot_sel(_log_sigmoid(g), triu_ref[...]) + carry_ref[...]
    lg = cs.shape[1]
    carry_ref[...] = cs[:, lg - 1:lg]
    rowo_ref[0] = cs[0:8]
    pad = jnp.zeros((GATE_W - GATE_ROWS, lg), F32)
    col_ref[0] = jnp.concatenate([cs, pad], axis=0).T
    grow_ref[0] = jnp.concatenate([g, pad], axis=0).T


def _gates(gt, gate_bias_l, bsz, lg):
    rows, t_rows = gt.shape
    s = t_rows // bsz
    nblk = s // lg
    triu = jnp.asarray(np.triu(np.ones((lg, lg), np.float32)), BF16)
    bias = jnp.concatenate([gate_bias_l, jnp.zeros((rows - gate_bias_l.shape[0],), F32)])[:, None]
    return pl.pallas_call(
        _gate_kernel,
        grid=(bsz, nblk),
        in_specs=[pl.BlockSpec((rows, lg), lambda i, j: (0, i * nblk + j)),
                  pl.BlockSpec((rows, 1), lambda i, j: (0, 0)),
                  pl.BlockSpec((lg, lg), lambda i, j: (0, 0))],
        out_specs=[pl.BlockSpec((1, lg, GATE_W), lambda i, j: (i, j, 0)),
                   pl.BlockSpec((1, 8, lg), lambda i, j: (i, 0, j)),
                   pl.BlockSpec((1, lg, GATE_W), lambda i, j: (i, j, 0))],
        out_shape=[jax.ShapeDtypeStruct((bsz, s, GATE_W), F32), jax.ShapeDtypeStruct((bsz, 8, s), F32),
                   jax.ShapeDtypeStruct((bsz, s, GATE_W), F32)],
        scratch_shapes=[pltpu.VMEM((rows, 1), F32)],
        compiler_params=_cparams("parallel", "arbitrary"),
        name="gate_cumsums",
    )(gt, bias, triu)


def _fox_kernel(qt_ref, k_ref, vt_ref, ccol_ref, crow_ref, gain_ref, o_ref, *, tq, tk):
    qi = pl.program_id(1)
    gw = GROUP_W
    log2e = math.log2(math.e)
    qt = (qt_ref[...] * (DH ** -0.5 * log2e)).astype(BF16)
    head_row = lax.broadcasted_iota(jnp.int32, (gw, tq), 0) >> 6
    q_heads = [jnp.where(head_row == h, qt, jnp.zeros_like(qt)) for h in range(HEADS)]
    key_i = lax.broadcasted_iota(jnp.int32, (tk, tq), 0)
    qry_i = lax.broadcasted_iota(jnp.int32, (tk, tq), 1)
    cqs = [crow_ref[0, h:h + 1, :] * log2e for h in range(HEADS)]
    kv_per_q = tq // tk
    ones_rows = jnp.ones((16, tk), BF16)

    def run_tiles(tiles, state):
        state = list(state)

        def scores(j, diag_offset, h):
            r0 = pl.multiple_of(j * tk, tk)
            st = jnp.dot(k_ref[pl.ds(r0, tk), :], q_heads[h], preferred_element_type=F32)
            u = st - ccol_ref[0, pl.ds(r0, tk), h:h + 1] * log2e
            if diag_offset is not None:
                u = jnp.where(key_i + diag_offset <= qry_i, u, NEG_BIG)
            m_new = jnp.maximum(state[h][0], jnp.max(u, axis=0, keepdims=True) + cqs[h])
            return u, m_new

        def accumulate(j, h, u, m_new):
            r0 = pl.multiple_of(j * tk, tk)
            m_old, l_old, acc_old = state[h]
            alpha = jnp.exp2(m_old - m_new)
            p = jnp.exp2(u + (cqs[h] - m_new))
            vt1 = jnp.concatenate([vt_ref[h * DH:(h + 1) * DH, pl.ds(r0, tk)], ones_rows], axis=0)
            pv = jnp.dot(vt1, p.astype(BF16), preferred_element_type=F32)
            l_new = alpha * l_old + pv[DH:DH + 1]
            state[h] = (m_new, l_new, alpha * acc_old + pv[0:DH])

        items = [(j, off, h) for j, off in tiles for h in range(HEADS)]
        ahead = min(FOX_LOOKAHEAD, HEADS - 1)
        queue = [scores(*it) for it in items[:ahead]]
        for n, (j, off, h) in enumerate(items):
            if n + ahead < len(items):
                queue.append(scores(*items[n + ahead]))
            accumulate(j, h, *queue.pop(0))
        return tuple(state)

    init = tuple((jnp.full((1, tq), NEG_BIG, F32), jnp.zeros((1, tq), F32), jnp.zeros((DH, tq), F32))
                 for _ in range(HEADS))
    n_full = qi * kv_per_q
    state = lax.fori_loop(0, n_full // 2, lambda i, s: run_tiles([(2 * i, None), (2 * i + 1, None)], s), init)
    diag = [(n_full + d, d * tk) for d in range(kv_per_q)]
    state = lax.cond(n_full % 2 == 1,
                     lambda s: run_tiles([(n_full - 1, None)] + diag, s),
                     lambda s: run_tiles(diag, s), state)

    outs = []
    for h in range(HEADS):
        _, l_fin, acc_fin = state[h]
        o = acc_fin / l_fin
        ms = jnp.mean(o * o, axis=0, keepdims=True)
        outs.append(o * lax.rsqrt(ms + EPS))
    out = jnp.concatenate(outs, axis=0) * gain_ref[...]
    o_ref[...] = out.T.astype(o_ref.dtype)


def _fox_mixer(fqt, fk, fvt, cf_col, cf_row, gain, bsz, tq, tk):
    gw, t_rows = fqt.shape
    s = t_rows // bsz
    nq = s // tq
    return pl.pallas_call(
        functools.partial(_fox_kernel, tq=tq, tk=tk),
        grid=(bsz, nq),
        in_specs=[pl.BlockSpec((gw, tq), lambda i, qi: (0, i * nq + qi)),
                  pl.BlockSpec((s, gw), lambda i, qi: (i, 0)),
                  pl.BlockSpec((gw, s), lambda i, qi: (0, i)),
                  pl.BlockSpec((1, s, GATE_W), lambda i, qi: (i, 0, 0)),
                  pl.BlockSpec((1, 8, tq), lambda i, qi: (i, 0, qi)),
                  pl.BlockSpec((gw, 1), lambda i, qi: (0, 0))],
        out_specs=pl.BlockSpec((tq, gw), lambda i, qi: (i * nq + qi, 0)),
        out_shape=jax.ShapeDtypeStruct((t_rows, gw), BF16),
        compiler_params=_cparams("parallel", "arbitrary"),
        name="fox_mixer",
    )(fqt, fk, fvt, cf_col, cf_row, gain[:, None])


def _head_lane_max(x):
    rows = x.shape[0]
    parts = [jnp.broadcast_to(jnp.max(x[:, h * DH:(h + 1) * DH], axis=-1, keepdims=True), (rows, DH))
             for h in range(HEADS)]
    return jnp.concatenate(parts, axis=-1)


def _mlstm_kernel(x_ref, gt_ref, cw_ref, gain_ref, tri_ref, eb_ref, ei_ref, mh_ref, o_ref,
                  cbuf, ct_ref, m_ref, *, nsub, lg):
    L = CHUNK
    gw = GROUP_W

    @pl.when(pl.program_id(1) == 0)
    def _():
        cbuf[0:8, :] = jnp.zeros((8, 2 * gw), F32)
        ct_ref[...] = jnp.zeros_like(ct_ref)
        m_ref[...] = jnp.zeros_like(m_ref)

    cbuf[8:8 + lg, :] = x_ref[0, :, 0:2 * gw]
    acc = None
    for j in range(ML_CONV):
        term = cbuf[pl.ds(8 - (ML_CONV - 1) + j, lg), :] * cw_ref[j:j + 1, :]
        acc = term if acc is None else acc + term
    cbuf[0:8, :] = cbuf[lg:lg + 8, :]
    qk = acc * _sigmoid(acc)
    q = qk[:, 0:gw]
    k = qk[:, gw:2 * gw] * (DH ** -0.5)
    v = x_ref[0, :, 2 * gw:3 * gw]
    og = x_ref[0, :, 3 * gw:4 * gw]
    g = gt_ref[0]

    bdp = _block_diag_mask(LANES)
    bdp2 = jnp.concatenate([bdp, bdp], axis=1)
    row = lax.broadcasted_iota(jnp.int32, (L, gw), 0)
    col = lax.broadcasted_iota(jnp.int32, (L, gw), 1) & (DH - 1)
    causal = col <= row
    diag = col == row
    ones = jnp.ones((L, LANES), F32)
    ones_blocks = jnp.where(bdp, 1.0, 0.0).astype(BF16)
    chunks = [slice(c * L, (c + 1) * L) for c in range(nsub)]

    def regroup(parts):
        return jnp.concatenate([t[:, 0:LANES] for t in parts] + [t[:, LANES:2 * LANES] for t in parts], axis=1)

    lsg = _log_sigmoid(g)
    cs = jnp.concatenate([_sel_dot(tri_ref[...], lsg[c]) for c in chunks], axis=0)
    b_exp = _dot_sel(cs, eb_ref[...])
    imb = _dot_sel(g, ei_ref[...]) - b_exp

    m_loc, nd_loc, b_last, m_src, d_ct = [], [], [], [], []
    for c in chunks:
        imb_row = jnp.sum(jnp.where(diag, imb[c], 0.0), axis=0, keepdims=True)
        d_log = jnp.where(causal, b_exp[c] + imb_row, NEG_BIG)
        ml = _head_lane_max(d_log)
        qk_loc = _heads_nt(q[c], k[c], bdp) * jnp.exp(d_log - ml)
        nd_loc.append(regroup([_dot(qp, jnp.concatenate([_pair_tile(vp, bdp), ones_blocks], axis=1))
                               for qp, vp in zip(_pairs(qk_loc), _pairs(v[c]))]))
        m_loc.append(ml)
        bl = b_exp[c][L - 1:L]
        src = bl + imb[c]
        ms = jnp.max(src, axis=0, keepdims=True)
        kw = k[c] * jnp.exp(src - ms)
        d_ct.append([jnp.where(bdp2, _dot_tn(kp, jnp.concatenate([vp, ones], axis=1)), 0.0)
                     for kp, vp in zip(_pairs(kw), _pairs(v[c]))])
        b_last.append(bl)
        m_src.append(ms)

    ct = [ct_ref[p] for p in range(PAIRS)]
    m_prev = m_ref[...]
    hs = []
    for n, c in enumerate(chunks):
        inter = b_exp[c] + m_prev
        m_t = jnp.maximum(inter, m_loc[n])
        w_inter = jnp.exp(inter - m_t)
        w_loc = jnp.exp(m_loc[n] - m_t)
        q_ct = regroup([_dot(qp, ct[p]) for p, qp in enumerate(_pairs(q[c]))])
        nd = (jnp.concatenate([w_inter, w_inter], axis=1) * q_ct
              + jnp.concatenate([w_loc, w_loc], axis=1) * nd_loc[n])
        hs.append(nd[:, 0:gw] / jnp.maximum(jnp.abs(nd[:, gw:2 * gw]), jnp.exp(-m_t)))
        m_new = jnp.maximum(b_last[n] + m_prev, m_src[n])
        decay = _pairs(jnp.exp(b_last[n] + m_prev - m_new))
        w_src = _pairs(jnp.exp(m_src[n] - m_new))
        ct = [ct[p] * jnp.concatenate([decay[p], decay[p]], axis=1)
              + d_ct[n][p] * jnp.concatenate([w_src[p], w_src[p]], axis=1) for p in range(PAIRS)]
        m_prev = m_new
    for p in range(PAIRS):
        ct_ref[p] = ct[p]
    m_ref[...] = m_prev
    hh = jnp.concatenate(hs, axis=0)
    o_ref[0] = (_head_rms(hh, mh_ref[...], gain_ref[...]) * _sigmoid(og)).astype(o_ref.dtype)


def _mlstm_mixer(ml, gt, conv_w, gain, lg):
    b, s, w = ml.shape
    gw = GROUP_W
    assert lg % CHUNK == 0
    tri = jnp.asarray(np.tril(np.ones((CHUNK, CHUNK), np.float32)), BF16)
    eb = jnp.asarray(_gate_expand(2 * HEADS), BF16)
    ei = jnp.asarray(_gate_expand(HEADS), BF16)
    mh = jnp.asarray(_head_block(1.0 / DH), BF16)
    fixed = lambda i, j: (0, 0)
    blk = lambda i, j: (i, j, 0)
    return pl.pallas_call(
        functools.partial(_mlstm_kernel, nsub=lg // CHUNK, lg=lg),
        grid=(b, s // lg),
        in_specs=[pl.BlockSpec((1, lg, w), blk), pl.BlockSpec((1, lg, GATE_W), blk),
                  pl.BlockSpec((ML_CONV, 2 * gw), fixed), pl.BlockSpec((1, gw), fixed),
                  pl.BlockSpec((CHUNK, CHUNK), fixed), pl.BlockSpec((GATE_W, gw), fixed),
                  pl.BlockSpec((GATE_W, gw), fixed), pl.BlockSpec((gw, gw), fixed)],
        out_specs=pl.BlockSpec((1, lg, gw), blk),
        out_shape=jax.ShapeDtypeStruct((b, s, gw), BF16),
        scratch_shapes=[pltpu.VMEM((lg + 8, 2 * gw), F32), pltpu.VMEM((PAIRS, LANES, 2 * LANES), F32),
                        pltpu.VMEM((1, gw), F32)],
        compiler_params=_cparams("parallel", "arbitrary"),
        name="mlstm_mixer",
    )(ml, gt, conv_w, gain[None, :], tri, eb, ei, mh)


def _post_kernel(ya_ref, yb_ref, yc_ref, yd_ref, h_ref, wo_ref, gpost_ref, gpre_ref, wg_ref, wu_ref, wd_ref,
                 gffn_ref, o_ref, *, ff_chunk):
    gw = GROUP_W
    tm = h_ref.shape[0]
    halves = [slice(0, tm // 2), slice(tm // 2, tm)]
    h1, a = [], []
    for r in halves:
        mix = None
        for i, y_ref in enumerate((ya_ref, yb_ref, yc_ref, yd_ref)):
            t = jnp.dot(y_ref[r, :], wo_ref[i * gw:(i + 1) * gw, :], preferred_element_type=F32)
            mix = t if mix is None else mix + t
        h1.append(h_ref[r, :] + _rms(mix, gpost_ref[...]))
    for n in range(len(halves)):
        a.append(_rms(h1[n], gpre_ref[...]).astype(BF16))
    d_ff = wg_ref.shape[1]
    ff = [None] * len(halves)
    for c0 in range(0, d_ff, ff_chunk):
        c1 = min(c0 + ff_chunk, d_ff)
        gu = [(jnp.dot(a[n], wg_ref[:, c0:c1], preferred_element_type=F32),
               jnp.dot(a[n], wu_ref[:, c0:c1], preferred_element_type=F32)) for n in range(len(halves))]
        for n, (g, u) in enumerate(gu):
            act = (g * _sigmoid(g) * u).astype(BF16)
            t = jnp.dot(act, wd_ref[c0:c1, :], preferred_element_type=F32)
            ff[n] = t if ff[n] is None else ff[n] + t
    for n, r in enumerate(halves):
        o_ref[r, :] = h1[n] + _rms(ff[n], gffn_ref[...])


def _post(ya, yb, yc, yd, h, w_out, g_post, g_pre, w_gate, w_up, w_down, g_ffn, tm):
    t_rows, d = h.shape
    gw = GROUP_W
    seq = ya.shape[0]
    assert seq % tm == 0
    d_ff = w_gate.shape[1]
    ff_chunk = min(d_ff, -(-d_ff // (2 * MXU_TILE)) * MXU_TILE)
    row = lambda i: (i, 0)
    fixed = lambda i: (0, 0)
    once = pl.Buffered(1)
    wspec = lambda shape: pl.BlockSpec(shape, fixed, pipeline_mode=once)
    gspec = pl.BlockSpec((1, d), fixed)
    return pl.pallas_call(
        functools.partial(_post_kernel, ff_chunk=ff_chunk),
        grid=(t_rows // tm,),
        in_specs=[pl.BlockSpec((tm, gw), _time_major_map(seq // tm))] + [pl.BlockSpec((tm, gw), row)] * 3
        + [pl.BlockSpec((tm, d), row), wspec((d, d)), gspec, gspec,
           wspec((d, d_ff)), wspec((d, d_ff)), wspec((d_ff, d)), gspec],
        out_specs=pl.BlockSpec((tm, d), row),
        out_shape=jax.ShapeDtypeStruct((t_rows, d), F32),
        compiler_params=_cparams("parallel"),
        name="out_proj_ffn",
    )(ya, yb, yc, yd, h, w_out.astype(BF16), g_post[None, :], g_pre[None, :],
      w_gate.astype(BF16), w_up.astype(BF16), w_down.astype(BF16), g_ffn[None, :])


def kernel(x, w_in, gate_bias, s5_lambda_re, s5_lambda_im, s5_b_re, s5_b_im, s5_c_re, s5_c_im, s5_d, s5_log_dt,
           s5_w_glu, hgrn_lb_logits, mlstm_conv_w, mix_gain, w_out, ln_mix_pre, ln_mix_post, ln_ffn_pre,
           ln_ffn_post, w_ffn_gate, w_ffn_up, w_ffn_down):
    bsz, seq, d = x.shape
    depth = w_in.shape[0]
    gw = GROUP_W
    tm = min(512, seq)
    lg = min(512, seq)
    s5_lb = min(128, seq)
    gate_lg = min(1024, seq)
    fox_tq = min(512, seq)
    fox_tk = min(512, seq)

    lb_all = pl.pallas_call(_lb_kernel, out_shape=jax.ShapeDtypeStruct(hgrn_lb_logits.shape, F32),
                            name="hgrn_lower_bounds")(hgrn_lb_logits)

    h = x.reshape(bsz * seq, d)
    for l in range(depth):
        gain = mix_gain[l]
        u5, hg, fk, ml, fqt, fvt, gt = _in_proj(h, ln_mix_pre[l], w_in[l], tm, seq)
        lam, wb, cm = _s5_params(s5_lambda_re[l], s5_lambda_im[l], s5_b_re[l], s5_b_im[l],
                                 s5_c_re[l], s5_c_im[l], s5_log_dt[l])
        ya = _s5_mixer(u5, lam, wb, cm, s5_d[l], s5_w_glu[l], gain[0:gw], s5_lb, bsz)
        yb = _hgrn_mixer(hg.reshape(bsz, seq, 4 * gw), lb_all[l], gain[gw:2 * gw], lg)
        cf_col, cf_row, gt3 = _gates(gt, gate_bias[l], bsz, gate_lg)
        yc = _fox_mixer(fqt, fk, fvt, cf_col, cf_row, gain[2 * gw:3 * gw], bsz, fox_tq, fox_tk)
        yd = _mlstm_mixer(ml.reshape(bsz, seq, 4 * gw), gt3, mlstm_conv_w[l], gain[3 * gw:4 * gw], lg)
        h = _post(ya, yb.reshape(bsz * seq, gw), yc, yd.reshape(bsz * seq, gw), h,
                  w_out[l], ln_mix_post[l], ln_ffn_pre[l], w_ffn_gate[l], w_ffn_up[l], w_ffn_down[l],
                  ln_ffn_post[l], tm)
    return h.reshape(bsz, seq, d)
```

```python
import functools
import math

import numpy as np
import jax
import jax.numpy as jnp
from jax import lax
from jax.experimental import pallas as pl
from jax.experimental.pallas import tpu as pltpu

F32 = jnp.float32
BF16 = jnp.bfloat16

EPS = 1e-6
NEG_BIG = -1e30
EXP_CLIP = 60.0

GROUP_W = 256
HEADS = 4
DH = GROUP_W // HEADS
S5_G, S5_P, S5_N = 16, 16, 64
ML_CONV = 4
CHUNK = 64
HG_LEVELS = (32, 16, 8, 4, 2, 1)
HG_SMALL_LEVELS = (4, 2, 1)
GATE_W = 128
FOX_LOOKAHEAD = 1
POST_ROWS = 256

VMEM_LIMIT_BYTES = 56 * 1024 * 1024
MXU_TILE = 256


def _cparams(*sem):
    return pltpu.CompilerParams(dimension_semantics=sem, vmem_limit_bytes=VMEM_LIMIT_BYTES)


def _dot(a, b):
    return jnp.dot(a.astype(BF16), b.astype(BF16), preferred_element_type=F32)


def _dot_nt(a, b):
    return lax.dot_general(a.astype(BF16), b.astype(BF16), (((1,), (1,)), ((), ())),
                           preferred_element_type=F32)


def _dot_tn(a, b):
    return lax.dot_general(a.astype(BF16), b.astype(BF16), (((0,), (0,)), ((), ())),
                           preferred_element_type=F32)


def _split(x, n):
    parts, r = [], x
    for i in range(n):
        p = r.astype(BF16)
        parts.append(p)
        if i + 1 < n:
            r = r - p.astype(F32)
    return parts


def _sel_dot(m01, x, n=3):
    out = None
    for p in _split(x, n):
        t = jnp.dot(m01, p, preferred_element_type=F32)
        out = t if out is None else out + t
    return out


def _dot_sel(x, m01, n=3):
    mm = _halves_dot if (x.shape[0] >= 2 * MXU_TILE and m01.shape[1] <= MXU_TILE) else (
        lambda a, b: jnp.dot(a, b, preferred_element_type=F32))
    out = None
    for p in _split(x, n):
        t = mm(p, m01)
        out = t if out is None else out + t
    return out


def _halves_dot(a, b):
    half = a.shape[0] // 2
    return jnp.concatenate([jnp.dot(a[0:half], b, preferred_element_type=F32),
                            jnp.dot(a[half:], b, preferred_element_type=F32)], axis=0)


def _log_sigmoid(z):
    return jnp.minimum(z, 0.0) - jnp.log(1.0 + jnp.exp(-jnp.abs(z)))


def _sigmoid(z):
    return 1.0 / (1.0 + jnp.exp(-z))


def _rms(x, gain):
    ms = jnp.mean(x * x, axis=-1, keepdims=True)
    return x * lax.rsqrt(ms + EPS) * gain


def _head_rms(o, mh, gain):
    ms = _dot_sel(o * o, mh, 2)
    return o * lax.rsqrt(ms + EPS) * gain


def _block_diag_mask(n):
    r = lax.broadcasted_iota(jnp.int32, (n, n), 0)
    c = lax.broadcasted_iota(jnp.int32, (n, n), 1)
    return (r >> 6) == (c >> 6)


LANES = 128
PAIRS = GROUP_W // LANES


def _pair_tile(x, bdp):
    return jnp.where(bdp, jnp.concatenate([x, x], axis=0), 0.0).astype(BF16)


def _pairs(x):
    return [x[:, p * LANES:(p + 1) * LANES] for p in range(PAIRS)]


def _heads_nt(a, x, bdp):
    return jnp.concatenate([_dot_nt(ap, _pair_tile(xp, bdp)) for ap, xp in zip(_pairs(a), _pairs(x))], axis=1)


def _heads_nn(a, x, bdp):
    return jnp.concatenate([_dot(ap, _pair_tile(xp, bdp)) for ap, xp in zip(_pairs(a), _pairs(x))], axis=1)


def _hgrn_level_mats():
    L = CHUNK
    t = np.arange(L)[:, None]
    j = np.arange(L)[None, :]
    blocks = [j <= t]
    for m in HG_SMALL_LEVELS:
        ref = (t // (2 * m)) * 2 * m + m - 1
        blocks.append(j <= ref)
    return np.concatenate(blocks, axis=0).astype(np.float32)


def _head_block(value):
    i = np.arange(GROUP_W)
    return np.where((i[:, None] // DH) == (i[None, :] // DH), value, 0.0).astype(np.float32)


def _gate_expand(col0):
    e = np.zeros((GATE_W, GROUP_W), np.float32)
    for h in range(HEADS):
        e[col0 + h, h * DH:(h + 1) * DH] = 1.0
    return e


def _lb_kernel(logit_ref, o_ref):
    x = logit_ref[...]
    depth = x.shape[0]
    m = x[0:1]
    for l in range(1, depth):
        m = jnp.maximum(m, x[l:l + 1])
    e = [jnp.exp(x[l:l + 1] - m) for l in range(depth)]
    tot = e[0]
    for l in range(1, depth):
        tot = tot + e[l]
    p = [el / tot for el in e]
    c = None
    for l in range(depth):
        c = p[l] if c is None else c + p[l]
        o_ref[l:l + 1, :] = jnp.maximum(c - p[0], 0.0)


def _s5_param_kernel(lr_ref, li_ref, ldt_ref, bre_ref, bim_ref, abr_ref, abi_ref, bbr_ref, bbi_ref):
    lr = jnp.minimum(lr_ref[...], -1e-4)
    li = li_ref[...]
    dt = jnp.exp(ldt_ref[...])
    mag = jnp.exp(lr * dt)
    ab_re = mag * jnp.cos(li * dt)
    ab_im = mag * jnp.sin(li * dt)
    den = lr * lr + li * li
    cf_re = ((ab_re - 1.0) * lr + ab_im * li) / den
    cf_im = (ab_im * lr - (ab_re - 1.0) * li) / den
    bre = bre_ref[...]
    bim = bim_ref[...]
    abr_ref[...] = ab_re
    abi_ref[...] = ab_im
    bbr_ref[...] = cf_re * bre - cf_im * bim
    bbi_ref[...] = cf_re * bim + cf_im * bre


def _s5_params(lam_re, lam_im, b_re, b_im, c_re, c_im, log_dt):
    gp = S5_G * S5_P
    rep = lambda a: jnp.repeat(a, S5_P, axis=0)
    ldt = jnp.broadcast_to(rep(log_dt[:, None]), (gp, S5_N))
    bt = lambda a: a.transpose(0, 2, 1).reshape(gp, S5_N)
    shp = jax.ShapeDtypeStruct((gp, S5_N), F32)
    ab_re, ab_im, bb_re, bb_im = pl.pallas_call(
        _s5_param_kernel, out_shape=(shp, shp, shp, shp), name="s5_params",
    )(rep(lam_re), rep(lam_im), ldt, bt(b_re), bt(b_im))
    lam = jnp.stack([ab_re[::S5_P].reshape(-1), ab_im[::S5_P].reshape(-1)])
    own_gp = jnp.asarray(np.eye(S5_G, dtype=np.float32).repeat(S5_P, axis=0))
    own_gn = jnp.asarray(np.eye(S5_G, dtype=np.float32).repeat(S5_N, axis=0))
    wide = lambda bb: (bb[:, None, :] * own_gp[:, :, None]).reshape(gp, S5_G * S5_N)
    wb = jnp.concatenate([wide(bb_re), wide(bb_im)], axis=1).astype(BF16)
    tall = lambda c: (c.transpose(0, 2, 1).reshape(S5_G * S5_N, S5_P)[:, None, :]
                      * own_gn[:, :, None]).reshape(S5_G * S5_N, gp)
    cm = jnp.concatenate([tall(c_re), -tall(c_im)], axis=0).astype(BF16)
    return lam, wb, cm


IN_COLS = (GROUP_W, 4 * GROUP_W, GROUP_W, 4 * GROUP_W)
IN_DTYPES = (F32, F32, BF16, F32)
GATE_ROWS = 16


def _in_proj_kernel(x_ref, g_ref, w_ref, wt_ref, o_u5, o_hg, o_fk, o_ml, o_fqt, o_fvt, o_gt):
    gw = GROUP_W
    a = _rms(x_ref[...], g_ref[...]).astype(BF16)
    outs = (o_u5, o_hg, o_fk, o_ml)
    c0 = 0
    for o_ref, width in zip(outs, IN_COLS):
        o_ref[...] = jnp.dot(a, w_ref[:, c0:c0 + width], preferred_element_type=F32).astype(o_ref.dtype)
        c0 += width
    t = lax.dot_general(wt_ref[...], a, (((1,), (1,)), ((), ())), preferred_element_type=F32)
    o_fqt[...] = t[0:gw]
    o_fvt[...] = t[gw:2 * gw].astype(o_fvt.dtype)
    o_gt[...] = t[2 * gw:2 * gw + GATE_ROWS]


def _time_major_map(tiles_per_seq):
    return lambda i: (i % tiles_per_seq, i // tiles_per_seq)


def _in_proj(h, gain, w_in_l, tm, seq):
    t_rows, d = h.shape
    assert seq % tm == 0 and t_rows % seq == 0
    gw = GROUP_W
    o_fox_f = 8 * gw
    o_ml = o_fox_f + HEADS
    o_ml_i = o_ml + 4 * gw
    o_ml_f = o_ml_i + HEADS
    gates = jnp.concatenate([w_in_l[:, o_fox_f:o_fox_f + HEADS], w_in_l[:, o_ml_i:o_ml_i + HEADS],
                             w_in_l[:, o_ml_f:o_ml_f + HEADS],
                             jnp.zeros((d, GATE_ROWS - 3 * HEADS), w_in_l.dtype)], axis=1)
    w = jnp.concatenate([w_in_l[:, :5 * gw], w_in_l[:, 6 * gw:7 * gw], w_in_l[:, o_ml:o_ml + 4 * gw]],
                        axis=1).astype(BF16)
    wt = jnp.concatenate([w_in_l[:, 5 * gw:6 * gw], w_in_l[:, 7 * gw:8 * gw], gates], axis=1).T.astype(BF16)
    n_tot = sum(IN_COLS)
    row = lambda i: (i, 0)
    colb = lambda i: (0, i)
    fixed = lambda i: (0, 0)
    return pl.pallas_call(
        _in_proj_kernel,
        grid=(t_rows // tm,),
        in_specs=[pl.BlockSpec((tm, d), row), pl.BlockSpec((1, d), fixed),
                  pl.BlockSpec((d, n_tot), fixed, pipeline_mode=pl.Buffered(1)),
                  pl.BlockSpec((2 * gw + GATE_ROWS, d), fixed, pipeline_mode=pl.Buffered(1))],
        out_specs=[pl.BlockSpec((tm, gw), _time_major_map(seq // tm))]
        + [pl.BlockSpec((tm, c), row) for c in IN_COLS[1:]]
        + [pl.BlockSpec((gw, tm), colb), pl.BlockSpec((gw, tm), colb), pl.BlockSpec((GATE_ROWS, tm), colb)],
        out_shape=[jax.ShapeDtypeStruct((seq, (t_rows // seq) * gw), F32)]
        + [jax.ShapeDtypeStruct((t_rows, c), dt) for c, dt in zip(IN_COLS[1:], IN_DTYPES[1:])]
        + [jax.ShapeDtypeStruct((gw, t_rows), F32), jax.ShapeDtypeStruct((gw, t_rows), BF16),
           jax.ShapeDtypeStruct((GATE_ROWS, t_rows), F32)],
        compiler_params=_cparams("parallel"),
        name="in_proj",
    )(h, gain[None, :], w, wt)


def _s5_kernel(u_ref, perm_ref, permt_ref, wb_ref, lam_ref, cm_ref, d_ref, wglu_ref, gain_ref, o_ref,
               xs0_ref, xs1_ref, st_ref, *, hb, nb):
    gw = GROUP_W
    ns = S5_G * S5_N
    xs_refs = (xs0_ref, xs1_ref)

    @pl.when(pl.program_id(0) == 0)
    def _():
        st_ref[...] = jnp.zeros_like(st_ref)

    ar = jnp.broadcast_to(lam_ref[0:1, :], (nb, ns))
    ai = jnp.broadcast_to(lam_ref[1:2, :], (nb, ns))

    def front(k):
        u = jnp.concatenate([u_ref[k * hb:(k + 1) * hb, b * gw:(b + 1) * gw] for b in range(nb)], axis=0)
        u_tb = _halves_dot(perm_ref[...], u.astype(BF16)).astype(BF16)
        xs_refs[k][...] = jnp.dot(u_tb, wb_ref[...], preferred_element_type=F32)
        return u

    def scan(k, xr, xi):
        xs = xs_refs[k]
        for t in range(hb):
            r = slice(t * nb, (t + 1) * nb)
            nr = ar * xr - ai * xi + xs[r, 0:ns]
            ni = ar * xi + ai * xr + xs[r, ns:2 * ns]
            xs[r, 0:ns] = nr
            xs[r, ns:2 * ns] = ni
            xr, xi = nr, ni
        return xr, xi

    def back(k, u):
        cx_tb = _halves_dot(xs_refs[k][...].astype(BF16), cm_ref[...])
        cx = None
        for part in _split(cx_tb, 2):
            t = _halves_dot(permt_ref[...], part)
            cx = t if cx is None else cx + t
        y = cx + d_ref[...] * u
        g = jax.nn.gelu(y)
        y = g * _sigmoid(_halves_dot(g.astype(BF16), wglu_ref[...]))
        out = _rms(y, gain_ref[...]).astype(o_ref.dtype)
        for b in range(nb):
            o_ref[k * hb:(k + 1) * hb, b * gw:(b + 1) * gw] = out[b * hb:(b + 1) * hb]

    u0 = front(0)
    u1 = front(1)
    xr, xi = scan(0, st_ref[:, 0:ns], st_ref[:, ns:2 * ns])
    back(0, u0)
    xr, xi = scan(1, xr, xi)
    st_ref[:, 0:ns] = xr
    st_ref[:, ns:2 * ns] = xi
    back(1, u1)


def _s5_mixer(u2d, lam, wb, cm, d_skip, w_glu, gain, lb, nb):
    s, w = u2d.shape
    gw = GROUP_W
    assert w == nb * gw
    ns2 = 2 * S5_G * S5_N
    assert lb % 2 == 0 and s % lb == 0
    hb = lb // 2
    idx = np.arange(hb * nb)
    perm = np.zeros((hb * nb, hb * nb), np.float32)
    perm[idx, (idx % nb) * hb + idx // nb] = 1.0
    fixed = lambda i: (0, 0)
    return pl.pallas_call(
        functools.partial(_s5_kernel, hb=hb, nb=nb),
        grid=(s // lb,),
        in_specs=[pl.BlockSpec((lb, w), lambda i: (i, 0)),
                  pl.BlockSpec(perm.shape, fixed), pl.BlockSpec(perm.shape, fixed),
                  pl.BlockSpec((gw, ns2), fixed), pl.BlockSpec((2, ns2 // 2), fixed),
                  pl.BlockSpec((ns2, gw), fixed), pl.BlockSpec((1, gw), fixed),
                  pl.BlockSpec((gw, gw), fixed), pl.BlockSpec((1, gw), fixed)],
        out_specs=pl.BlockSpec((lb, w), lambda i: (i, 0)),
        out_shape=jax.ShapeDtypeStruct((s, w), BF16),
        scratch_shapes=[pltpu.VMEM((hb * nb, ns2), F32), pltpu.VMEM((hb * nb, ns2), F32),
                        pltpu.VMEM((nb, ns2), F32)],
        compiler_params=_cparams("arbitrary"),
        name="s5_mixer",
    )(u2d, jnp.asarray(perm, BF16), jnp.asarray(perm.T, BF16), wb, lam, cm, d_skip[None, :],
      w_glu.astype(BF16), gain[None, :])


def _hgrn_kernel(x_ref, lb_ref, gain_ref, mall_ref, mh_ref, o_ref, st_ref, *, ngroup, nsub):
    L = CHUNK
    gw = GROUP_W
    R = nsub * L

    @pl.when(pl.program_id(1) == 0)
    def _():
        st_ref[...] = jnp.zeros_like(st_ref)

    lb = lb_ref[...]
    gain = gain_ref[...]
    bdp = _block_diag_mask(LANES)
    row = lax.broadcasted_iota(jnp.int32, (R, gw), 0) & (L - 1)
    row_c = lax.broadcasted_iota(jnp.int32, (L, gw), 0)
    col_c = lax.broadcasted_iota(jnp.int32, (L, gw), 1) & (DH - 1)
    chunks = [slice(c * L, (c + 1) * L) for c in range(nsub)]

    def group(gi, carry):
        r0 = pl.multiple_of(gi * R, R)
        q = x_ref[0, pl.ds(r0, R), 0:gw]
        z = x_ref[0, pl.ds(r0, R), gw:2 * gw]
        v = x_ref[0, pl.ds(r0, R), 2 * gw:3 * gw]
        gg = x_ref[0, pl.ds(r0, R), 3 * gw:4 * gw]
        logf = _log_sigmoid(z) + jnp.log(1.0 + lb * jnp.exp(jnp.minimum(-z, EXP_CLIP)))
        kk = (1.0 - lb) * _sigmoid(-z)
        cums = [_sel_dot(mall_ref[...], logf[c]) for c in chunks]
        bs = [cm[0:L] for cm in cums]
        b = jnp.concatenate(bs, axis=0)
        a = [jnp.where(row_c == col_c, _heads_nt(q[c], kk[c], bdp), 0.0) for c in chunks]
        for m in HG_LEVELS:
            upper = (row_c & m) != 0
            sh = int(math.log2(2 * m))
            same = (row_c >> sh) == (col_c >> sh)
            for n, c in enumerate(chunks):
                if m in HG_SMALL_LEVELS:
                    i = 1 + HG_SMALL_LEVELS.index(m)
                    b_ref = cums[n][i * L:(i + 1) * L]
                else:
                    b_ref = jnp.concatenate([jnp.broadcast_to(bs[n][c0 + m - 1:c0 + m], (2 * m, gw))
                                             for c0 in range(0, L, 2 * m)], axis=0)
                w = jnp.where(upper, q[c], kk[c]) * jnp.exp(-jnp.abs(bs[n] - b_ref))
                ql = jnp.where(upper, w, 0.0)
                kl = jnp.where(upper, 0.0, w)
                a[n] = a[n] + jnp.where(same, _heads_nt(ql, kl, bdp), 0.0)
        o_intra = [_heads_nn(a[n], v[c], bdp) for n, c in enumerate(chunks)]
        b_last = [b[(n + 1) * L - 1:(n + 1) * L] for n in range(nsub)]
        kdec = kk * jnp.exp(jnp.concatenate([jnp.broadcast_to(bl, (L, gw)) for bl in b_last], axis=0) - b)
        d_st = [[jnp.where(bdp, _dot_tn(vp, kp), 0.0) for vp, kp in zip(_pairs(v[c]), _pairs(kdec[c]))]
                for c in chunks]
        qe = q * jnp.exp(b)
        st = [st_ref[p] for p in range(PAIRS)]
        outs = []
        for n, c in enumerate(chunks):
            o_inter = jnp.concatenate([_dot_nt(qp, st[p]) for p, qp in enumerate(_pairs(qe[c]))], axis=1)
            outs.append(o_intra[n] + o_inter)
            decay = _pairs(jnp.exp(b_last[n]))
            st = [st[p] * decay[p] + d_st[n][p] for p in range(PAIRS)]
        for p in range(PAIRS):
            st_ref[p] = st[p]
        o = jnp.concatenate(outs, axis=0)
        out = _head_rms(o, mh_ref[...], gain) * (gg * _sigmoid(gg))
        o_ref[0, pl.ds(r0, R), :] = out.astype(o_ref.dtype)
        return carry

    lax.fori_loop(0, ngroup, group, 0)


def _hgrn_mixer(hg, lb, gain, lg, nsub=8):
    b, s, w = hg.shape
    gw = GROUP_W
    assert lg % (CHUNK * nsub) == 0
    mall = jnp.asarray(_hgrn_level_mats(), BF16)
    mh = jnp.asarray(_head_block(1.0 / DH), BF16)
    fixed = lambda i, j: (0, 0)
    return pl.pallas_call(
        functools.partial(_hgrn_kernel, ngroup=lg // (CHUNK * nsub), nsub=nsub),
        grid=(b, s // lg),
        in_specs=[pl.BlockSpec((1, lg, w), lambda i, j: (i, j, 0)),
                  pl.BlockSpec((1, gw), fixed), pl.BlockSpec((1, gw), fixed),
                  pl.BlockSpec(mall.shape, fixed), pl.BlockSpec((gw, gw), fixed)],
        out_specs=pl.BlockSpec((1, lg, gw), lambda i, j: (i, j, 0)),
        out_shape=jax.ShapeDtypeStruct((b, s, gw), BF16),
        scratch_shapes=[pltpu.VMEM((PAIRS, LANES, LANES), F32)],
        compiler_params=_cparams("parallel", "arbitrary"),
        name="hgrn2_mixer",
    )(hg, lb[None, :], gain[None, :], mall, mh)


FOX_SPLIT = 3


def _fox_bias_lane(h, j):
    return (h ^ 1) * DH + j


def _gate_kernel(gt_ref, bias_ref, triu_ref, place_ref, ones_ref, kb_ref, rowo_ref, grow_ref, carry_ref):
    @pl.when(pl.program_id(1) == 0)
    def _():
        carry_ref[...] = jnp.zeros_like(carry_ref)

    g = gt_ref[...] + bias_ref[...]
    cs = _dot_sel(_log_sigmoid(g), triu_ref[...]) + carry_ref[...]
    lg = cs.shape[1]
    carry_ref[...] = cs[:, lg - 1:lg]
    rowo_ref[0] = cs[0:8]
    pad = jnp.zeros((GATE_W - GATE_ROWS, lg), F32)
    grow_ref[0] = jnp.concatenate([g, pad], axis=0).T
    ck = jnp.concatenate([cs, pad], axis=0).T * (-math.log2(math.e))
    lane = lax.broadcasted_iota(jnp.int32, ck.shape, 1)
    comb = jnp.zeros_like(ck)
    for j, part in enumerate(_split(ck, FOX_SPLIT)):
        moved = part.astype(F32) if j == 0 else pltpu.roll(part.astype(F32), HEADS * j, axis=1)
        comb = jnp.where((lane >= HEADS * j) & (lane < HEADS * (j + 1)), moved, comb)
    kb = _halves_dot(comb.astype(BF16), place_ref[...]) + ones_ref[...]
    kb_ref[...] = kb.astype(kb_ref.dtype)


def _gates(gt, gate_bias_l, bsz, lg):
    rows, t_rows = gt.shape
    s = t_rows // bsz
    nblk = s // lg
    gw = GROUP_W
    triu = jnp.asarray(np.triu(np.ones((lg, lg), np.float32)), BF16)
    bias = jnp.concatenate([gate_bias_l, jnp.zeros((rows - gate_bias_l.shape[0],), F32)])[:, None]
    place = np.zeros((GATE_W, gw), np.float32)
    ones = np.zeros((1, gw), np.float32)
    for h in range(HEADS):
        for j in range(FOX_SPLIT):
            place[HEADS * j + h, _fox_bias_lane(h, j)] = 1.0
            ones[0, _fox_bias_lane(h, FOX_SPLIT + j)] = 1.0
    fixed = lambda i, j: (0, 0)
    return pl.pallas_call(
        _gate_kernel,
        grid=(bsz, nblk),
        in_specs=[pl.BlockSpec((rows, lg), lambda i, j: (0, i * nblk + j)),
                  pl.BlockSpec((rows, 1), fixed), pl.BlockSpec((lg, lg), fixed),
                  pl.BlockSpec((GATE_W, gw), fixed), pl.BlockSpec((1, gw), fixed)],
        out_specs=[pl.BlockSpec((lg, gw), lambda i, j: (i * nblk + j, 0)),
                   pl.BlockSpec((1, 8, lg), lambda i, j: (i, 0, j)),
                   pl.BlockSpec((1, lg, GATE_W), lambda i, j: (i, j, 0))],
        out_shape=[jax.ShapeDtypeStruct((t_rows, gw), BF16), jax.ShapeDtypeStruct((bsz, 8, s), F32),
                   jax.ShapeDtypeStruct((bsz, s, GATE_W), F32)],
        scratch_shapes=[pltpu.VMEM((rows, 1), F32)],
        compiler_params=_cparams("parallel", "arbitrary"),
        name="gate_cumsums",
    )(gt, bias, triu, jnp.asarray(place, BF16), jnp.asarray(ones))


def _fox_kernel(qt_ref, k_ref, kb_ref, vt_ref, crow_ref, gain_ref, o_ref, kaug_ref, *, tq, tk):
    qi = pl.program_id(1)
    gw = GROUP_W
    seq = k_ref.shape[0]
    log2e = math.log2(math.e)
    fill_rows = min(seq, 512)

    @pl.when(qi == 0)
    def _():
        lane_head = lax.broadcasted_iota(jnp.int32, (fill_rows, gw), 1) >> 6

        def fill(i, carry):
            r0 = pl.multiple_of(i * fill_rows, fill_rows)
            kk = k_ref[pl.ds(r0, fill_rows), :]
            kb = kb_ref[pl.ds(r0, fill_rows), :]
            for h in range(HEADS):
                kaug_ref[h, pl.ds(r0, fill_rows), :] = jnp.where(lane_head == h, kk, kb)
            return carry

        lax.fori_loop(0, seq // fill_rows, fill, 0)

    qt = qt_ref[...] * (DH ** -0.5 * log2e)
    row = lax.broadcasted_iota(jnp.int32, (gw, tq), 0)
    head_row = row >> 6
    q_bias = jnp.where((row & (DH - 1)) < FOX_SPLIT, 1.0, 0.0)
    for h in range(HEADS):
        for j, part in enumerate(_split(crow_ref[0, h:h + 1, :] * log2e, FOX_SPLIT)):
            q_bias = jnp.where(row == _fox_bias_lane(h, FOX_SPLIT + j), part.astype(F32), q_bias)
    q_heads = [jnp.where(head_row == h, qt, jnp.where(head_row == (h ^ 1), q_bias, 0.0)).astype(BF16)
               for h in range(HEADS)]
    key_i = lax.broadcasted_iota(jnp.int32, (tk, tq), 0)
    qry_i = lax.broadcasted_iota(jnp.int32, (tk, tq), 1)
    kv_per_q = tq // tk
    ones_rows = jnp.ones((16, tk), BF16)

    def run_tiles(tiles, state):
        state = list(state)

        def scores(j, diag_offset, h):
            r0 = pl.multiple_of(j * tk, tk)
            u = jnp.dot(kaug_ref[h, pl.ds(r0, tk), :], q_heads[h], preferred_element_type=F32)
            if diag_offset is not None:
                u = jnp.where(key_i + diag_offset <= qry_i, u, NEG_BIG)
            m_new = jnp.maximum(state[h][0], jnp.max(u, axis=0, keepdims=True))
            return u, m_new

        def accumulate(j, h, u, m_new):
            r0 = pl.multiple_of(j * tk, tk)
            m_old, l_old, acc_old = state[h]
            alpha = jnp.exp2(m_old - m_new)
            vt1 = jnp.concatenate([vt_ref[h * DH:(h + 1) * DH, pl.ds(r0, tk)], ones_rows], axis=0)
            pv = None
            for k0 in range(0, tk, MXU_TILE):
                p = jnp.exp2(u[k0:k0 + MXU_TILE] - m_new).astype(BF16)
                t = jnp.dot(vt1[:, k0:k0 + MXU_TILE], p, preferred_element_type=F32)
                pv = t if pv is None else pv + t
            l_new = alpha * l_old + pv[DH:DH + 1]
            state[h] = (m_new, l_new, alpha * acc_old + pv[0:DH])

        items = [(j, off, h) for j, off in tiles for h in range(HEADS)]
        ahead = min(FOX_LOOKAHEAD, HEADS - 1)
        queue = [scores(*it) for it in items[:ahead]]
        for n, (j, off, h) in enumerate(items):
            if n + ahead < len(items):
                queue.append(scores(*items[n + ahead]))
            accumulate(j, h, *queue.pop(0))
        return tuple(state)

    init = tuple((jnp.full((1, tq), NEG_BIG, F32), jnp.zeros((1, tq), F32), jnp.zeros((DH, tq), F32))
                 for _ in range(HEADS))
    n_full = qi * kv_per_q
    state = lax.fori_loop(0, n_full // 2, lambda i, s: run_tiles([(2 * i, None), (2 * i + 1, None)], s), init)
    diag = [(n_full + d, d * tk) for d in range(kv_per_q)]
    state = lax.cond(n_full % 2 == 1,
                     lambda s: run_tiles([(n_full - 1, None)] + diag, s),
                     lambda s: run_tiles(diag, s), state)

    outs = []
    for h in range(HEADS):
        _, l_fin, acc_fin = state[h]
        o = acc_fin / l_fin
        ms = jnp.mean(o * o, axis=0, keepdims=True)
        outs.append(o * lax.rsqrt(ms + EPS))
    out = jnp.concatenate(outs, axis=0) * gain_ref[...]
    o_ref[...] = out.T.astype(o_ref.dtype)


def _fox_mixer(fqt, fk, k_bias, fvt, cf_row, gain, bsz, tq, tk):
    gw, t_rows = fqt.shape
    s = t_rows // bsz
    nq = s // tq
    return pl.pallas_call(
        functools.partial(_fox_kernel, tq=tq, tk=tk),
        grid=(bsz, nq),
        in_specs=[pl.BlockSpec((gw, tq), lambda i, qi: (0, i * nq + qi)),
                  pl.BlockSpec((s, gw), lambda i, qi: (i, 0)),
                  pl.BlockSpec((s, gw), lambda i, qi: (i, 0)),
                  pl.BlockSpec((gw, s), lambda i, qi: (0, i)),
                  pl.BlockSpec((1, 8, tq), lambda i, qi: (i, 0, qi)),
                  pl.BlockSpec((gw, 1), lambda i, qi: (0, 0))],
        out_specs=pl.BlockSpec((tq, gw), lambda i, qi: (i * nq + qi, 0)),
        out_shape=jax.ShapeDtypeStruct((t_rows, gw), BF16),
        scratch_shapes=[pltpu.VMEM((HEADS, s, gw), BF16)],
        compiler_params=_cparams("arbitrary", "arbitrary"),
        name="fox_mixer",
    )(fqt, fk, k_bias, fvt, cf_row, gain[:, None])


def _head_lane_max(x):
    rows = x.shape[0]
    parts = [jnp.broadcast_to(jnp.max(x[:, h * DH:(h + 1) * DH], axis=-1, keepdims=True), (rows, DH))
             for h in range(HEADS)]
    return jnp.concatenate(parts, axis=-1)


def _mlstm_kernel(x_ref, gt_ref, cw_ref, gain_ref, tri_ref, eb_ref, ei_ref, mh_ref, o_ref,
                  cbuf, ct_ref, m_ref, *, nsub, lg):
    L = CHUNK
    gw = GROUP_W

    @pl.when(pl.program_id(1) == 0)
    def _():
        cbuf[0:8, :] = jnp.zeros((8, 2 * gw), F32)
        ct_ref[...] = jnp.zeros_like(ct_ref)
        m_ref[...] = jnp.zeros_like(m_ref)

    cbuf[8:8 + lg, :] = x_ref[0, :, 0:2 * gw]
    acc = None
    for j in range(ML_CONV):
        term = cbuf[pl.ds(8 - (ML_CONV - 1) + j, lg), :] * cw_ref[j:j + 1, :]
        acc = term if acc is None else acc + term
    cbuf[0:8, :] = cbuf[lg:lg + 8, :]
    qk = acc * _sigmoid(acc)
    q = qk[:, 0:gw]
    k = qk[:, gw:2 * gw] * (DH ** -0.5)
    v = x_ref[0, :, 2 * gw:3 * gw]
    og = x_ref[0, :, 3 * gw:4 * gw]
    g = gt_ref[0]

    bdp = _block_diag_mask(LANES)
    bdp2 = jnp.concatenate([bdp, bdp], axis=1)
    row = lax.broadcasted_iota(jnp.int32, (L, gw), 0)
    col = lax.broadcasted_iota(jnp.int32, (L, gw), 1) & (DH - 1)
    causal = col <= row
    diag = col == row
    ones = jnp.ones((L, LANES), F32)
    ones_blocks = jnp.where(bdp, 1.0, 0.0).astype(BF16)
    chunks = [slice(c * L, (c + 1) * L) for c in range(nsub)]

    def regroup(parts):
        return jnp.concatenate([t[:, 0:LANES] for t in parts] + [t[:, LANES:2 * LANES] for t in parts], axis=1)

    lsg = _log_sigmoid(g)
    cs = jnp.concatenate([_sel_dot(tri_ref[...], lsg[c]) for c in chunks], axis=0)
    b_exp = _dot_sel(cs, eb_ref[...])
    imb = _dot_sel(g, ei_ref[...]) - b_exp

    m_loc, nd_loc, b_last, m_src, d_ct = [], [], [], [], []
    for c in chunks:
        imb_row = jnp.sum(jnp.where(diag, imb[c], 0.0), axis=0, keepdims=True)
        d_log = jnp.where(causal, b_exp[c] + imb_row, NEG_BIG)
        ml = _head_lane_max(d_log)
        qk_loc = _heads_nt(q[c], k[c], bdp) * jnp.exp(d_log - ml)
        nd_loc.append(regroup([_dot(qp, jnp.concatenate([_pair_tile(vp, bdp), ones_blocks], axis=1))
                               for qp, vp in zip(_pairs(qk_loc), _pairs(v[c]))]))
        m_loc.append(ml)
        bl = b_exp[c][L - 1:L]
        src = bl + imb[c]
        ms = jnp.max(src, axis=0, keepdims=True)
        kw = k[c] * jnp.exp(src - ms)
        d_ct.append([jnp.where(bdp2, _dot_tn(kp, jnp.concatenate([vp, ones], axis=1)), 0.0)
                     for kp, vp in zip(_pairs(kw), _pairs(v[c]))])
        b_last.append(bl)
        m_src.append(ms)

    ct = [ct_ref[p] for p in range(PAIRS)]
    m_prev = m_ref[...]
    hs = []
    for n, c in enumerate(chunks):
        inter = b_exp[c] + m_prev
        m_t = jnp.maximum(inter, m_loc[n])
        w_inter = jnp.exp(inter - m_t)
        w_loc = jnp.exp(m_loc[n] - m_t)
        q_ct = regroup([_dot(qp, ct[p]) for p, qp in enumerate(_pairs(q[c]))])
        nd = (jnp.concatenate([w_inter, w_inter], axis=1) * q_ct
              + jnp.concatenate([w_loc, w_loc], axis=1) * nd_loc[n])
        hs.append(nd[:, 0:gw] / jnp.maximum(jnp.abs(nd[:, gw:2 * gw]), jnp.exp(-m_t)))
        m_new = jnp.maximum(b_last[n] + m_prev, m_src[n])
        decay = _pairs(jnp.exp(b_last[n] + m_prev - m_new))
        w_src = _pairs(jnp.exp(m_src[n] - m_new))
        ct = [ct[p] * jnp.concatenate([decay[p], decay[p]], axis=1)
              + d_ct[n][p] * jnp.concatenate([w_src[p], w_src[p]], axis=1) for p in range(PAIRS)]
        m_prev = m_new
    for p in range(PAIRS):
        ct_ref[p] = ct[p]
    m_ref[...] = m_prev
    hh = jnp.concatenate(hs, axis=0)
    o_ref[0] = (_head_rms(hh, mh_ref[...], gain_ref[...]) * _sigmoid(og)).astype(o_ref.dtype)


def _mlstm_mixer(ml, gt, conv_w, gain, lg):
    b, s, w = ml.shape
    gw = GROUP_W
    assert lg % CHUNK == 0
    tri = jnp.asarray(np.tril(np.ones((CHUNK, CHUNK), np.float32)), BF16)
    eb = jnp.asarray(_gate_expand(2 * HEADS), BF16)
    ei = jnp.asarray(_gate_expand(HEADS), BF16)
    mh = jnp.asarray(_head_block(1.0 / DH), BF16)
    fixed = lambda i, j: (0, 0)
    blk = lambda i, j: (i, j, 0)
    return pl.pallas_call(
        functools.partial(_mlstm_kernel, nsub=lg // CHUNK, lg=lg),
        grid=(b, s // lg),
        in_specs=[pl.BlockSpec((1, lg, w), blk), pl.BlockSpec((1, lg, GATE_W), blk),
                  pl.BlockSpec((ML_CONV, 2 * gw), fixed), pl.BlockSpec((1, gw), fixed),
                  pl.BlockSpec((CHUNK, CHUNK), fixed), pl.BlockSpec((GATE_W, gw), fixed),
                  pl.BlockSpec((GATE_W, gw), fixed), pl.BlockSpec((gw, gw), fixed)],
        out_specs=pl.BlockSpec((1, lg, gw), blk),
        out_shape=jax.ShapeDtypeStruct((b, s, gw), BF16),
        scratch_shapes=[pltpu.VMEM((lg + 8, 2 * gw), F32), pltpu.VMEM((PAIRS, LANES, 2 * LANES), F32),
                        pltpu.VMEM((1, gw), F32)],
        compiler_params=_cparams("parallel", "arbitrary"),
        name="mlstm_mixer",
    )(ml, gt, conv_w, gain[None, :], tri, eb, ei, mh)


def _post_kernel(ya_ref, yb_ref, yc_ref, yd_ref, h_ref, wo_ref, gpost_ref, gpre_ref, wg_ref, wu_ref, wd_ref,
                 gffn_ref, o_ref, *, ff_chunk):
    gw = GROUP_W
    tm = h_ref.shape[0]
    for r0 in range(0, tm, 2 * POST_ROWS):
        _post_rows(ya_ref, yb_ref, yc_ref, yd_ref, h_ref, wo_ref, gpost_ref, gpre_ref, wg_ref, wu_ref, wd_ref,
                   gffn_ref, o_ref, ff_chunk, [slice(r0, r0 + POST_ROWS), slice(r0 + POST_ROWS, r0 + 2 * POST_ROWS)])


def _post_rows(ya_ref, yb_ref, yc_ref, yd_ref, h_ref, wo_ref, gpost_ref, gpre_ref, wg_ref, wu_ref, wd_ref,
               gffn_ref, o_ref, ff_chunk, halves):
    gw = GROUP_W
    h1, a = [], []
    for r in halves:
        mix = None
        for i, y_ref in enumerate((ya_ref, yb_ref, yc_ref, yd_ref)):
            t = jnp.dot(y_ref[r, :], wo_ref[i * gw:(i + 1) * gw, :], preferred_element_type=F32)
            mix = t if mix is None else mix + t
        h1.append(h_ref[r, :] + _rms(mix, gpost_ref[...]))
    for n in range(len(halves)):
        a.append(_rms(h1[n], gpre_ref[...]).astype(BF16))
    d_ff = wg_ref.shape[1]
    ff = [None] * len(halves)
    for c0 in range(0, d_ff, ff_chunk):
        c1 = min(c0 + ff_chunk, d_ff)
        gu = [(jnp.dot(a[n], wg_ref[:, c0:c1], preferred_element_type=F32),
               jnp.dot(a[n], wu_ref[:, c0:c1], preferred_element_type=F32)) for n in range(len(halves))]
        for n, (g, u) in enumerate(gu):
            act = (g * _sigmoid(g) * u).astype(BF16)
            t = jnp.dot(act, wd_ref[c0:c1, :], preferred_element_type=F32)
            ff[n] = t if ff[n] is None else ff[n] + t
    for n, r in enumerate(halves):
        o_ref[r, :] = h1[n] + _rms(ff[n], gffn_ref[...])


def _post(ya, yb, yc, yd, h, w_out, g_post, g_pre, w_gate, w_up, w_down, g_ffn, tm):
    t_rows, d = h.shape
    gw = GROUP_W
    seq = ya.shape[0]
    assert seq % tm == 0
    d_ff = w_gate.shape[1]
    ff_chunk = min(d_ff, -(-d_ff // (2 * MXU_TILE)) * MXU_TILE)
    row = lambda i: (i, 0)
    fixed = lambda i: (0, 0)
    once = pl.Buffered(1)
    wspec = lambda shape: pl.BlockSpec(shape, fixed, pipeline_mode=once)
    gspec = pl.BlockSpec((1, d), fixed)
    return pl.pallas_call(
        functools.partial(_post_kernel, ff_chunk=ff_chunk),
        grid=(t_rows // tm,),
        in_specs=[pl.BlockSpec((tm, gw), _time_major_map(seq // tm))] + [pl.BlockSpec((tm, gw), row)] * 3
        + [pl.BlockSpec((tm, d), row), wspec((d, d)), gspec, gspec,
           wspec((d, d_ff)), wspec((d, d_ff)), wspec((d_ff, d)), gspec],
        out_specs=pl.BlockSpec((tm, d), row),
        out_shape=jax.ShapeDtypeStruct((t_rows, d), F32),
        compiler_params=_cparams("parallel"),
        name="out_proj_ffn",
    )(ya, yb, yc, yd, h, w_out.astype(BF16), g_post[None, :], g_pre[None, :],
      w_gate.astype(BF16), w_up.astype(BF16), w_down.astype(BF16), g_ffn[None, :])


def kernel(x, w_in, gate_bias, s5_lambda_re, s5_lambda_im, s5_b_re, s5_b_im, s5_c_re, s5_c_im, s5_d, s5_log_dt,
           s5_w_glu, hgrn_lb_logits, mlstm_conv_w, mix_gain, w_out, ln_mix_pre, ln_mix_post, ln_ffn_pre,
           ln_ffn_post, w_ffn_gate, w_ffn_up, w_ffn_down):
    bsz, seq, d = x.shape
    depth = w_in.shape[0]
    gw = GROUP_W
    tm = min(512, seq)
    tm_in = min(1024, seq)
    lg = min(512, seq)
    s5_lb = min(128, seq)
    gate_lg = min(1024, seq)
    fox_tq = min(512, seq)
    fox_tk = min(512, seq)

    lb_all = pl.pallas_call(_lb_kernel, out_shape=jax.ShapeDtypeStruct(hgrn_lb_logits.shape, F32),
                            name="hgrn_lower_bounds")(hgrn_lb_logits)

    h = x.reshape(bsz * seq, d)
    for l in range(depth):
        gain = mix_gain[l]
        u5, hg, fk, ml, fqt, fvt, gt = _in_proj(h, ln_mix_pre[l], w_in[l], tm_in, seq)
        lam, wb, cm = _s5_params(s5_lambda_re[l], s5_lambda_im[l], s5_b_re[l], s5_b_im[l],
                                 s5_c_re[l], s5_c_im[l], s5_log_dt[l])
        ya = _s5_mixer(u5, lam, wb, cm, s5_d[l], s5_w_glu[l], gain[0:gw], s5_lb, bsz)
        yb = _hgrn_mixer(hg.reshape(bsz, seq, 4 * gw), lb_all[l], gain[gw:2 * gw], lg)
        k_bias, cf_row, gt3 = _gates(gt, gate_bias[l], bsz, gate_lg)
        yc = _fox_mixer(fqt, fk, k_bias, fvt, cf_row, gain[2 * gw:3 * gw], bsz, fox_tq, fox_tk)
        yd = _mlstm_mixer(ml.reshape(bsz, seq, 4 * gw), gt3, mlstm_conv_w[l], gain[3 * gw:4 * gw], lg)
        h = _post(ya, yb.reshape(bsz * seq, gw), yc, yd.reshape(bsz * seq, gw), h,
                  w_out[l], ln_mix_post[l], ln_ffn_pre[l], w_ffn_gate[l], w_ffn_up[l], w_ffn_down[l],
                  ln_ffn_post[l], tm)
    return h.reshape(bsz, seq, d)
```

```python
import functools
import math

import numpy as np
import jax
import jax.numpy as jnp
from jax import lax
from jax.experimental import pallas as pl
from jax.experimental.pallas import tpu as pltpu

F32 = jnp.float32
BF16 = jnp.bfloat16

EPS = 1e-6
NEG_BIG = -1e30
EXP_CLIP = 60.0

GROUP_W = 256
HEADS = 4
DH = GROUP_W // HEADS
HEAD_SHIFT = DH.bit_length() - 1
S5_G, S5_P, S5_N = 16, 16, 64
ML_CONV = 4
CHUNK = 64
HG_LEVELS = (32, 16, 8, 4, 2, 1)
HG_SMALL_LEVELS = (4, 2, 1)
GATE_W = 128
FOX_LOOKAHEAD = 1
POST_ROWS = 256

VMEM_LIMIT_BYTES = 56 * 1024 * 1024
MXU_TILE = 256
SUBLANES = 8
BF16_ROWS = 16


def _cparams(*sem):
    return pltpu.CompilerParams(dimension_semantics=sem, vmem_limit_bytes=VMEM_LIMIT_BYTES)


def _dot(a, b):
    return jnp.dot(a.astype(BF16), b.astype(BF16), preferred_element_type=F32)


def _dot_nt(a, b):
    return lax.dot_general(a.astype(BF16), b.astype(BF16), (((1,), (1,)), ((), ())),
                           preferred_element_type=F32)


def _dot_tn(a, b):
    return lax.dot_general(a.astype(BF16), b.astype(BF16), (((0,), (0,)), ((), ())),
                           preferred_element_type=F32)


def _split(x, n):
    parts, r = [], x
    for i in range(n):
        p = r.astype(BF16)
        parts.append(p)
        if i + 1 < n:
            r = r - p.astype(F32)
    return parts


def _sel_dot(m01, x, n=3):
    out = None
    for p in _split(x, n):
        t = jnp.dot(m01, p, preferred_element_type=F32)
        out = t if out is None else out + t
    return out


def _dot_sel(x, m01, n=3):
    mm = _halves_dot if (x.shape[0] >= 2 * MXU_TILE and m01.shape[1] <= MXU_TILE) else (
        lambda a, b: jnp.dot(a, b, preferred_element_type=F32))
    out = None
    for p in _split(x, n):
        t = mm(p, m01)
        out = t if out is None else out + t
    return out


def _halves_dot(a, b):
    half = a.shape[0] // 2
    return jnp.concatenate([jnp.dot(a[0:half], b, preferred_element_type=F32),
                            jnp.dot(a[half:], b, preferred_element_type=F32)], axis=0)


def _log_sigmoid(z):
    return jnp.minimum(z, 0.0) - jnp.log(1.0 + jnp.exp(-jnp.abs(z)))


def _sigmoid(z):
    return 1.0 / (1.0 + jnp.exp(-z))


def _rms(x, gain):
    ms = jnp.mean(x * x, axis=-1, keepdims=True)
    return x * lax.rsqrt(ms + EPS) * gain


def _head_rms(o, mh, gain):
    ms = _dot_sel(o * o, mh, 2)
    return o * lax.rsqrt(ms + EPS) * gain


def _block_diag_mask(n):
    r = lax.broadcasted_iota(jnp.int32, (n, n), 0)
    c = lax.broadcasted_iota(jnp.int32, (n, n), 1)
    return (r >> HEAD_SHIFT) == (c >> HEAD_SHIFT)


LANES = 128
PAIRS = GROUP_W // LANES


def _pair_tile(x, bdp):
    return jnp.where(bdp, jnp.concatenate([x, x], axis=0), 0.0).astype(BF16)


def _pairs(x):
    return [x[:, p * LANES:(p + 1) * LANES] for p in range(PAIRS)]


def _heads_nt(a, x, bdp):
    return jnp.concatenate([_dot_nt(ap, _pair_tile(xp, bdp)) for ap, xp in zip(_pairs(a), _pairs(x))], axis=1)


def _heads_nn(a, x, bdp):
    return jnp.concatenate([_dot(ap, _pair_tile(xp, bdp)) for ap, xp in zip(_pairs(a), _pairs(x))], axis=1)


def _hgrn_level_mats():
    L = CHUNK
    t = np.arange(L)[:, None]
    j = np.arange(L)[None, :]
    blocks = [j <= t]
    for m in HG_SMALL_LEVELS:
        ref = (t // (2 * m)) * 2 * m + m - 1
        blocks.append(j <= ref)
    return np.concatenate(blocks, axis=0).astype(np.float32)


def _head_block(value):
    i = np.arange(GROUP_W)
    return np.where((i[:, None] // DH) == (i[None, :] // DH), value, 0.0).astype(np.float32)


def _gate_expand(col0):
    e = np.zeros((GATE_W, GROUP_W), np.float32)
    for h in range(HEADS):
        e[col0 + h, h * DH:(h + 1) * DH] = 1.0
    return e


def _lb_kernel(logit_ref, o_ref):
    x = logit_ref[...]
    depth = x.shape[0]
    m = x[0:1]
    for l in range(1, depth):
        m = jnp.maximum(m, x[l:l + 1])
    e = [jnp.exp(x[l:l + 1] - m) for l in range(depth)]
    tot = e[0]
    for l in range(1, depth):
        tot = tot + e[l]
    p = [el / tot for el in e]
    c = None
    for l in range(depth):
        c = p[l] if c is None else c + p[l]
        o_ref[l:l + 1, :] = jnp.maximum(c - p[0], 0.0)


def _s5_param_kernel(lr_ref, li_ref, ldt_ref, bre_ref, bim_ref, abr_ref, abi_ref, bbr_ref, bbi_ref):
    lr = jnp.minimum(lr_ref[...], -1e-4)
    li = li_ref[...]
    dt = jnp.exp(ldt_ref[...])
    mag = jnp.exp(lr * dt)
    ab_re = mag * jnp.cos(li * dt)
    ab_im = mag * jnp.sin(li * dt)
    den = lr * lr + li * li
    cf_re = ((ab_re - 1.0) * lr + ab_im * li) / den
    cf_im = (ab_im * lr - (ab_re - 1.0) * li) / den
    bre = bre_ref[...]
    bim = bim_ref[...]
    abr_ref[...] = ab_re
    abi_ref[...] = ab_im
    bbr_ref[...] = cf_re * bre - cf_im * bim
    bbi_ref[...] = cf_re * bim + cf_im * bre


def _s5_params(lam_re, lam_im, b_re, b_im, c_re, c_im, log_dt):
    gp = S5_G * S5_P
    rep = lambda a: jnp.repeat(a, S5_P, axis=0)
    ldt = jnp.broadcast_to(rep(log_dt[:, None]), (gp, S5_N))
    bt = lambda a: a.transpose(0, 2, 1).reshape(gp, S5_N)
    shp = jax.ShapeDtypeStruct((gp, S5_N), F32)
    ab_re, ab_im, bb_re, bb_im = pl.pallas_call(
        _s5_param_kernel, out_shape=(shp, shp, shp, shp), name="s5_params",
    )(rep(lam_re), rep(lam_im), ldt, bt(b_re), bt(b_im))
    lam = jnp.stack([ab_re[::S5_P].reshape(-1), ab_im[::S5_P].reshape(-1)])
    own_gp = jnp.asarray(np.eye(S5_G, dtype=np.float32).repeat(S5_P, axis=0))
    own_gn = jnp.asarray(np.eye(S5_G, dtype=np.float32).repeat(S5_N, axis=0))
    wide = lambda bb: (bb[:, None, :] * own_gp[:, :, None]).reshape(gp, S5_G * S5_N)
    wb = jnp.concatenate([wide(bb_re), wide(bb_im)], axis=1).astype(BF16)
    tall = lambda c: (c.transpose(0, 2, 1).reshape(S5_G * S5_N, S5_P)[:, None, :]
                      * own_gn[:, :, None]).reshape(S5_G * S5_N, gp)
    cm = jnp.concatenate([tall(c_re), -tall(c_im)], axis=0).astype(BF16)
    return lam, wb, cm


IN_COLS = (GROUP_W, 4 * GROUP_W, GROUP_W, 4 * GROUP_W)
IN_DTYPES = (F32, F32, BF16, F32)
GATE_ROWS = 16


def _in_proj_kernel(x_ref, g_ref, w_ref, wt_ref, o_u5, o_hg, o_fk, o_ml, o_fqt, o_fvt, o_gt):
    gw = GROUP_W
    a = _rms(x_ref[...], g_ref[...]).astype(BF16)
    outs = (o_u5, o_hg, o_fk, o_ml)
    c0 = 0
    for o_ref, width in zip(outs, IN_COLS):
        o_ref[...] = jnp.dot(a, w_ref[:, c0:c0 + width], preferred_element_type=F32).astype(o_ref.dtype)
        c0 += width
    t = lax.dot_general(wt_ref[...], a, (((1,), (1,)), ((), ())), preferred_element_type=F32)
    o_fqt[...] = t[0:gw]
    o_fvt[...] = t[gw:2 * gw].astype(o_fvt.dtype)
    o_gt[...] = t[2 * gw:2 * gw + GATE_ROWS]


def _time_major_map(tiles_per_seq):
    return lambda i: (i % tiles_per_seq, i // tiles_per_seq)


def _in_proj(h, gain, w_in_l, tm, seq):
    t_rows, d = h.shape
    assert seq % tm == 0 and t_rows % seq == 0
    gw = GROUP_W
    o_fox_f = 8 * gw
    o_ml = o_fox_f + HEADS
    o_ml_i = o_ml + 4 * gw
    o_ml_f = o_ml_i + HEADS
    gates = jnp.concatenate([w_in_l[:, o_fox_f:o_fox_f + HEADS], w_in_l[:, o_ml_i:o_ml_i + HEADS],
                             w_in_l[:, o_ml_f:o_ml_f + HEADS],
                             jnp.zeros((d, GATE_ROWS - 3 * HEADS), w_in_l.dtype)], axis=1)
    w = jnp.concatenate([w_in_l[:, :5 * gw], w_in_l[:, 6 * gw:7 * gw], w_in_l[:, o_ml:o_ml + 4 * gw]],
                        axis=1).astype(BF16)
    wt = jnp.concatenate([w_in_l[:, 5 * gw:6 * gw], w_in_l[:, 7 * gw:8 * gw], gates], axis=1).T.astype(BF16)
    n_tot = sum(IN_COLS)
    row = lambda i: (i, 0)
    colb = lambda i: (0, i)
    fixed = lambda i: (0, 0)
    return pl.pallas_call(
        _in_proj_kernel,
        grid=(t_rows // tm,),
        in_specs=[pl.BlockSpec((tm, d), row), pl.BlockSpec((1, d), fixed),
                  pl.BlockSpec((d, n_tot), fixed, pipeline_mode=pl.Buffered(1)),
                  pl.BlockSpec((2 * gw + GATE_ROWS, d), fixed, pipeline_mode=pl.Buffered(1))],
        out_specs=[pl.BlockSpec((tm, gw), _time_major_map(seq // tm))]
        + [pl.BlockSpec((tm, c), row) for c in IN_COLS[1:]]
        + [pl.BlockSpec((gw, tm), colb), pl.BlockSpec((gw, tm), colb), pl.BlockSpec((GATE_ROWS, tm), colb)],
        out_shape=[jax.ShapeDtypeStruct((seq, (t_rows // seq) * gw), F32)]
        + [jax.ShapeDtypeStruct((t_rows, c), dt) for c, dt in zip(IN_COLS[1:], IN_DTYPES[1:])]
        + [jax.ShapeDtypeStruct((gw, t_rows), F32), jax.ShapeDtypeStruct((gw, t_rows), BF16),
           jax.ShapeDtypeStruct((GATE_ROWS, t_rows), F32)],
        compiler_params=_cparams("parallel"),
        name="in_proj",
    )(h, gain[None, :], w, wt)


def _s5_kernel(u_ref, perm_ref, permt_ref, wb_ref, lam_ref, cm_ref, d_ref, wglu_ref, gain_ref, o_ref,
               *scratch, hb, nb):
    gw = GROUP_W
    ns = S5_G * S5_N
    xs_refs, st_ref = scratch[:-1], scratch[-1]
    nsub = len(xs_refs)

    @pl.when(pl.program_id(0) == 0)
    def _():
        st_ref[...] = jnp.zeros_like(st_ref)

    ar = jnp.broadcast_to(lam_ref[0:1, :], (nb, ns))
    ai = jnp.broadcast_to(lam_ref[1:2, :], (nb, ns))

    def front(k):
        u = jnp.concatenate([u_ref[k * hb:(k + 1) * hb, b * gw:(b + 1) * gw] for b in range(nb)], axis=0)
        u_tb = _halves_dot(perm_ref[...], u.astype(BF16)).astype(BF16)
        xs_refs[k][...] = jnp.dot(u_tb, wb_ref[...], preferred_element_type=F32)
        return u

    def scan(k, xr, xi):
        xs = xs_refs[k]
        for t in range(hb):
            r = slice(t * nb, (t + 1) * nb)
            nr = ar * xr - ai * xi + xs[r, 0:ns]
            ni = ar * xi + ai * xr + xs[r, ns:2 * ns]
            xs[r, 0:ns] = nr
            xs[r, ns:2 * ns] = ni
            xr, xi = nr, ni
        return xr, xi

    def back(k, u):
        cx_tb = _halves_dot(xs_refs[k][...].astype(BF16), cm_ref[...])
        cx = None
        for part in _split(cx_tb, 2):
            t = _halves_dot(permt_ref[...], part)
            cx = t if cx is None else cx + t
        y = cx + d_ref[...] * u
        g = jax.nn.gelu(y)
        y = g * _sigmoid(_halves_dot(g.astype(BF16), wglu_ref[...]))
        out = _rms(y, gain_ref[...]).astype(o_ref.dtype)
        for b in range(nb):
            o_ref[k * hb:(k + 1) * hb, b * gw:(b + 1) * gw] = out[b * hb:(b + 1) * hb]

    xr, xi = st_ref[:, 0:ns], st_ref[:, ns:2 * ns]
    us = {0: front(0)}
    for k in range(nsub):
        if k + 1 < nsub:
            us[k + 1] = front(k + 1)
        xr, xi = scan(k, xr, xi)
        if k + 1 == nsub:
            st_ref[:, 0:ns] = xr
            st_ref[:, ns:2 * ns] = xi
        back(k, us.pop(k))


def _s5_mixer(u2d, lam, wb, cm, d_skip, w_glu, gain, lb, nb, nsub):
    s, w = u2d.shape
    gw = GROUP_W
    assert w == nb * gw
    ns2 = 2 * S5_G * S5_N
    assert lb % nsub == 0 and s % lb == 0
    hb = lb // nsub
    idx = np.arange(hb * nb)
    perm = np.zeros((hb * nb, hb * nb), np.float32)
    perm[idx, (idx % nb) * hb + idx // nb] = 1.0
    fixed = lambda i: (0, 0)
    return pl.pallas_call(
        functools.partial(_s5_kernel, hb=hb, nb=nb),
        grid=(s // lb,),
        in_specs=[pl.BlockSpec((lb, w), lambda i: (i, 0)),
                  pl.BlockSpec(perm.shape, fixed), pl.BlockSpec(perm.shape, fixed),
                  pl.BlockSpec((gw, ns2), fixed), pl.BlockSpec((2, ns2 // 2), fixed),
                  pl.BlockSpec((ns2, gw), fixed), pl.BlockSpec((1, gw), fixed),
                  pl.BlockSpec((gw, gw), fixed), pl.BlockSpec((1, gw), fixed)],
        out_specs=pl.BlockSpec((lb, w), lambda i: (i, 0)),
        out_shape=jax.ShapeDtypeStruct((s, w), BF16),
        scratch_shapes=[pltpu.VMEM((hb * nb, ns2), F32)] * nsub + [pltpu.VMEM((nb, ns2), F32)],
        compiler_params=_cparams("arbitrary"),
        name="s5_mixer",
    )(u2d, jnp.asarray(perm, BF16), jnp.asarray(perm.T, BF16), wb, lam, cm, d_skip[None, :],
      w_glu.astype(BF16), gain[None, :])


def _hgrn_kernel(x_ref, lb_ref, gain_ref, mall_ref, mh_ref, o_ref, st_ref, *, ngroup, nsub):
    L = CHUNK
    gw = GROUP_W
    R = nsub * L

    @pl.when(pl.program_id(1) == 0)
    def _():
        st_ref[...] = jnp.zeros_like(st_ref)

    lb = lb_ref[...]
    gain = gain_ref[...]
    bdp = _block_diag_mask(LANES)
    row = lax.broadcasted_iota(jnp.int32, (R, gw), 0) & (L - 1)
    row_c = lax.broadcasted_iota(jnp.int32, (L, gw), 0)
    col_c = lax.broadcasted_iota(jnp.int32, (L, gw), 1) & (DH - 1)
    chunks = [slice(c * L, (c + 1) * L) for c in range(nsub)]

    def group(gi, carry):
        r0 = pl.multiple_of(gi * R, R)
        q = x_ref[0, pl.ds(r0, R), 0:gw]
        z = x_ref[0, pl.ds(r0, R), gw:2 * gw]
        v = x_ref[0, pl.ds(r0, R), 2 * gw:3 * gw]
        gg = x_ref[0, pl.ds(r0, R), 3 * gw:4 * gw]
        logf = _log_sigmoid(z) + jnp.log(1.0 + lb * jnp.exp(jnp.minimum(-z, EXP_CLIP)))
        kk = (1.0 - lb) * _sigmoid(-z)
        cums = [_sel_dot(mall_ref[...], logf[c]) for c in chunks]
        bs = [cm[0:L] for cm in cums]
        b = jnp.concatenate(bs, axis=0)
        a = [jnp.where(row_c == col_c, _heads_nt(q[c], kk[c], bdp), 0.0) for c in chunks]
        for m in HG_LEVELS:
            upper = (row_c & m) != 0
            sh = int(math.log2(2 * m))
            same = (row_c >> sh) == (col_c >> sh)
            for n, c in enumerate(chunks):
                if m in HG_SMALL_LEVELS:
                    i = 1 + HG_SMALL_LEVELS.index(m)
                    b_ref = cums[n][i * L:(i + 1) * L]
                else:
                    b_ref = jnp.concatenate([jnp.broadcast_to(bs[n][c0 + m - 1:c0 + m], (2 * m, gw))
                                             for c0 in range(0, L, 2 * m)], axis=0)
                w = jnp.where(upper, q[c], kk[c]) * jnp.exp(-jnp.abs(bs[n] - b_ref))
                ql = jnp.where(upper, w, 0.0)
                kl = jnp.where(upper, 0.0, w)
                a[n] = a[n] + jnp.where(same, _heads_nt(ql, kl, bdp), 0.0)
        o_intra = [_heads_nn(a[n], v[c], bdp) for n, c in enumerate(chunks)]
        b_last = [b[(n + 1) * L - 1:(n + 1) * L] for n in range(nsub)]
        kdec = kk * jnp.exp(jnp.concatenate([jnp.broadcast_to(bl, (L, gw)) for bl in b_last], axis=0) - b)
        d_st = [[jnp.where(bdp, _dot_tn(vp, kp), 0.0) for vp, kp in zip(_pairs(v[c]), _pairs(kdec[c]))]
                for c in chunks]
        qe = q * jnp.exp(b)
        st = [st_ref[p] for p in range(PAIRS)]
        outs = []
        for n, c in enumerate(chunks):
            o_inter = jnp.concatenate([_dot_nt(qp, st[p]) for p, qp in enumerate(_pairs(qe[c]))], axis=1)
            outs.append(o_intra[n] + o_inter)
            decay = _pairs(jnp.exp(b_last[n]))
            st = [st[p] * decay[p] + d_st[n][p] for p in range(PAIRS)]
        for p in range(PAIRS):
            st_ref[p] = st[p]
        o = jnp.concatenate(outs, axis=0)
        out = _head_rms(o, mh_ref[...], gain) * (gg * _sigmoid(gg))
        o_ref[0, pl.ds(r0, R), :] = out.astype(o_ref.dtype)
        return carry

    lax.fori_loop(0, ngroup, group, 0)


def _hgrn_mixer(hg, lb, gain, lg, nsub=8):
    b, s, w = hg.shape
    gw = GROUP_W
    assert lg % (CHUNK * nsub) == 0
    mall = jnp.asarray(_hgrn_level_mats(), BF16)
    mh = jnp.asarray(_head_block(1.0 / DH), BF16)
    fixed = lambda i, j: (0, 0)
    return pl.pallas_call(
        functools.partial(_hgrn_kernel, ngroup=lg // (CHUNK * nsub), nsub=nsub),
        grid=(b, s // lg),
        in_specs=[pl.BlockSpec((1, lg, w), lambda i, j: (i, j, 0)),
                  pl.BlockSpec((1, gw), fixed), pl.BlockSpec((1, gw), fixed),
                  pl.BlockSpec(mall.shape, fixed), pl.BlockSpec((gw, gw), fixed)],
        out_specs=pl.BlockSpec((1, lg, gw), lambda i, j: (i, j, 0)),
        out_shape=jax.ShapeDtypeStruct((b, s, gw), BF16),
        scratch_shapes=[pltpu.VMEM((PAIRS, LANES, LANES), F32)],
        compiler_params=_cparams("parallel", "arbitrary"),
        name="hgrn2_mixer",
    )(hg, lb[None, :], gain[None, :], mall, mh)


FOX_SPLIT = 3


def _fox_bias_lane(h, j):
    return (h ^ 1) * DH + j


def _gate_kernel(gt_ref, bias_ref, triu_ref, place_ref, ones_ref, kb_ref, rowo_ref, grow_ref, carry_ref):
    @pl.when(pl.program_id(1) == 0)
    def _():
        carry_ref[...] = jnp.zeros_like(carry_ref)

    g = gt_ref[...] + bias_ref[...]
    cs = _dot_sel(_log_sigmoid(g), triu_ref[...]) + carry_ref[...]
    lg = cs.shape[1]
    carry_ref[...] = cs[:, lg - 1:lg]
    rowo_ref[0] = cs[0:SUBLANES]
    pad = jnp.zeros((GATE_W - GATE_ROWS, lg), F32)
    grow_ref[0] = jnp.concatenate([g, pad], axis=0).T
    ck = jnp.concatenate([cs, pad], axis=0).T * (-math.log2(math.e))
    lane = lax.broadcasted_iota(jnp.int32, ck.shape, 1)
    comb = jnp.zeros_like(ck)
    for j, part in enumerate(_split(ck, FOX_SPLIT)):
        moved = part.astype(F32) if j == 0 else pltpu.roll(part.astype(F32), HEADS * j, axis=1)
        comb = jnp.where((lane >= HEADS * j) & (lane < HEADS * (j + 1)), moved, comb)
    kb = _halves_dot(comb.astype(BF16), place_ref[...]) + ones_ref[...]
    kb_ref[...] = kb.astype(kb_ref.dtype)


def _gates(gt, gate_bias_l, bsz, lg):
    rows, t_rows = gt.shape
    s = t_rows // bsz
    nblk = s // lg
    gw = GROUP_W
    triu = jnp.asarray(np.triu(np.ones((lg, lg), np.float32)), BF16)
    bias = jnp.concatenate([gate_bias_l, jnp.zeros((rows - gate_bias_l.shape[0],), F32)])[:, None]
    place = np.zeros((GATE_W, gw), np.float32)
    ones = np.zeros((1, gw), np.float32)
    for h in range(HEADS):
        for j in range(FOX_SPLIT):
            place[HEADS * j + h, _fox_bias_lane(h, j)] = 1.0
            ones[0, _fox_bias_lane(h, FOX_SPLIT + j)] = 1.0
    fixed = lambda i, j: (0, 0)
    return pl.pallas_call(
        _gate_kernel,
        grid=(bsz, nblk),
        in_specs=[pl.BlockSpec((rows, lg), lambda i, j: (0, i * nblk + j)),
                  pl.BlockSpec((rows, 1), fixed), pl.BlockSpec((lg, lg), fixed),
                  pl.BlockSpec((GATE_W, gw), fixed), pl.BlockSpec((1, gw), fixed)],
        out_specs=[pl.BlockSpec((lg, gw), lambda i, j: (i * nblk + j, 0)),
                   pl.BlockSpec((1, SUBLANES, lg), lambda i, j: (i, 0, j)),
                   pl.BlockSpec((1, lg, GATE_W), lambda i, j: (i, j, 0))],
        out_shape=[jax.ShapeDtypeStruct((t_rows, gw), BF16), jax.ShapeDtypeStruct((bsz, SUBLANES, s), F32),
                   jax.ShapeDtypeStruct((bsz, s, GATE_W), F32)],
        scratch_shapes=[pltpu.VMEM((rows, 1), F32)],
        compiler_params=_cparams("parallel", "arbitrary"),
        name="gate_cumsums",
    )(gt, bias, triu, jnp.asarray(place, BF16), jnp.asarray(ones))


def _fox_kernel(qt_ref, k_ref, kb_ref, vt_ref, crow_ref, gain_ref, o_ref, kaug_ref, *, tq, tk):
    qi = pl.program_id(1)
    gw = GROUP_W
    seq = k_ref.shape[0]
    log2e = math.log2(math.e)
    fill_rows = min(seq, 512)

    @pl.when(qi == 0)
    def _():
        lane_head = lax.broadcasted_iota(jnp.int32, (fill_rows, gw), 1) >> HEAD_SHIFT

        def fill(i, carry):
            r0 = pl.multiple_of(i * fill_rows, fill_rows)
            kk = k_ref[pl.ds(r0, fill_rows), :]
            kb = kb_ref[pl.ds(r0, fill_rows), :]
            for h in range(HEADS):
                kaug_ref[h, pl.ds(r0, fill_rows), :] = jnp.where(lane_head == h, kk, kb)
            return carry

        lax.fori_loop(0, seq // fill_rows, fill, 0)

    qt = qt_ref[...] * (DH ** -0.5 * log2e)
    row = lax.broadcasted_iota(jnp.int32, (gw, tq), 0)
    head_row = row >> HEAD_SHIFT
    q_bias = jnp.where((row & (DH - 1)) < FOX_SPLIT, 1.0, 0.0)
    for h in range(HEADS):
        for j, part in enumerate(_split(crow_ref[0, h:h + 1, :] * log2e, FOX_SPLIT)):
            q_bias = jnp.where(row == _fox_bias_lane(h, FOX_SPLIT + j), part.astype(F32), q_bias)
    q_heads = [jnp.where(head_row == h, qt, jnp.where(head_row == (h ^ 1), q_bias, 0.0)).astype(BF16)
               for h in range(HEADS)]
    key_i = lax.broadcasted_iota(jnp.int32, (tk, tq), 0)
    qry_i = lax.broadcasted_iota(jnp.int32, (tk, tq), 1)
    kv_per_q = tq // tk
    ones_rows = jnp.ones((BF16_ROWS, tk), BF16)

    def run_tiles(tiles, state):
        state = list(state)

        def scores(j, diag_offset, h):
            r0 = pl.multiple_of(j * tk, tk)
            u = jnp.dot(kaug_ref[h, pl.ds(r0, tk), :], q_heads[h], preferred_element_type=F32)
            if diag_offset is not None:
                u = jnp.where(key_i + diag_offset <= qry_i, u, NEG_BIG)
            m_new = jnp.maximum(state[h][0], jnp.max(u, axis=0, keepdims=True))
            return u, m_new

        def accumulate(j, h, u, m_new):
            r0 = pl.multiple_of(j * tk, tk)
            m_old, l_old, acc_old = state[h]
            alpha = jnp.exp2(m_old - m_new)
            vt1 = jnp.concatenate([vt_ref[h * DH:(h + 1) * DH, pl.ds(r0, tk)], ones_rows], axis=0)
            pv = None
            for k0 in range(0, tk, MXU_TILE):
                p = jnp.exp2(u[k0:k0 + MXU_TILE] - m_new).astype(BF16)
                t = jnp.dot(vt1[:, k0:k0 + MXU_TILE], p, preferred_element_type=F32)
                pv = t if pv is None else pv + t
            l_new = alpha * l_old + pv[DH:DH + 1]
            state[h] = (m_new, l_new, alpha * acc_old + pv[0:DH])

        items = [(j, off, h) for j, off in tiles for h in range(HEADS)]
        ahead = min(FOX_LOOKAHEAD, HEADS - 1)
        queue = [scores(*it) for it in items[:ahead]]
        for n, (j, off, h) in enumerate(items):
            if n + ahead < len(items):
                queue.append(scores(*items[n + ahead]))
            accumulate(j, h, *queue.pop(0))
        return tuple(state)

    init = tuple((jnp.full((1, tq), NEG_BIG, F32), jnp.zeros((1, tq), F32), jnp.zeros((DH, tq), F32))
                 for _ in range(HEADS))
    n_full = qi * kv_per_q
    state = lax.fori_loop(0, n_full // 2, lambda i, s: run_tiles([(2 * i, None), (2 * i + 1, None)], s), init)
    diag = [(n_full + d, d * tk) for d in range(kv_per_q)]
    state = lax.cond(n_full % 2 == 1,
                     lambda s: run_tiles([(n_full - 1, None)] + diag, s),
                     lambda s: run_tiles(diag, s), state)

    outs = []
    for h in range(HEADS):
        _, l_fin, acc_fin = state[h]
        o = acc_fin / l_fin
        ms = jnp.mean(o * o, axis=0, keepdims=True)
        outs.append(o * lax.rsqrt(ms + EPS))
    out = jnp.concatenate(outs, axis=0) * gain_ref[...]
    o_ref[...] = out.T.astype(o_ref.dtype)


def _fox_mixer(fqt, fk, k_bias, fvt, cf_row, gain, bsz, tq, tk):
    gw, t_rows = fqt.shape
    s = t_rows // bsz
    nq = s // tq
    return pl.pallas_call(
        functools.partial(_fox_kernel, tq=tq, tk=tk),
        grid=(bsz, nq),
        in_specs=[pl.BlockSpec((gw, tq), lambda i, qi: (0, i * nq + qi)),
                  pl.BlockSpec((s, gw), lambda i, qi: (i, 0)),
                  pl.BlockSpec((s, gw), lambda i, qi: (i, 0)),
                  pl.BlockSpec((gw, s), lambda i, qi: (0, i)),
                  pl.BlockSpec((1, SUBLANES, tq), lambda i, qi: (i, 0, qi)),
                  pl.BlockSpec((gw, 1), lambda i, qi: (0, 0))],
        out_specs=pl.BlockSpec((tq, gw), lambda i, qi: (i * nq + qi, 0)),
        out_shape=jax.ShapeDtypeStruct((t_rows, gw), BF16),
        scratch_shapes=[pltpu.VMEM((HEADS, s, gw), BF16)],
        compiler_params=_cparams("arbitrary", "arbitrary"),
        name="fox_mixer",
    )(fqt, fk, k_bias, fvt, cf_row, gain[:, None])


def _head_lane_max(x):
    rows = x.shape[0]
    parts = [jnp.broadcast_to(jnp.max(x[:, h * DH:(h + 1) * DH], axis=-1, keepdims=True), (rows, DH))
             for h in range(HEADS)]
    return jnp.concatenate(parts, axis=-1)


def _mlstm_kernel(x_ref, gt_ref, cw_ref, gain_ref, tri_ref, eb_ref, ei_ref, mh_ref, o_ref,
                  cbuf, ct_ref, m_ref, *, nsub, lg):
    L = CHUNK
    gw = GROUP_W
    tail = SUBLANES

    @pl.when(pl.program_id(1) == 0)
    def _():
        cbuf[0:tail, :] = jnp.zeros((tail, 2 * gw), F32)
        ct_ref[...] = jnp.zeros_like(ct_ref)
        m_ref[...] = jnp.zeros_like(m_ref)

    cbuf[tail:tail + lg, :] = x_ref[0, :, 0:2 * gw]
    acc = None
    for j in range(ML_CONV):
        term = cbuf[pl.ds(tail - (ML_CONV - 1) + j, lg), :] * cw_ref[j:j + 1, :]
        acc = term if acc is None else acc + term
    cbuf[0:tail, :] = cbuf[lg:lg + tail, :]
    qk = acc * _sigmoid(acc)
    q = qk[:, 0:gw]
    k = qk[:, gw:2 * gw] * (DH ** -0.5)
    v = x_ref[0, :, 2 * gw:3 * gw]
    og = x_ref[0, :, 3 * gw:4 * gw]
    g = gt_ref[0]

    bdp = _block_diag_mask(LANES)
    bdp2 = jnp.concatenate([bdp, bdp], axis=1)
    row = lax.broadcasted_iota(jnp.int32, (L, gw), 0)
    col = lax.broadcasted_iota(jnp.int32, (L, gw), 1) & (DH - 1)
    causal = col <= row
    diag = col == row
    ones = jnp.ones((L, LANES), F32)
    ones_blocks = jnp.where(bdp, 1.0, 0.0).astype(BF16)
    chunks = [slice(c * L, (c + 1) * L) for c in range(nsub)]

    def regroup(parts):
        return jnp.concatenate([t[:, 0:LANES] for t in parts] + [t[:, LANES:2 * LANES] for t in parts], axis=1)

    lsg = _log_sigmoid(g)
    cs = jnp.concatenate([_sel_dot(tri_ref[...], lsg[c]) for c in chunks], axis=0)
    b_exp = _dot_sel(cs, eb_ref[...])
    imb = _dot_sel(g, ei_ref[...]) - b_exp

    m_loc, nd_loc, b_last, m_src, d_ct = [], [], [], [], []
    for c in chunks:
        imb_row = jnp.sum(jnp.where(diag, imb[c], 0.0), axis=0, keepdims=True)
        d_log = jnp.where(causal, b_exp[c] + imb_row, NEG_BIG)
        ml = _head_lane_max(d_log)
        qk_loc = _heads_nt(q[c], k[c], bdp) * jnp.exp(d_log - ml)
        nd_loc.append(regroup([_dot(qp, jnp.concatenate([_pair_tile(vp, bdp), ones_blocks], axis=1))
                               for qp, vp in zip(_pairs(qk_loc), _pairs(v[c]))]))
        m_loc.append(ml)
        bl = b_exp[c][L - 1:L]
        src = bl + imb[c]
        ms = jnp.max(src, axis=0, keepdims=True)
        kw = k[c] * jnp.exp(src - ms)
        d_ct.append([jnp.where(bdp2, _dot_tn(kp, jnp.concatenate([vp, ones], axis=1)), 0.0)
                     for kp, vp in zip(_pairs(kw), _pairs(v[c]))])
        b_last.append(bl)
        m_src.append(ms)

    ct = [ct_ref[p] for p in range(PAIRS)]
    m_prev = m_ref[...]
    hs = []
    for n, c in enumerate(chunks):
        inter = b_exp[c] + m_prev
        m_t = jnp.maximum(inter, m_loc[n])
        w_inter = jnp.exp(inter - m_t)
        w_loc = jnp.exp(m_loc[n] - m_t)
        q_ct = regroup([_dot(qp, ct[p]) for p, qp in enumerate(_pairs(q[c]))])
        nd = (jnp.concatenate([w_inter, w_inter], axis=1) * q_ct
              + jnp.concatenate([w_loc, w_loc], axis=1) * nd_loc[n])
        hs.append(nd[:, 0:gw] / jnp.maximum(jnp.abs(nd[:, gw:2 * gw]), jnp.exp(-m_t)))
        m_new = jnp.maximum(b_last[n] + m_prev, m_src[n])
        decay = _pairs(jnp.exp(b_last[n] + m_prev - m_new))
        w_src = _pairs(jnp.exp(m_src[n] - m_new))
        ct = [ct[p] * jnp.concatenate([decay[p], decay[p]], axis=1)
              + d_ct[n][p] * jnp.concatenate([w_src[p], w_src[p]], axis=1) for p in range(PAIRS)]
        m_prev = m_new
    for p in range(PAIRS):
        ct_ref[p] = ct[p]
    m_ref[...] = m_prev
    hh = jnp.concatenate(hs, axis=0)
    o_ref[0] = (_head_rms(hh, mh_ref[...], gain_ref[...]) * _sigmoid(og)).astype(o_ref.dtype)


def _mlstm_mixer(ml, gt, conv_w, gain, lg):
    b, s, w = ml.shape
    gw = GROUP_W
    assert lg % CHUNK == 0
    tri = jnp.asarray(np.tril(np.ones((CHUNK, CHUNK), np.float32)), BF16)
    eb = jnp.asarray(_gate_expand(2 * HEADS), BF16)
    ei = jnp.asarray(_gate_expand(HEADS), BF16)
    mh = jnp.asarray(_head_block(1.0 / DH), BF16)
    fixed = lambda i, j: (0, 0)
    blk = lambda i, j: (i, j, 0)
    return pl.pallas_call(
        functools.partial(_mlstm_kernel, nsub=lg // CHUNK, lg=lg),
        grid=(b, s // lg),
        in_specs=[pl.BlockSpec((1, lg, w), blk), pl.BlockSpec((1, lg, GATE_W), blk),
                  pl.BlockSpec((ML_CONV, 2 * gw), fixed), pl.BlockSpec((1, gw), fixed),
                  pl.BlockSpec((CHUNK, CHUNK), fixed), pl.BlockSpec((GATE_W, gw), fixed),
                  pl.BlockSpec((GATE_W, gw), fixed), pl.BlockSpec((gw, gw), fixed)],
        out_specs=pl.BlockSpec((1, lg, gw), blk),
        out_shape=jax.ShapeDtypeStruct((b, s, gw), BF16),
        scratch_shapes=[pltpu.VMEM((lg + SUBLANES, 2 * gw), F32), pltpu.VMEM((PAIRS, LANES, 2 * LANES), F32),
                        pltpu.VMEM((1, gw), F32)],
        compiler_params=_cparams("parallel", "arbitrary"),
        name="mlstm_mixer",
    )(ml, gt, conv_w, gain[None, :], tri, eb, ei, mh)


def _post_kernel(ya_ref, yb_ref, yc_ref, yd_ref, h_ref, wo_ref, gpost_ref, gpre_ref, wg_ref, wu_ref, wd_ref,
                 gffn_ref, o_ref, *, ff_chunk):
    gw = GROUP_W
    tm = h_ref.shape[0]
    for r0 in range(0, tm, 2 * POST_ROWS):
        _post_rows(ya_ref, yb_ref, yc_ref, yd_ref, h_ref, wo_ref, gpost_ref, gpre_ref, wg_ref, wu_ref, wd_ref,
                   gffn_ref, o_ref, ff_chunk, [slice(r0, r0 + POST_ROWS), slice(r0 + POST_ROWS, r0 + 2 * POST_ROWS)])


def _post_rows(ya_ref, yb_ref, yc_ref, yd_ref, h_ref, wo_ref, gpost_ref, gpre_ref, wg_ref, wu_ref, wd_ref,
               gffn_ref, o_ref, ff_chunk, halves):
    gw = GROUP_W
    h1, a = [], []
    for r in halves:
        mix = None
        for i, y_ref in enumerate((ya_ref, yb_ref, yc_ref, yd_ref)):
            t = jnp.dot(y_ref[r, :], wo_ref[i * gw:(i + 1) * gw, :], preferred_element_type=F32)
            mix = t if mix is None else mix + t
        h1.append(h_ref[r, :] + _rms(mix, gpost_ref[...]))
    for n in range(len(halves)):
        a.append(_rms(h1[n], gpre_ref[...]).astype(BF16))
    d_ff = wg_ref.shape[1]
    ff = [None] * len(halves)
    for c0 in range(0, d_ff, ff_chunk):
        c1 = min(c0 + ff_chunk, d_ff)
        gu = [(jnp.dot(a[n], wg_ref[:, c0:c1], preferred_element_type=F32),
               jnp.dot(a[n], wu_ref[:, c0:c1], preferred_element_type=F32)) for n in range(len(halves))]
        for n, (g, u) in enumerate(gu):
            act = (g * _sigmoid(g) * u).astype(BF16)
            t = jnp.dot(act, wd_ref[c0:c1, :], preferred_element_type=F32)
            ff[n] = t if ff[n] is None else ff[n] + t
    for n, r in enumerate(halves):
        o_ref[r, :] = h1[n] + _rms(ff[n], gffn_ref[...])


def _post(ya, yb, yc, yd, h, w_out, g_post, g_pre, w_gate, w_up, w_down, g_ffn, tm):
    t_rows, d = h.shape
    gw = GROUP_W
    seq = ya.shape[0]
    assert seq % tm == 0
    d_ff = w_gate.shape[1]
    ff_chunk = min(d_ff, -(-d_ff // (2 * MXU_TILE)) * MXU_TILE)
    row = lambda i: (i, 0)
    fixed = lambda i: (0, 0)
    once = pl.Buffered(1)
    wspec = lambda shape: pl.BlockSpec(shape, fixed, pipeline_mode=once)
    gspec = pl.BlockSpec((1, d), fixed)
    return pl.pallas_call(
        functools.partial(_post_kernel, ff_chunk=ff_chunk),
        grid=(t_rows // tm,),
        in_specs=[pl.BlockSpec((tm, gw), _time_major_map(seq // tm))] + [pl.BlockSpec((tm, gw), row)] * 3
        + [pl.BlockSpec((tm, d), row), wspec((d, d)), gspec, gspec,
           wspec((d, d_ff)), wspec((d, d_ff)), wspec((d_ff, d)), gspec],
        out_specs=pl.BlockSpec((tm, d), row),
        out_shape=jax.ShapeDtypeStruct((t_rows, d), F32),
        compiler_params=_cparams("parallel"),
        name="out_proj_ffn",
    )(ya, yb, yc, yd, h, w_out.astype(BF16), g_post[None, :], g_pre[None, :],
      w_gate.astype(BF16), w_up.astype(BF16), w_down.astype(BF16), g_ffn[None, :])


def kernel(x, w_in, gate_bias, s5_lambda_re, s5_lambda_im, s5_b_re, s5_b_im, s5_c_re, s5_c_im, s5_d, s5_log_dt,
           s5_w_glu, hgrn_lb_logits, mlstm_conv_w, mix_gain, w_out, ln_mix_pre, ln_mix_post, ln_ffn_pre,
           ln_ffn_post, w_ffn_gate, w_ffn_up, w_ffn_down):
    bsz, seq, d = x.shape
    depth = w_in.shape[0]
    gw = GROUP_W
    tm = min(512, seq)
    tm_in = min(1024, seq)
    lg = min(512, seq)
    s5_sub = 64
    s5_nsub = max(1, min(4, seq // s5_sub))
    s5_lb = s5_sub * s5_nsub
    gate_lg = min(1024, seq)
    fox_tq = min(512, seq)
    fox_tk = fox_tq

    lb_all = pl.pallas_call(_lb_kernel, out_shape=jax.ShapeDtypeStruct(hgrn_lb_logits.shape, F32),
                            name="hgrn_lower_bounds")(hgrn_lb_logits)

    h = x.reshape(bsz * seq, d)
    for l in range(depth):
        gain = mix_gain[l]
        u5, hg, fk, ml, fqt, fvt, gt = _in_proj(h, ln_mix_pre[l], w_in[l], tm_in, seq)
        lam, wb, cm = _s5_params(s5_lambda_re[l], s5_lambda_im[l], s5_b_re[l], s5_b_im[l],
                                 s5_c_re[l], s5_c_im[l], s5_log_dt[l])
        ya = _s5_mixer(u5, lam, wb, cm, s5_d[l], s5_w_glu[l], gain[0:gw], s5_lb, bsz, s5_nsub)
        yb = _hgrn_mixer(hg.reshape(bsz, seq, 4 * gw), lb_all[l], gain[gw:2 * gw], lg)
        k_bias, cf_row, gt3 = _gates(gt, gate_bias[l], bsz, gate_lg)
        yc = _fox_mixer(fqt, fk, k_bias, fvt, cf_row, gain[2 * gw:3 * gw], bsz, fox_tq, fox_tk)
        yd = _mlstm_mixer(ml.reshape(bsz, seq, 4 * gw), gt3, mlstm_conv_w[l], gain[3 * gw:4 * gw], lg)
        h = _post(ya, yb.reshape(bsz * seq, gw), yc, yd.reshape(bsz * seq, gw), h,
                  w_out[l], ln_mix_post[l], ln_ffn_pre[l], w_ffn_gate[l], w_ffn_up[l], w_ffn_down[l],
                  ln_ffn_post[l], tm)
    return h.reshape(bsz, seq, d)
```

```python
import functools
import math

import numpy as np
import jax
import jax.numpy as jnp
from jax import lax
from jax.experimental import pallas as pl
from jax.experimental.pallas import tpu as pltpu

F32 = jnp.float32
BF16 = jnp.bfloat16

EPS = 1e-6
NEG_BIG = -1e30
EXP_CLIP = 60.0

GROUP_W = 256
HEADS = 4
DH = GROUP_W // HEADS
HEAD_SHIFT = DH.bit_length() - 1
S5_G, S5_P, S5_N = 16, 16, 64
ML_CONV = 4
CHUNK = 64
HG_LEVELS = (32, 16, 8, 4, 2, 1)
HG_SMALL_LEVELS = (4, 2, 1)
GATE_W = 128
FOX_LOOKAHEAD = 2
POST_ROWS = 256

VMEM_LIMIT_BYTES = 56 * 1024 * 1024
MXU_TILE = 256
SUBLANES = 8
BF16_ROWS = 16


def _cparams(*sem):
    return pltpu.CompilerParams(dimension_semantics=sem, vmem_limit_bytes=VMEM_LIMIT_BYTES)


def _dot(a, b):
    return jnp.dot(a.astype(BF16), b.astype(BF16), preferred_element_type=F32)


def _dot_nt(a, b):
    return lax.dot_general(a.astype(BF16), b.astype(BF16), (((1,), (1,)), ((), ())),
                           preferred_element_type=F32)


def _dot_tn(a, b):
    return lax.dot_general(a.astype(BF16), b.astype(BF16), (((0,), (0,)), ((), ())),
                           preferred_element_type=F32)


def _split(x, n):
    parts, r = [], x
    for i in range(n):
        p = r.astype(BF16)
        parts.append(p)
        if i + 1 < n:
            r = r - p.astype(F32)
    return parts


def _sel_dot(m01, x, n=3):
    out = None
    for p in _split(x, n):
        t = jnp.dot(m01, p, preferred_element_type=F32)
        out = t if out is None else out + t
    return out


def _dot_sel(x, m01, n=3):
    mm = _halves_dot if (x.shape[0] >= 2 * MXU_TILE and m01.shape[1] <= MXU_TILE) else (
        lambda a, b: jnp.dot(a, b, preferred_element_type=F32))
    out = None
    for p in _split(x, n):
        t = mm(p, m01)
        out = t if out is None else out + t
    return out


def _halves_dot(a, b):
    half = a.shape[0] // 2
    return jnp.concatenate([jnp.dot(a[0:half], b, preferred_element_type=F32),
                            jnp.dot(a[half:], b, preferred_element_type=F32)], axis=0)


def _log_sigmoid(z):
    return jnp.minimum(z, 0.0) - jnp.log(1.0 + jnp.exp(-jnp.abs(z)))


def _sigmoid(z):
    return 1.0 / (1.0 + jnp.exp(-z))


def _rms(x, gain):
    ms = jnp.mean(x * x, axis=-1, keepdims=True)
    return x * lax.rsqrt(ms + EPS) * gain


def _head_rms(o, mh, gain):
    ms = _dot_sel(o * o, mh, 2)
    return o * lax.rsqrt(ms + EPS) * gain


def _block_diag_mask(n):
    r = lax.broadcasted_iota(jnp.int32, (n, n), 0)
    c = lax.broadcasted_iota(jnp.int32, (n, n), 1)
    return (r >> HEAD_SHIFT) == (c >> HEAD_SHIFT)


LANES = 128
PAIRS = GROUP_W // LANES


def _pair_tile(x, bdp):
    return jnp.where(bdp, jnp.concatenate([x, x], axis=0), 0.0).astype(BF16)


def _pairs(x):
    return [x[:, p * LANES:(p + 1) * LANES] for p in range(PAIRS)]


def _heads_nt(a, x, bdp):
    return jnp.concatenate([_dot_nt(ap, _pair_tile(xp, bdp)) for ap, xp in zip(_pairs(a), _pairs(x))], axis=1)


def _heads_nn(a, x, bdp):
    return jnp.concatenate([_dot(ap, _pair_tile(xp, bdp)) for ap, xp in zip(_pairs(a), _pairs(x))], axis=1)


def _hgrn_level_mats():
    L = CHUNK
    t = np.arange(L)[:, None]
    j = np.arange(L)[None, :]
    blocks = [j <= t]
    for m in HG_SMALL_LEVELS:
        ref = (t // (2 * m)) * 2 * m + m - 1
        blocks.append(j <= ref)
    return np.concatenate(blocks, axis=0).astype(np.float32)


def _head_block(value):
    i = np.arange(GROUP_W)
    return np.where((i[:, None] // DH) == (i[None, :] // DH), value, 0.0).astype(np.float32)


def _gate_expand(col0):
    e = np.zeros((GATE_W, GROUP_W), np.float32)
    for h in range(HEADS):
        e[col0 + h, h * DH:(h + 1) * DH] = 1.0
    return e


def _lb_kernel(logit_ref, o_ref):
    x = logit_ref[...]
    depth = x.shape[0]
    m = x[0:1]
    for l in range(1, depth):
        m = jnp.maximum(m, x[l:l + 1])
    e = [jnp.exp(x[l:l + 1] - m) for l in range(depth)]
    tot = e[0]
    for l in range(1, depth):
        tot = tot + e[l]
    p = [el / tot for el in e]
    c = None
    for l in range(depth):
        c = p[l] if c is None else c + p[l]
        o_ref[l:l + 1, :] = jnp.maximum(c - p[0], 0.0)


def _s5_param_kernel(lr_ref, li_ref, ldt_ref, bre_ref, bim_ref, abr_ref, abi_ref, bbr_ref, bbi_ref):
    lr = jnp.minimum(lr_ref[...], -1e-4)
    li = li_ref[...]
    dt = jnp.exp(ldt_ref[...])
    mag = jnp.exp(lr * dt)
    ab_re = mag * jnp.cos(li * dt)
    ab_im = mag * jnp.sin(li * dt)
    den = lr * lr + li * li
    cf_re = ((ab_re - 1.0) * lr + ab_im * li) / den
    cf_im = (ab_im * lr - (ab_re - 1.0) * li) / den
    bre = bre_ref[...]
    bim = bim_ref[...]
    abr_ref[...] = ab_re
    abi_ref[...] = ab_im
    bbr_ref[...] = cf_re * bre - cf_im * bim
    bbi_ref[...] = cf_re * bim + cf_im * bre


def _s5_params(lam_re, lam_im, b_re, b_im, c_re, c_im, log_dt):
    gp = S5_G * S5_P
    rep = lambda a: jnp.repeat(a, S5_P, axis=0)
    ldt = jnp.broadcast_to(rep(log_dt[:, None]), (gp, S5_N))
    bt = lambda a: a.transpose(0, 2, 1).reshape(gp, S5_N)
    shp = jax.ShapeDtypeStruct((gp, S5_N), F32)
    ab_re, ab_im, bb_re, bb_im = pl.pallas_call(
        _s5_param_kernel, out_shape=(shp, shp, shp, shp), name="s5_params",
    )(rep(lam_re), rep(lam_im), ldt, bt(b_re), bt(b_im))
    lam = jnp.stack([ab_re[::S5_P].reshape(-1), ab_im[::S5_P].reshape(-1)])
    own_gp = jnp.asarray(np.eye(S5_G, dtype=np.float32).repeat(S5_P, axis=0))
    own_gn = jnp.asarray(np.eye(S5_G, dtype=np.float32).repeat(S5_N, axis=0))
    wide = lambda bb: (bb[:, None, :] * own_gp[:, :, None]).reshape(gp, S5_G * S5_N)
    wb = jnp.concatenate([wide(bb_re), wide(bb_im)], axis=1).astype(BF16)
    tall = lambda c: (c.transpose(0, 2, 1).reshape(S5_G * S5_N, S5_P)[:, None, :]
                      * own_gn[:, :, None]).reshape(S5_G * S5_N, gp)
    cm = jnp.concatenate([tall(c_re), -tall(c_im)], axis=0).astype(BF16)
    return lam, wb, cm


IN_COLS = (GROUP_W, 4 * GROUP_W, GROUP_W, 4 * GROUP_W)
IN_DTYPES = (F32, F32, BF16, F32)
GATE_ROWS = 16


def _in_proj_kernel(x_ref, g_ref, w_ref, wt_ref, o_u5, o_hg, o_fk, o_ml, o_fqt, o_fvt, o_gt):
    gw = GROUP_W
    a = _rms(x_ref[...], g_ref[...]).astype(BF16)
    outs = (o_u5, o_hg, o_fk, o_ml)
    c0 = 0
    for o_ref, width in zip(outs, IN_COLS):
        o_ref[...] = jnp.dot(a, w_ref[:, c0:c0 + width], preferred_element_type=F32).astype(o_ref.dtype)
        c0 += width
    t = lax.dot_general(wt_ref[...], a, (((1,), (1,)), ((), ())), preferred_element_type=F32)
    o_fqt[...] = t[0:gw]
    o_fvt[...] = t[gw:2 * gw].astype(o_fvt.dtype)
    o_gt[...] = t[2 * gw:2 * gw + GATE_ROWS]


def _time_major_map(tiles_per_seq):
    return lambda i: (i % tiles_per_seq, i // tiles_per_seq)


def _in_proj(h, gain, w_in_l, tm, seq):
    t_rows, d = h.shape
    assert seq % tm == 0 and t_rows % seq == 0
    gw = GROUP_W
    o_fox_f = 8 * gw
    o_ml = o_fox_f + HEADS
    o_ml_i = o_ml + 4 * gw
    o_ml_f = o_ml_i + HEADS
    gates = jnp.concatenate([w_in_l[:, o_fox_f:o_fox_f + HEADS], w_in_l[:, o_ml_i:o_ml_i + HEADS],
                             w_in_l[:, o_ml_f:o_ml_f + HEADS],
                             jnp.zeros((d, GATE_ROWS - 3 * HEADS), w_in_l.dtype)], axis=1)
    w = jnp.concatenate([w_in_l[:, :5 * gw], w_in_l[:, 6 * gw:7 * gw], w_in_l[:, o_ml:o_ml + 4 * gw]],
                        axis=1).astype(BF16)
    wt = jnp.concatenate([w_in_l[:, 5 * gw:6 * gw], w_in_l[:, 7 * gw:8 * gw], gates], axis=1).T.astype(BF16)
    n_tot = sum(IN_COLS)
    row = lambda i: (i, 0)
    colb = lambda i: (0, i)
    fixed = lambda i: (0, 0)
    return pl.pallas_call(
        _in_proj_kernel,
        grid=(t_rows // tm,),
        in_specs=[pl.BlockSpec((tm, d), row), pl.BlockSpec((1, d), fixed),
                  pl.BlockSpec((d, n_tot), fixed, pipeline_mode=pl.Buffered(1)),
                  pl.BlockSpec((2 * gw + GATE_ROWS, d), fixed, pipeline_mode=pl.Buffered(1))],
        out_specs=[pl.BlockSpec((tm, gw), _time_major_map(seq // tm))]
        + [pl.BlockSpec((tm, c), row) for c in IN_COLS[1:]]
        + [pl.BlockSpec((gw, tm), colb), pl.BlockSpec((gw, tm), colb),
           pl.BlockSpec((GATE_ROWS, tm), lambda i: (i // (seq // tm), i % (seq // tm)))],
        out_shape=[jax.ShapeDtypeStruct((seq, (t_rows // seq) * gw), F32)]
        + [jax.ShapeDtypeStruct((t_rows, c), dt) for c, dt in zip(IN_COLS[1:], IN_DTYPES[1:])]
        + [jax.ShapeDtypeStruct((gw, t_rows), F32), jax.ShapeDtypeStruct((gw, t_rows), BF16),
           jax.ShapeDtypeStruct(((t_rows // seq) * GATE_ROWS, seq), F32)],
        compiler_params=_cparams("parallel"),
        name="in_proj",
    )(h, gain[None, :], w, wt)


def _s5_kernel(u_ref, perm_ref, permt_ref, wb_ref, lam_ref, cm_ref, d_ref, wglu_ref, gain_ref, o_ref,
               *scratch, hb, nb):
    gw = GROUP_W
    ns = S5_G * S5_N
    xs_refs, st_ref = scratch[:-1], scratch[-1]
    nsub = len(xs_refs)

    @pl.when(pl.program_id(0) == 0)
    def _():
        st_ref[...] = jnp.zeros_like(st_ref)

    ar = jnp.broadcast_to(lam_ref[0:1, :], (nb, ns))
    ai = jnp.broadcast_to(lam_ref[1:2, :], (nb, ns))

    def front(k):
        u = jnp.concatenate([u_ref[k * hb:(k + 1) * hb, b * gw:(b + 1) * gw] for b in range(nb)], axis=0)
        u_tb = _halves_dot(perm_ref[...], u.astype(BF16)).astype(BF16)
        xs_refs[k][...] = jnp.dot(u_tb, wb_ref[...], preferred_element_type=F32)
        return u

    def scan(k, xr, xi):
        xs = xs_refs[k]
        for t in range(hb):
            r = slice(t * nb, (t + 1) * nb)
            nr = ar * xr - ai * xi + xs[r, 0:ns]
            ni = ar * xi + ai * xr + xs[r, ns:2 * ns]
            xs[r, 0:ns] = nr
            xs[r, ns:2 * ns] = ni
            xr, xi = nr, ni
        return xr, xi

    def back(k, u):
        cx_tb = _halves_dot(xs_refs[k][...].astype(BF16), cm_ref[...])
        cx = None
        for part in _split(cx_tb, 2):
            t = _halves_dot(permt_ref[...], part)
            cx = t if cx is None else cx + t
        y = cx + d_ref[...] * u
        g = jax.nn.gelu(y)
        y = g * _sigmoid(_halves_dot(g.astype(BF16), wglu_ref[...]))
        out = _rms(y, gain_ref[...]).astype(o_ref.dtype)
        for b in range(nb):
            o_ref[k * hb:(k + 1) * hb, b * gw:(b + 1) * gw] = out[b * hb:(b + 1) * hb]

    xr, xi = st_ref[:, 0:ns], st_ref[:, ns:2 * ns]
    us = {0: front(0)}
    for k in range(nsub):
        if k + 1 < nsub:
            us[k + 1] = front(k + 1)
        xr, xi = scan(k, xr, xi)
        if k + 1 == nsub:
            st_ref[:, 0:ns] = xr
            st_ref[:, ns:2 * ns] = xi
        back(k, us.pop(k))


def _s5_mixer(u2d, lam, wb, cm, d_skip, w_glu, gain, lb, nb, nsub):
    s, w = u2d.shape
    gw = GROUP_W
    assert w == nb * gw
    ns2 = 2 * S5_G * S5_N
    assert lb % nsub == 0 and s % lb == 0
    hb = lb // nsub
    idx = np.arange(hb * nb)
    perm = np.zeros((hb * nb, hb * nb), np.float32)
    perm[idx, (idx % nb) * hb + idx // nb] = 1.0
    fixed = lambda i: (0, 0)
    return pl.pallas_call(
        functools.partial(_s5_kernel, hb=hb, nb=nb),
        grid=(s // lb,),
        in_specs=[pl.BlockSpec((lb, w), lambda i: (i, 0)),
                  pl.BlockSpec(perm.shape, fixed), pl.BlockSpec(perm.shape, fixed),
                  pl.BlockSpec((gw, ns2), fixed), pl.BlockSpec((2, ns2 // 2), fixed),
                  pl.BlockSpec((ns2, gw), fixed), pl.BlockSpec((1, gw), fixed),
                  pl.BlockSpec((gw, gw), fixed), pl.BlockSpec((1, gw), fixed)],
        out_specs=pl.BlockSpec((lb, w), lambda i: (i, 0)),
        out_shape=jax.ShapeDtypeStruct((s, w), BF16),
        scratch_shapes=[pltpu.VMEM((hb * nb, ns2), F32)] * nsub + [pltpu.VMEM((nb, ns2), F32)],
        compiler_params=_cparams("arbitrary"),
        name="s5_mixer",
    )(u2d, jnp.asarray(perm, BF16), jnp.asarray(perm.T, BF16), wb, lam, cm, d_skip[None, :],
      w_glu.astype(BF16), gain[None, :])


def _hgrn_kernel(x_ref, lb_ref, gain_ref, mall_ref, mh_ref, o_ref, st_ref, *, ngroup, nsub):
    L = CHUNK
    gw = GROUP_W
    R = nsub * L

    @pl.when(pl.program_id(1) == 0)
    def _():
        st_ref[...] = jnp.zeros_like(st_ref)

    lb = lb_ref[...]
    gain = gain_ref[...]
    bdp = _block_diag_mask(LANES)
    row = lax.broadcasted_iota(jnp.int32, (R, gw), 0) & (L - 1)
    row_c = lax.broadcasted_iota(jnp.int32, (L, gw), 0)
    col_c = lax.broadcasted_iota(jnp.int32, (L, gw), 1) & (DH - 1)
    chunks = [slice(c * L, (c + 1) * L) for c in range(nsub)]

    def group(gi, carry):
        r0 = pl.multiple_of(gi * R, R)
        q = x_ref[0, pl.ds(r0, R), 0:gw]
        z = x_ref[0, pl.ds(r0, R), gw:2 * gw]
        v = x_ref[0, pl.ds(r0, R), 2 * gw:3 * gw]
        gg = x_ref[0, pl.ds(r0, R), 3 * gw:4 * gw]
        logf = _log_sigmoid(z) + jnp.log(1.0 + lb * jnp.exp(jnp.minimum(-z, EXP_CLIP)))
        kk = (1.0 - lb) * _sigmoid(-z)
        cums = [_sel_dot(mall_ref[...], logf[c]) for c in chunks]
        bs = [cm[0:L] for cm in cums]
        b = jnp.concatenate(bs, axis=0)
        a = [jnp.where(row_c == col_c, _heads_nt(q[c], kk[c], bdp), 0.0) for c in chunks]
        for m in HG_LEVELS:
            upper = (row_c & m) != 0
            sh = int(math.log2(2 * m))
            same = (row_c >> sh) == (col_c >> sh)
            for n, c in enumerate(chunks):
                if m in HG_SMALL_LEVELS:
                    i = 1 + HG_SMALL_LEVELS.index(m)
                    b_ref = cums[n][i * L:(i + 1) * L]
                else:
                    b_ref = jnp.concatenate([jnp.broadcast_to(bs[n][c0 + m - 1:c0 + m], (2 * m, gw))
                                             for c0 in range(0, L, 2 * m)], axis=0)
                w = jnp.where(upper, q[c], kk[c]) * jnp.exp(-jnp.abs(bs[n] - b_ref))
                ql = jnp.where(upper, w, 0.0)
                kl = jnp.where(upper, 0.0, w)
                a[n] = a[n] + jnp.where(same, _heads_nt(ql, kl, bdp), 0.0)
        o_intra = [_heads_nn(a[n], v[c], bdp) for n, c in enumerate(chunks)]
        b_last = [b[(n + 1) * L - 1:(n + 1) * L] for n in range(nsub)]
        kdec = kk * jnp.exp(jnp.concatenate([jnp.broadcast_to(bl, (L, gw)) for bl in b_last], axis=0) - b)
        d_st = [[jnp.where(bdp, _dot_tn(vp, kp), 0.0) for vp, kp in zip(_pairs(v[c]), _pairs(kdec[c]))]
                for c in chunks]
        qe = q * jnp.exp(b)
        st = [st_ref[p] for p in range(PAIRS)]
        outs = []
        for n, c in enumerate(chunks):
            o_inter = jnp.concatenate([_dot_nt(qp, st[p]) for p, qp in enumerate(_pairs(qe[c]))], axis=1)
            outs.append(o_intra[n] + o_inter)
            decay = _pairs(jnp.exp(b_last[n]))
            st = [st[p] * decay[p] + d_st[n][p] for p in range(PAIRS)]
        for p in range(PAIRS):
            st_ref[p] = st[p]
        o = jnp.concatenate(outs, axis=0)
        out = _head_rms(o, mh_ref[...], gain) * (gg * _sigmoid(gg))
        o_ref[0, pl.ds(r0, R), :] = out.astype(o_ref.dtype)
        return carry

    lax.fori_loop(0, ngroup, group, 0)


def _hgrn_mixer(hg, lb, gain, lg, nsub=8):
    b, s, w = hg.shape
    gw = GROUP_W
    assert lg % (CHUNK * nsub) == 0
    mall = jnp.asarray(_hgrn_level_mats(), BF16)
    mh = jnp.asarray(_head_block(1.0 / DH), BF16)
    fixed = lambda i, j: (0, 0)
    return pl.pallas_call(
        functools.partial(_hgrn_kernel, ngroup=lg // (CHUNK * nsub), nsub=nsub),
        grid=(b, s // lg),
        in_specs=[pl.BlockSpec((1, lg, w), lambda i, j: (i, j, 0)),
                  pl.BlockSpec((1, gw), fixed), pl.BlockSpec((1, gw), fixed),
                  pl.BlockSpec(mall.shape, fixed), pl.BlockSpec((gw, gw), fixed)],
        out_specs=pl.BlockSpec((1, lg, gw), lambda i, j: (i, j, 0)),
        out_shape=jax.ShapeDtypeStruct((b, s, gw), BF16),
        scratch_shapes=[pltpu.VMEM((PAIRS, LANES, LANES), F32)],
        compiler_params=_cparams("parallel", "arbitrary"),
        name="hgrn2_mixer",
    )(hg, lb[None, :], gain[None, :], mall, mh)


FOX_SPLIT = 3


def _fox_bias_lane(h, j):
    return (h ^ 1) * DH + j


def _gate_kernel(gt_ref, bias_ref, triu_ref, place_ref, ones_ref, eye_ref, kb_ref, rowo_ref, grow_ref, carry_ref,
                 *, nb):
    rows = GATE_ROWS

    @pl.when(pl.program_id(0) == 0)
    def _():
        carry_ref[...] = jnp.zeros_like(carry_ref)

    g = gt_ref[...] + bias_ref[...]
    cs = _dot_sel(_log_sigmoid(g), triu_ref[...]) + carry_ref[...]
    lg = cs.shape[1]
    carry_ref[...] = cs[:, lg - 1:lg]
    ck = cs * (-math.log2(math.e))
    for b in range(nb):
        r = slice(b * rows, (b + 1) * rows)
        rowo_ref[b] = cs[b * rows:b * rows + SUBLANES]
        grow_ref[b] = _dot_tn(jnp.concatenate(_split(g[r], FOX_SPLIT), axis=0), eye_ref[...])
        kb = _dot_tn(jnp.concatenate(_split(ck[r], FOX_SPLIT), axis=0), place_ref[...]) + ones_ref[...]
        kb_ref[b] = kb.astype(kb_ref.dtype)


def _gates(gt, gate_bias_l, bsz, lg):
    rows = GATE_ROWS
    s = gt.shape[1]
    assert gt.shape[0] == bsz * rows and s % lg == 0
    nblk = s // lg
    gw = GROUP_W
    triu = jnp.asarray(np.triu(np.ones((lg, lg), np.float32)), BF16)
    bias = jnp.tile(jnp.concatenate([gate_bias_l, jnp.zeros((rows - gate_bias_l.shape[0],), F32)]), bsz)[:, None]
    place = np.zeros((FOX_SPLIT * rows, gw), np.float32)
    eye = np.zeros((FOX_SPLIT * rows, GATE_W), np.float32)
    ones = np.zeros((1, gw), np.float32)
    for j in range(FOX_SPLIT):
        for h in range(HEADS):
            place[j * rows + h, _fox_bias_lane(h, j)] = 1.0
            ones[0, _fox_bias_lane(h, FOX_SPLIT + j)] = 1.0
        for f in range(rows):
            eye[j * rows + f, f] = 1.0
    fixed = lambda j: (0, 0)
    k_bias, cf_row, g_rows = pl.pallas_call(
        functools.partial(_gate_kernel, nb=bsz),
        grid=(nblk,),
        in_specs=[pl.BlockSpec((bsz * rows, lg), lambda j: (0, j)),
                  pl.BlockSpec((bsz * rows, 1), fixed), pl.BlockSpec((lg, lg), fixed),
                  pl.BlockSpec(place.shape, fixed), pl.BlockSpec((1, gw), fixed), pl.BlockSpec(eye.shape, fixed)],
        out_specs=[pl.BlockSpec((bsz, lg, gw), lambda j: (0, j, 0)),
                   pl.BlockSpec((bsz, SUBLANES, lg), lambda j: (0, 0, j)),
                   pl.BlockSpec((bsz, lg, GATE_W), lambda j: (0, j, 0))],
        out_shape=[jax.ShapeDtypeStruct((bsz, s, gw), BF16), jax.ShapeDtypeStruct((bsz, SUBLANES, s), F32),
                   jax.ShapeDtypeStruct((bsz, s, GATE_W), F32)],
        scratch_shapes=[pltpu.VMEM((bsz * rows, 1), F32)],
        compiler_params=_cparams("arbitrary"),
        name="gate_cumsums",
    )(gt, bias, triu, jnp.asarray(place, BF16), jnp.asarray(ones), jnp.asarray(eye, BF16))
    return k_bias.reshape(bsz * s, gw), cf_row, g_rows


def _fox_kernel(qt_ref, k_ref, kb_ref, vt_ref, crow_ref, gain_ref, o_ref, kaug_ref, *, tq, tk):
    qi = pl.program_id(1)
    gw = GROUP_W
    seq = k_ref.shape[0]
    log2e = math.log2(math.e)
    fill_rows = min(seq, 512)

    @pl.when(qi == 0)
    def _():
        lane_head = lax.broadcasted_iota(jnp.int32, (fill_rows, gw), 1) >> HEAD_SHIFT

        def fill(i, carry):
            r0 = pl.multiple_of(i * fill_rows, fill_rows)
            kk = k_ref[pl.ds(r0, fill_rows), :]
            kb = kb_ref[pl.ds(r0, fill_rows), :]
            for h in range(HEADS):
                kaug_ref[h, pl.ds(r0, fill_rows), :] = jnp.where(lane_head == h, kk, kb)
            return carry

        lax.fori_loop(0, seq // fill_rows, fill, 0)

    qt = qt_ref[...] * (DH ** -0.5 * log2e)
    row = lax.broadcasted_iota(jnp.int32, (gw, tq), 0)
    head_row = row >> HEAD_SHIFT
    q_bias = jnp.where((row & (DH - 1)) < FOX_SPLIT, 1.0, 0.0)
    for h in range(HEADS):
        for j, part in enumerate(_split(crow_ref[0, h:h + 1, :] * log2e, FOX_SPLIT)):
            q_bias = jnp.where(row == _fox_bias_lane(h, FOX_SPLIT + j), part.astype(F32), q_bias)
    q_heads = [jnp.where(head_row == h, qt, jnp.where(head_row == (h ^ 1), q_bias, 0.0)).astype(BF16)
               for h in range(HEADS)]
    key_i = lax.broadcasted_iota(jnp.int32, (tk, tq), 0)
    qry_i = lax.broadcasted_iota(jnp.int32, (tk, tq), 1)
    kv_per_q = tq // tk
    ones_rows = jnp.ones((BF16_ROWS, tk), BF16)

    def run_tiles(tiles, state):
        state = list(state)

        def scores(j, diag_offset, h):
            r0 = pl.multiple_of(j * tk, tk)
            u = jnp.dot(kaug_ref[h, pl.ds(r0, tk), :], q_heads[h], preferred_element_type=F32)
            if diag_offset is not None:
                u = jnp.where(key_i + diag_offset <= qry_i, u, NEG_BIG)
            m_new = jnp.maximum(state[h][0], jnp.max(u, axis=0, keepdims=True))
            return u, m_new

        def accumulate(j, h, u, m_new):
            r0 = pl.multiple_of(j * tk, tk)
            m_old, l_old, acc_old = state[h]
            alpha = jnp.exp2(m_old - m_new)
            vt1 = jnp.concatenate([vt_ref[h * DH:(h + 1) * DH, pl.ds(r0, tk)], ones_rows], axis=0)
            pv = None
            for k0 in range(0, tk, MXU_TILE):
                p = jnp.exp2(u[k0:k0 + MXU_TILE] - m_new).astype(BF16)
                t = jnp.dot(vt1[:, k0:k0 + MXU_TILE], p, preferred_element_type=F32)
                pv = t if pv is None else pv + t
            l_new = alpha * l_old + pv[DH:DH + 1]
            state[h] = (m_new, l_new, alpha * acc_old + pv[0:DH])

        items = [(j, off, h) for j, off in tiles for h in range(HEADS)]
        ahead = min(FOX_LOOKAHEAD, HEADS - 1)
        queue = [scores(*it) for it in items[:ahead]]
        for n, (j, off, h) in enumerate(items):
            if n + ahead < len(items):
                queue.append(scores(*items[n + ahead]))
            accumulate(j, h, *queue.pop(0))
        return tuple(state)

    init = tuple((jnp.full((1, tq), NEG_BIG, F32), jnp.zeros((1, tq), F32), jnp.zeros((DH, tq), F32))
                 for _ in range(HEADS))
    n_full = qi * kv_per_q
    state = lax.fori_loop(0, n_full // 2, lambda i, s: run_tiles([(2 * i, None), (2 * i + 1, None)], s), init)
    diag = [(n_full + d, d * tk) for d in range(kv_per_q)]
    state = lax.cond(n_full % 2 == 1,
                     lambda s: run_tiles([(n_full - 1, None)] + diag, s),
                     lambda s: run_tiles(diag, s), state)

    outs = []
    for h in range(HEADS):
        _, l_fin, acc_fin = state[h]
        o = acc_fin / l_fin
        ms = jnp.mean(o * o, axis=0, keepdims=True)
        outs.append(o * lax.rsqrt(ms + EPS))
    out = jnp.concatenate(outs, axis=0) * gain_ref[...]
    o_ref[...] = out.T.astype(o_ref.dtype)


def _fox_mixer(fqt, fk, k_bias, fvt, cf_row, gain, bsz, tq, tk):
    gw, t_rows = fqt.shape
    s = t_rows // bsz
    nq = s // tq
    return pl.pallas_call(
        functools.partial(_fox_kernel, tq=tq, tk=tk),
        grid=(bsz, nq),
        in_specs=[pl.BlockSpec((gw, tq), lambda i, qi: (0, i * nq + qi)),
                  pl.BlockSpec((s, gw), lambda i, qi: (i, 0)),
                  pl.BlockSpec((s, gw), lambda i, qi: (i, 0)),
                  pl.BlockSpec((gw, s), lambda i, qi: (0, i)),
                  pl.BlockSpec((1, SUBLANES, tq), lambda i, qi: (i, 0, qi)),
                  pl.BlockSpec((gw, 1), lambda i, qi: (0, 0))],
        out_specs=pl.BlockSpec((tq, gw), lambda i, qi: (i * nq + qi, 0)),
        out_shape=jax.ShapeDtypeStruct((t_rows, gw), BF16),
        scratch_shapes=[pltpu.VMEM((HEADS, s, gw), BF16)],
        compiler_params=_cparams("arbitrary", "arbitrary"),
        name="fox_mixer",
    )(fqt, fk, k_bias, fvt, cf_row, gain[:, None])


def _head_lane_max(x):
    rows = x.shape[0]
    parts = [jnp.broadcast_to(jnp.max(x[:, h * DH:(h + 1) * DH], axis=-1, keepdims=True), (rows, DH))
             for h in range(HEADS)]
    return jnp.concatenate(parts, axis=-1)


def _mlstm_kernel(x_ref, gt_ref, cw_ref, gain_ref, tri_ref, eb_ref, ei_ref, mh_ref, o_ref,
                  cbuf, ct_ref, m_ref, *, nsub, lg):
    L = CHUNK
    gw = GROUP_W
    tail = SUBLANES

    @pl.when(pl.program_id(1) == 0)
    def _():
        cbuf[0:tail, :] = jnp.zeros((tail, 2 * gw), F32)
        ct_ref[...] = jnp.zeros_like(ct_ref)
        m_ref[...] = jnp.zeros_like(m_ref)

    cbuf[tail:tail + lg, :] = x_ref[0, :, 0:2 * gw]
    acc = None
    for j in range(ML_CONV):
        term = cbuf[pl.ds(tail - (ML_CONV - 1) + j, lg), :] * cw_ref[j:j + 1, :]
        acc = term if acc is None else acc + term
    cbuf[0:tail, :] = cbuf[lg:lg + tail, :]
    qk = acc * _sigmoid(acc)
    q = qk[:, 0:gw]
    k = qk[:, gw:2 * gw] * (DH ** -0.5)
    v = x_ref[0, :, 2 * gw:3 * gw]
    og = x_ref[0, :, 3 * gw:4 * gw]
    g = gt_ref[0]

    bdp = _block_diag_mask(LANES)
    bdp2 = jnp.concatenate([bdp, bdp], axis=1)
    row = lax.broadcasted_iota(jnp.int32, (L, gw), 0)
    col = lax.broadcasted_iota(jnp.int32, (L, gw), 1) & (DH - 1)
    causal = col <= row
    diag = col == row
    ones = jnp.ones((L, LANES), F32)
    ones_blocks = jnp.where(bdp, 1.0, 0.0).astype(BF16)
    chunks = [slice(c * L, (c + 1) * L) for c in range(nsub)]

    def regroup(parts):
        return jnp.concatenate([t[:, 0:LANES] for t in parts] + [t[:, LANES:2 * LANES] for t in parts], axis=1)

    lsg = _log_sigmoid(g)
    cs = jnp.concatenate([_sel_dot(tri_ref[...], lsg[c]) for c in chunks], axis=0)
    b_exp = _dot_sel(cs, eb_ref[...])
    imb = _dot_sel(g, ei_ref[...]) - b_exp

    m_loc, nd_loc, b_last, m_src, d_ct = [], [], [], [], []
    for c in chunks:
        imb_row = jnp.sum(jnp.where(diag, imb[c], 0.0), axis=0, keepdims=True)
        d_log = jnp.where(causal, b_exp[c] + imb_row, NEG_BIG)
        ml = _head_lane_max(d_log)
        qk_loc = _heads_nt(q[c], k[c], bdp) * jnp.exp(d_log - ml)
        nd_loc.append(regroup([_dot(qp, jnp.concatenate([_pair_tile(vp, bdp), ones_blocks], axis=1))
                               for qp, vp in zip(_pairs(qk_loc), _pairs(v[c]))]))
        m_loc.append(ml)
        bl = b_exp[c][L - 1:L]
        src = bl + imb[c]
        ms = jnp.max(src, axis=0, keepdims=True)
        kw = k[c] * jnp.exp(src - ms)
        d_ct.append([jnp.where(bdp2, _dot_tn(kp, jnp.concatenate([vp, ones], axis=1)), 0.0)
                     for kp, vp in zip(_pairs(kw), _pairs(v[c]))])
        b_last.append(bl)
        m_src.append(ms)

    ct = [ct_ref[p] for p in range(PAIRS)]
    m_prev = m_ref[...]
    hs = []
    for n, c in enumerate(chunks):
        inter = b_exp[c] + m_prev
        m_t = jnp.maximum(inter, m_loc[n])
        w_inter = jnp.exp(inter - m_t)
        w_loc = jnp.exp(m_loc[n] - m_t)
        q_ct = regroup([_dot(qp, ct[p]) for p, qp in enumerate(_pairs(q[c]))])
        nd = (jnp.concatenate([w_inter, w_inter], axis=1) * q_ct
              + jnp.concatenate([w_loc, w_loc], axis=1) * nd_loc[n])
        hs.append(nd[:, 0:gw] / jnp.maximum(jnp.abs(nd[:, gw:2 * gw]), jnp.exp(-m_t)))
        m_new = jnp.maximum(b_last[n] + m_prev, m_src[n])
        decay = _pairs(jnp.exp(b_last[n] + m_prev - m_new))
        w_src = _pairs(jnp.exp(m_src[n] - m_new))
        ct = [ct[p] * jnp.concatenate([decay[p], decay[p]], axis=1)
              + d_ct[n][p] * jnp.concatenate([w_src[p], w_src[p]], axis=1) for p in range(PAIRS)]
        m_prev = m_new
    for p in range(PAIRS):
        ct_ref[p] = ct[p]
    m_ref[...] = m_prev
    hh = jnp.concatenate(hs, axis=0)
    o_ref[0] = (_head_rms(hh, mh_ref[...], gain_ref[...]) * _sigmoid(og)).astype(o_ref.dtype)


def _mlstm_mixer(ml, gt, conv_w, gain, lg):
    b, s, w = ml.shape
    gw = GROUP_W
    assert lg % CHUNK == 0
    tri = jnp.asarray(np.tril(np.ones((CHUNK, CHUNK), np.float32)), BF16)
    eb = jnp.asarray(_gate_expand(2 * HEADS), BF16)
    ei = jnp.asarray(_gate_expand(HEADS), BF16)
    mh = jnp.asarray(_head_block(1.0 / DH), BF16)
    fixed = lambda i, j: (0, 0)
    blk = lambda i, j: (i, j, 0)
    return pl.pallas_call(
        functools.partial(_mlstm_kernel, nsub=lg // CHUNK, lg=lg),
        grid=(b, s // lg),
        in_specs=[pl.BlockSpec((1, lg, w), blk), pl.BlockSpec((1, lg, GATE_W), blk),
                  pl.BlockSpec((ML_CONV, 2 * gw), fixed), pl.BlockSpec((1, gw), fixed),
                  pl.BlockSpec((CHUNK, CHUNK), fixed), pl.BlockSpec((GATE_W, gw), fixed),
                  pl.BlockSpec((GATE_W, gw), fixed), pl.BlockSpec((gw, gw), fixed)],
        out_specs=pl.BlockSpec((1, lg, gw), blk),
        out_shape=jax.ShapeDtypeStruct((b, s, gw), BF16),
        scratch_shapes=[pltpu.VMEM((lg + SUBLANES, 2 * gw), F32), pltpu.VMEM((PAIRS, LANES, 2 * LANES), F32),
                        pltpu.VMEM((1, gw), F32)],
        compiler_params=_cparams("parallel", "arbitrary"),
        name="mlstm_mixer",
    )(ml, gt, conv_w, gain[None, :], tri, eb, ei, mh)


def _post_kernel(ya_ref, yb_ref, yc_ref, yd_ref, h_ref, wo_ref, gpost_ref, gpre_ref, wg_ref, wu_ref, wd_ref,
                 gffn_ref, o_ref, *, ff_chunk):
    gw = GROUP_W
    tm = h_ref.shape[0]
    for r0 in range(0, tm, 2 * POST_ROWS):
        _post_rows(ya_ref, yb_ref, yc_ref, yd_ref, h_ref, wo_ref, gpost_ref, gpre_ref, wg_ref, wu_ref, wd_ref,
                   gffn_ref, o_ref, ff_chunk, [slice(r0, r0 + POST_ROWS), slice(r0 + POST_ROWS, r0 + 2 * POST_ROWS)])


def _post_rows(ya_ref, yb_ref, yc_ref, yd_ref, h_ref, wo_ref, gpost_ref, gpre_ref, wg_ref, wu_ref, wd_ref,
               gffn_ref, o_ref, ff_chunk, halves):
    gw = GROUP_W
    h1, a = [], []
    for r in halves:
        mix = None
        for i, y_ref in enumerate((ya_ref, yb_ref, yc_ref, yd_ref)):
            t = jnp.dot(y_ref[r, :], wo_ref[i * gw:(i + 1) * gw, :], preferred_element_type=F32)
            mix = t if mix is None else mix + t
        h1.append(h_ref[r, :] + _rms(mix, gpost_ref[...]))
    for n in range(len(halves)):
        a.append(_rms(h1[n], gpre_ref[...]).astype(BF16))
    d_ff = wg_ref.shape[1]
    ff = [None] * len(halves)
    for c0 in range(0, d_ff, ff_chunk):
        c1 = min(c0 + ff_chunk, d_ff)
        gu = [(jnp.dot(a[n], wg_ref[:, c0:c1], preferred_element_type=F32),
               jnp.dot(a[n], wu_ref[:, c0:c1], preferred_element_type=F32)) for n in range(len(halves))]
        for n, (g, u) in enumerate(gu):
            act = (g * _sigmoid(g) * u).astype(BF16)
            t = jnp.dot(act, wd_ref[c0:c1, :], preferred_element_type=F32)
            ff[n] = t if ff[n] is None else ff[n] + t
    for n, r in enumerate(halves):
        o_ref[r, :] = h1[n] + _rms(ff[n], gffn_ref[...])


def _post(ya, yb, yc, yd, h, w_out, g_post, g_pre, w_gate, w_up, w_down, g_ffn, tm):
    t_rows, d = h.shape
    gw = GROUP_W
    seq = ya.shape[0]
    assert seq % tm == 0
    d_ff = w_gate.shape[1]
    ff_chunk = min(d_ff, -(-d_ff // (2 * MXU_TILE)) * MXU_TILE)
    row = lambda i: (i, 0)
    fixed = lambda i: (0, 0)
    once = pl.Buffered(1)
    wspec = lambda shape: pl.BlockSpec(shape, fixed, pipeline_mode=once)
    gspec = pl.BlockSpec((1, d), fixed)
    return pl.pallas_call(
        functools.partial(_post_kernel, ff_chunk=ff_chunk),
        grid=(t_rows // tm,),
        in_specs=[pl.BlockSpec((tm, gw), _time_major_map(seq // tm))] + [pl.BlockSpec((tm, gw), row)] * 3
        + [pl.BlockSpec((tm, d), row), wspec((d, d)), gspec, gspec,
           wspec((d, d_ff)), wspec((d, d_ff)), wspec((d_ff, d)), gspec],
        out_specs=pl.BlockSpec((tm, d), row),
        out_shape=jax.ShapeDtypeStruct((t_rows, d), F32),
        compiler_params=_cparams("parallel"),
        name="out_proj_ffn",
    )(ya, yb, yc, yd, h, w_out.astype(BF16), g_post[None, :], g_pre[None, :],
      w_gate.astype(BF16), w_up.astype(BF16), w_down.astype(BF16), g_ffn[None, :])


def kernel(x, w_in, gate_bias, s5_lambda_re, s5_lambda_im, s5_b_re, s5_b_im, s5_c_re, s5_c_im, s5_d, s5_log_dt,
           s5_w_glu, hgrn_lb_logits, mlstm_conv_w, mix_gain, w_out, ln_mix_pre, ln_mix_post, ln_ffn_pre,
           ln_ffn_post, w_ffn_gate, w_ffn_up, w_ffn_down):
    bsz, seq, d = x.shape
    depth = w_in.shape[0]
    gw = GROUP_W
    tm = min(512, seq)
    tm_in = min(1024, seq)
    lg = min(512, seq)
    s5_sub = 64
    s5_nsub = max(1, min(4, seq // s5_sub))
    s5_lb = s5_sub * s5_nsub
    gate_lg = min(1024, seq)
    fox_tq = min(512, seq)
    fox_tk = fox_tq

    lb_all = pl.pallas_call(_lb_kernel, out_shape=jax.ShapeDtypeStruct(hgrn_lb_logits.shape, F32),
                            name="hgrn_lower_bounds")(hgrn_lb_logits)

    h = x.reshape(bsz * seq, d)
    for l in range(depth):
        gain = mix_gain[l]
        u5, hg, fk, ml, fqt, fvt, gt = _in_proj(h, ln_mix_pre[l], w_in[l], tm_in, seq)
        lam, wb, cm = _s5_params(s5_lambda_re[l], s5_lambda_im[l], s5_b_re[l], s5_b_im[l],
                                 s5_c_re[l], s5_c_im[l], s5_log_dt[l])
        ya = _s5_mixer(u5, lam, wb, cm, s5_d[l], s5_w_glu[l], gain[0:gw], s5_lb, bsz, s5_nsub)
        yb = _hgrn_mixer(hg.reshape(bsz, seq, 4 * gw), lb_all[l], gain[gw:2 * gw], lg)
        k_bias, cf_row, gt3 = _gates(gt, gate_bias[l], bsz, gate_lg)
        yc = _fox_mixer(fqt, fk, k_bias, fvt, cf_row, gain[2 * gw:3 * gw], bsz, fox_tq, fox_tk)
        yd = _mlstm_mixer(ml.reshape(bsz, seq, 4 * gw), gt3, mlstm_conv_w[l], gain[3 * gw:4 * gw], lg)
        h = _post(ya, yb.reshape(bsz * seq, gw), yc, yd.reshape(bsz * seq, gw), h,
                  w_out[l], ln_mix_post[l], ln_ffn_pre[l], w_ffn_gate[l], w_ffn_up[l], w_ffn_down[l],
                  ln_ffn_post[l], tm)
    return h.reshape(bsz, seq, d)
```

```python
import functools
import math

import numpy as np
import jax
import jax.numpy as jnp
from jax import lax
from jax.experimental import pallas as pl
from jax.experimental.pallas import tpu as pltpu

F32 = jnp.float32
BF16 = jnp.bfloat16

EPS = 1e-6
NEG_BIG = -1e30
EXP_CLIP = 60.0

GROUP_W = 256
HEADS = 4
DH = GROUP_W // HEADS
HEAD_SHIFT = DH.bit_length() - 1
S5_G, S5_P, S5_N = 16, 16, 64
ML_CONV = 4
CHUNK = 64
HG_LEVELS = (32, 16, 8, 4, 2, 1)
HG_SMALL_LEVELS = (4, 2, 1)
GATE_W = 128
FOX_LOOKAHEAD = 2
POST_ROWS = 256

VMEM_LIMIT_BYTES = 56 * 1024 * 1024
MXU_TILE = 256
SUBLANES = 8
BF16_ROWS = 16


def _cparams(*sem):
    return pltpu.CompilerParams(dimension_semantics=sem, vmem_limit_bytes=VMEM_LIMIT_BYTES)


def _dot(a, b):
    return jnp.dot(a.astype(BF16), b.astype(BF16), preferred_element_type=F32)


def _dot_nt(a, b):
    return lax.dot_general(a.astype(BF16), b.astype(BF16), (((1,), (1,)), ((), ())),
                           preferred_element_type=F32)


def _dot_tn(a, b):
    return lax.dot_general(a.astype(BF16), b.astype(BF16), (((0,), (0,)), ((), ())),
                           preferred_element_type=F32)


def _split(x, n):
    parts, r = [], x
    for i in range(n):
        p = r.astype(BF16)
        parts.append(p)
        if i + 1 < n:
            r = r - p.astype(F32)
    return parts


def _sel_dot(m01, x, n=3):
    out = None
    for p in _split(x, n):
        t = jnp.dot(m01, p, preferred_element_type=F32)
        out = t if out is None else out + t
    return out


def _dot_sel(x, m01, n=3):
    mm = _halves_dot if (x.shape[0] >= 2 * MXU_TILE and m01.shape[1] <= MXU_TILE) else (
        lambda a, b: jnp.dot(a, b, preferred_element_type=F32))
    out = None
    for p in _split(x, n):
        t = mm(p, m01)
        out = t if out is None else out + t
    return out


def _halves_dot(a, b):
    half = a.shape[0] // 2
    return jnp.concatenate([jnp.dot(a[0:half], b, preferred_element_type=F32),
                            jnp.dot(a[half:], b, preferred_element_type=F32)], axis=0)


def _log_sigmoid(z):
    return jnp.minimum(z, 0.0) - jnp.log(1.0 + jnp.exp(-jnp.abs(z)))


def _sigmoid(z):
    return 1.0 / (1.0 + jnp.exp(-z))


def _rms(x, gain):
    ms = jnp.mean(x * x, axis=-1, keepdims=True)
    return x * lax.rsqrt(ms + EPS) * gain


def _head_rms(o, mh, gain):
    ms = _dot_sel(o * o, mh, 2)
    return o * lax.rsqrt(ms + EPS) * gain


def _block_diag_mask(n):
    r = lax.broadcasted_iota(jnp.int32, (n, n), 0)
    c = lax.broadcasted_iota(jnp.int32, (n, n), 1)
    return (r >> HEAD_SHIFT) == (c >> HEAD_SHIFT)


LANES = 128
PAIRS = GROUP_W // LANES


def _pair_tile(x, bdp):
    return jnp.where(bdp, jnp.concatenate([x, x], axis=0), 0.0).astype(BF16)


def _pairs(x):
    return [x[:, p * LANES:(p + 1) * LANES] for p in range(PAIRS)]


def _heads_nt(a, x, bdp):
    return jnp.concatenate([_dot_nt(ap, _pair_tile(xp, bdp)) for ap, xp in zip(_pairs(a), _pairs(x))], axis=1)


def _heads_nn(a, x, bdp):
    return jnp.concatenate([_dot(ap, _pair_tile(xp, bdp)) for ap, xp in zip(_pairs(a), _pairs(x))], axis=1)


def _hgrn_level_mats():
    L = CHUNK
    t = np.arange(L)[:, None]
    j = np.arange(L)[None, :]
    blocks = [j <= t]
    for m in HG_SMALL_LEVELS:
        ref = (t // (2 * m)) * 2 * m + m - 1
        blocks.append(j <= ref)
    return np.concatenate(blocks, axis=0).astype(np.float32)


def _head_block(value):
    i = np.arange(GROUP_W)
    return np.where((i[:, None] // DH) == (i[None, :] // DH), value, 0.0).astype(np.float32)


def _gate_expand(col0):
    e = np.zeros((GATE_W, GROUP_W), np.float32)
    for h in range(HEADS):
        e[col0 + h, h * DH:(h + 1) * DH] = 1.0
    return e


def _lb_kernel(logit_ref, o_ref):
    x = logit_ref[...]
    depth = x.shape[0]
    m = x[0:1]
    for l in range(1, depth):
        m = jnp.maximum(m, x[l:l + 1])
    e = [jnp.exp(x[l:l + 1] - m) for l in range(depth)]
    tot = e[0]
    for l in range(1, depth):
        tot = tot + e[l]
    p = [el / tot for el in e]
    c = None
    for l in range(depth):
        c = p[l] if c is None else c + p[l]
        o_ref[l:l + 1, :] = jnp.maximum(c - p[0], 0.0)


def _s5_param_kernel(lr_ref, li_ref, ldt_ref, bre_ref, bim_ref, abr_ref, abi_ref, bbr_ref, bbi_ref):
    lr = jnp.minimum(lr_ref[...], -1e-4)
    li = li_ref[...]
    dt = jnp.exp(ldt_ref[...])
    mag = jnp.exp(lr * dt)
    ab_re = mag * jnp.cos(li * dt)
    ab_im = mag * jnp.sin(li * dt)
    den = lr * lr + li * li
    cf_re = ((ab_re - 1.0) * lr + ab_im * li) / den
    cf_im = (ab_im * lr - (ab_re - 1.0) * li) / den
    bre = bre_ref[...]
    bim = bim_ref[...]
    abr_ref[...] = ab_re
    abi_ref[...] = ab_im
    bbr_ref[...] = cf_re * bre - cf_im * bim
    bbi_ref[...] = cf_re * bim + cf_im * bre


def _s5_params(lam_re, lam_im, b_re, b_im, c_re, c_im, log_dt):
    gp = S5_G * S5_P
    rep = lambda a: jnp.repeat(a, S5_P, axis=0)
    ldt = jnp.broadcast_to(rep(log_dt[:, None]), (gp, S5_N))
    bt = lambda a: a.transpose(0, 2, 1).reshape(gp, S5_N)
    shp = jax.ShapeDtypeStruct((gp, S5_N), F32)
    ab_re, ab_im, bb_re, bb_im = pl.pallas_call(
        _s5_param_kernel, out_shape=(shp, shp, shp, shp), name="s5_params",
    )(rep(lam_re), rep(lam_im), ldt, bt(b_re), bt(b_im))
    lam = jnp.stack([ab_re[::S5_P].reshape(-1), ab_im[::S5_P].reshape(-1)])
    own_gp = jnp.asarray(np.eye(S5_G, dtype=np.float32).repeat(S5_P, axis=0))
    own_gn = jnp.asarray(np.eye(S5_G, dtype=np.float32).repeat(S5_N, axis=0))
    wide = lambda bb: (bb[:, None, :] * own_gp[:, :, None]).reshape(gp, S5_G * S5_N)
    wb = jnp.concatenate([wide(bb_re), wide(bb_im)], axis=1).astype(BF16)
    tall = lambda c: (c.transpose(0, 2, 1).reshape(S5_G * S5_N, S5_P)[:, None, :]
                      * own_gn[:, :, None]).reshape(S5_G * S5_N, gp)
    cm = jnp.concatenate([tall(c_re), -tall(c_im)], axis=0).astype(BF16)
    return lam, wb, cm


IN_COLS = (GROUP_W, 4 * GROUP_W, GROUP_W, 4 * GROUP_W)
IN_DTYPES = (F32, F32, BF16, F32)
GATE_ROWS = 16


def _in_proj_kernel(x_ref, g_ref, w_ref, wt_ref, o_u5, o_hg, o_fk, o_ml, o_fqt, o_fvt, o_gt):
    gw = GROUP_W
    a = _rms(x_ref[...], g_ref[...]).astype(BF16)
    outs = (o_u5, o_hg, o_fk, o_ml)
    c0 = 0
    for o_ref, width in zip(outs, IN_COLS):
        o_ref[...] = jnp.dot(a, w_ref[:, c0:c0 + width], preferred_element_type=F32).astype(o_ref.dtype)
        c0 += width
    t = lax.dot_general(wt_ref[...], a, (((1,), (1,)), ((), ())), preferred_element_type=F32)
    o_fqt[...] = t[0:gw]
    o_fvt[...] = t[gw:2 * gw].astype(o_fvt.dtype)
    o_gt[...] = t[2 * gw:2 * gw + GATE_ROWS]


def _time_major_map(tiles_per_seq):
    return lambda i: (i % tiles_per_seq, i // tiles_per_seq)


def _in_proj(h, gain, w_in_l, tm, seq):
    t_rows, d = h.shape
    assert seq % tm == 0 and t_rows % seq == 0
    gw = GROUP_W
    o_fox_f = 8 * gw
    o_ml = o_fox_f + HEADS
    o_ml_i = o_ml + 4 * gw
    o_ml_f = o_ml_i + HEADS
    gates = jnp.concatenate([w_in_l[:, o_fox_f:o_fox_f + HEADS], w_in_l[:, o_ml_i:o_ml_i + HEADS],
                             w_in_l[:, o_ml_f:o_ml_f + HEADS],
                             jnp.zeros((d, GATE_ROWS - 3 * HEADS), w_in_l.dtype)], axis=1)
    w = jnp.concatenate([w_in_l[:, :5 * gw], w_in_l[:, 6 * gw:7 * gw], w_in_l[:, o_ml:o_ml + 4 * gw]],
                        axis=1).astype(BF16)
    wt = jnp.concatenate([w_in_l[:, 5 * gw:6 * gw], w_in_l[:, 7 * gw:8 * gw], gates], axis=1).T.astype(BF16)
    n_tot = sum(IN_COLS)
    row = lambda i: (i, 0)
    colb = lambda i: (0, i)
    fixed = lambda i: (0, 0)
    return pl.pallas_call(
        _in_proj_kernel,
        grid=(t_rows // tm,),
        in_specs=[pl.BlockSpec((tm, d), row), pl.BlockSpec((1, d), fixed),
                  pl.BlockSpec((d, n_tot), fixed, pipeline_mode=pl.Buffered(1)),
                  pl.BlockSpec((2 * gw + GATE_ROWS, d), fixed, pipeline_mode=pl.Buffered(1))],
        out_specs=[pl.BlockSpec((tm, gw), _time_major_map(seq // tm))]
        + [pl.BlockSpec((tm, c), row) for c in IN_COLS[1:]]
        + [pl.BlockSpec((gw, tm), colb), pl.BlockSpec((gw, tm), colb),
           pl.BlockSpec((GATE_ROWS, tm), lambda i: (i // (seq // tm), i % (seq // tm)))],
        out_shape=[jax.ShapeDtypeStruct((seq, (t_rows // seq) * gw), F32)]
        + [jax.ShapeDtypeStruct((t_rows, c), dt) for c, dt in zip(IN_COLS[1:], IN_DTYPES[1:])]
        + [jax.ShapeDtypeStruct((gw, t_rows), F32), jax.ShapeDtypeStruct((gw, t_rows), BF16),
           jax.ShapeDtypeStruct(((t_rows // seq) * GATE_ROWS, seq), F32)],
        compiler_params=_cparams("parallel"),
        name="in_proj",
    )(h, gain[None, :], w, wt)


def _s5_kernel(u_ref, perm_ref, permt_ref, wb_ref, lam_ref, cm_ref, d_ref, wglu_ref, gain_ref, o_ref,
               *scratch, hb, nb):
    gw = GROUP_W
    ns = S5_G * S5_N
    xs_refs, st_ref = scratch[:-1], scratch[-1]
    nsub = len(xs_refs)

    @pl.when(pl.program_id(0) == 0)
    def _():
        st_ref[...] = jnp.zeros_like(st_ref)

    ar = jnp.broadcast_to(lam_ref[0:1, :], (nb, ns))
    ai = jnp.broadcast_to(lam_ref[1:2, :], (nb, ns))

    def front(k):
        u = jnp.concatenate([u_ref[k * hb:(k + 1) * hb, b * gw:(b + 1) * gw] for b in range(nb)], axis=0)
        u_tb = _halves_dot(perm_ref[...], u.astype(BF16)).astype(BF16)
        xs_refs[k][...] = jnp.dot(u_tb, wb_ref[...], preferred_element_type=F32)
        return u

    def scan(k, xr, xi):
        xs = xs_refs[k]
        for t in range(hb):
            r = slice(t * nb, (t + 1) * nb)
            nr = ar * xr - ai * xi + xs[r, 0:ns]
            ni = ar * xi + ai * xr + xs[r, ns:2 * ns]
            xs[r, 0:ns] = nr
            xs[r, ns:2 * ns] = ni
            xr, xi = nr, ni
        return xr, xi

    def back(k, u):
        cx_tb = _halves_dot(xs_refs[k][...].astype(BF16), cm_ref[...])
        cx = None
        for part in _split(cx_tb, 2):
            t = _halves_dot(permt_ref[...], part)
            cx = t if cx is None else cx + t
        y = cx + d_ref[...] * u
        g = jax.nn.gelu(y)
        y = g * _sigmoid(_halves_dot(g.astype(BF16), wglu_ref[...]))
        out = _rms(y, gain_ref[...]).astype(o_ref.dtype)
        for b in range(nb):
            o_ref[k * hb:(k + 1) * hb, b * gw:(b + 1) * gw] = out[b * hb:(b + 1) * hb]

    xr, xi = st_ref[:, 0:ns], st_ref[:, ns:2 * ns]
    us = {0: front(0)}
    for k in range(nsub):
        if k + 1 < nsub:
            us[k + 1] = front(k + 1)
        xr, xi = scan(k, xr, xi)
        if k + 1 == nsub:
            st_ref[:, 0:ns] = xr
            st_ref[:, ns:2 * ns] = xi
        back(k, us.pop(k))


def _s5_mixer(u2d, lam, wb, cm, d_skip, w_glu, gain, lb, nb, nsub):
    s, w = u2d.shape
    gw = GROUP_W
    assert w == nb * gw
    ns2 = 2 * S5_G * S5_N
    assert lb % nsub == 0 and s % lb == 0
    hb = lb // nsub
    idx = np.arange(hb * nb)
    perm = np.zeros((hb * nb, hb * nb), np.float32)
    perm[idx, (idx % nb) * hb + idx // nb] = 1.0
    fixed = lambda i: (0, 0)
    return pl.pallas_call(
        functools.partial(_s5_kernel, hb=hb, nb=nb),
        grid=(s // lb,),
        in_specs=[pl.BlockSpec((lb, w), lambda i: (i, 0)),
                  pl.BlockSpec(perm.shape, fixed), pl.BlockSpec(perm.shape, fixed),
                  pl.BlockSpec((gw, ns2), fixed), pl.BlockSpec((2, ns2 // 2), fixed),
                  pl.BlockSpec((ns2, gw), fixed), pl.BlockSpec((1, gw), fixed),
                  pl.BlockSpec((gw, gw), fixed), pl.BlockSpec((1, gw), fixed)],
        out_specs=pl.BlockSpec((lb, w), lambda i: (i, 0)),
        out_shape=jax.ShapeDtypeStruct((s, w), BF16),
        scratch_shapes=[pltpu.VMEM((hb * nb, ns2), F32)] * nsub + [pltpu.VMEM((nb, ns2), F32)],
        compiler_params=_cparams("arbitrary"),
        name="s5_mixer",
    )(u2d, jnp.asarray(perm, BF16), jnp.asarray(perm.T, BF16), wb, lam, cm, d_skip[None, :],
      w_glu.astype(BF16), gain[None, :])


def _hgrn_kernel(x_ref, lb_ref, gain_ref, mall_ref, mh_ref, o_ref, st_ref, *, ngroup, nsub):
    L = CHUNK
    gw = GROUP_W
    R = nsub * L

    @pl.when(pl.program_id(1) == 0)
    def _():
        st_ref[...] = jnp.zeros_like(st_ref)

    lb = lb_ref[...]
    gain = gain_ref[...]
    bdp = _block_diag_mask(LANES)
    row = lax.broadcasted_iota(jnp.int32, (R, gw), 0) & (L - 1)
    row_c = lax.broadcasted_iota(jnp.int32, (L, gw), 0)
    col_c = lax.broadcasted_iota(jnp.int32, (L, gw), 1) & (DH - 1)
    chunks = [slice(c * L, (c + 1) * L) for c in range(nsub)]

    def group(gi, carry):
        r0 = pl.multiple_of(gi * R, R)
        q = x_ref[0, pl.ds(r0, R), 0:gw]
        z = x_ref[0, pl.ds(r0, R), gw:2 * gw]
        v = x_ref[0, pl.ds(r0, R), 2 * gw:3 * gw]
        gg = x_ref[0, pl.ds(r0, R), 3 * gw:4 * gw]
        logf = _log_sigmoid(z) + jnp.log(1.0 + lb * jnp.exp(jnp.minimum(-z, EXP_CLIP)))
        kk = (1.0 - lb) * _sigmoid(-z)
        cums = [_sel_dot(mall_ref[...], logf[c]) for c in chunks]
        bs = [cm[0:L] for cm in cums]
        b = jnp.concatenate(bs, axis=0)
        a = [jnp.where(row_c == col_c, _heads_nt(q[c], kk[c], bdp), 0.0) for c in chunks]
        for m in HG_LEVELS:
            upper = (row_c & m) != 0
            sh = int(math.log2(2 * m))
            same = (row_c >> sh) == (col_c >> sh)
            for n, c in enumerate(chunks):
                if m in HG_SMALL_LEVELS:
                    i = 1 + HG_SMALL_LEVELS.index(m)
                    b_ref = cums[n][i * L:(i + 1) * L]
                else:
                    b_ref = jnp.concatenate([jnp.broadcast_to(bs[n][c0 + m - 1:c0 + m], (2 * m, gw))
                                             for c0 in range(0, L, 2 * m)], axis=0)
                w = jnp.where(upper, q[c], kk[c]) * jnp.exp(-jnp.abs(bs[n] - b_ref))
                ql = jnp.where(upper, w, 0.0)
                kl = jnp.where(upper, 0.0, w)
                a[n] = a[n] + jnp.where(same, _heads_nt(ql, kl, bdp), 0.0)
        o_intra = [_heads_nn(a[n], v[c], bdp) for n, c in enumerate(chunks)]
        b_last = [b[(n + 1) * L - 1:(n + 1) * L] for n in range(nsub)]
        kdec = kk * jnp.exp(jnp.concatenate([jnp.broadcast_to(bl, (L, gw)) for bl in b_last], axis=0) - b)
        d_st = [[jnp.where(bdp, _dot_tn(vp, kp), 0.0) for vp, kp in zip(_pairs(v[c]), _pairs(kdec[c]))]
                for c in chunks]
        qe = q * jnp.exp(b)
        st = [st_ref[p] for p in range(PAIRS)]
        outs = []
        for n, c in enumerate(chunks):
            o_inter = jnp.concatenate([_dot_nt(qp, st[p]) for p, qp in enumerate(_pairs(qe[c]))], axis=1)
            outs.append(o_intra[n] + o_inter)
            decay = _pairs(jnp.exp(b_last[n]))
            st = [st[p] * decay[p] + d_st[n][p] for p in range(PAIRS)]
        for p in range(PAIRS):
            st_ref[p] = st[p]
        o = jnp.concatenate(outs, axis=0)
        out = _head_rms(o, mh_ref[...], gain) * (gg * _sigmoid(gg))
        o_ref[0, pl.ds(r0, R), :] = out.astype(o_ref.dtype)
        return carry

    lax.fori_loop(0, ngroup, group, 0)


def _hgrn_mixer(hg, lb, gain, lg, nsub=8):
    b, s, w = hg.shape
    gw = GROUP_W
    assert lg % (CHUNK * nsub) == 0
    mall = jnp.asarray(_hgrn_level_mats(), BF16)
    mh = jnp.asarray(_head_block(1.0 / DH), BF16)
    fixed = lambda i, j: (0, 0)
    return pl.pallas_call(
        functools.partial(_hgrn_kernel, ngroup=lg // (CHUNK * nsub), nsub=nsub),
        grid=(b, s // lg),
        in_specs=[pl.BlockSpec((1, lg, w), lambda i, j: (i, j, 0)),
                  pl.BlockSpec((1, gw), fixed), pl.BlockSpec((1, gw), fixed),
                  pl.BlockSpec(mall.shape, fixed), pl.BlockSpec((gw, gw), fixed)],
        out_specs=pl.BlockSpec((1, lg, gw), lambda i, j: (i, j, 0)),
        out_shape=jax.ShapeDtypeStruct((b, s, gw), BF16),
        scratch_shapes=[pltpu.VMEM((PAIRS, LANES, LANES), F32)],
        compiler_params=_cparams("parallel", "arbitrary"),
        name="hgrn2_mixer",
    )(hg, lb[None, :], gain[None, :], mall, mh)


FOX_SPLIT = 3


def _fox_bias_lane(h, j):
    return (h ^ 1) * DH + j


def _gate_kernel(gt_ref, bias_ref, triu_ref, place_ref, ones_ref, eye_ref, kb_ref, rowo_ref, grow_ref, carry_ref,
                 *, nb):
    rows = GATE_ROWS

    @pl.when(pl.program_id(0) == 0)
    def _():
        carry_ref[...] = jnp.zeros_like(carry_ref)

    g = gt_ref[...] + bias_ref[...]
    cs = _dot_sel(_log_sigmoid(g), triu_ref[...]) + carry_ref[...]
    lg = cs.shape[1]
    carry_ref[...] = cs[:, lg - 1:lg]
    ck = cs * (-math.log2(math.e))
    for b in range(nb):
        r = slice(b * rows, (b + 1) * rows)
        rowo_ref[b] = cs[b * rows:b * rows + SUBLANES]
        grow_ref[b] = _dot_tn(jnp.concatenate(_split(g[r], FOX_SPLIT), axis=0), eye_ref[...])
        kb = _dot_tn(jnp.concatenate(_split(ck[r], FOX_SPLIT), axis=0), place_ref[...]) + ones_ref[...]
        kb_ref[b] = kb.astype(kb_ref.dtype)


def _gates(gt, gate_bias_l, bsz, lg):
    rows = GATE_ROWS
    s = gt.shape[1]
    assert gt.shape[0] == bsz * rows and s % lg == 0
    nblk = s // lg
    gw = GROUP_W
    triu = jnp.asarray(np.triu(np.ones((lg, lg), np.float32)), BF16)
    bias = jnp.tile(jnp.concatenate([gate_bias_l, jnp.zeros((rows - gate_bias_l.shape[0],), F32)]), bsz)[:, None]
    place = np.zeros((FOX_SPLIT * rows, gw), np.float32)
    eye = np.zeros((FOX_SPLIT * rows, GATE_W), np.float32)
    ones = np.zeros((1, gw), np.float32)
    for j in range(FOX_SPLIT):
        for h in range(HEADS):
            place[j * rows + h, _fox_bias_lane(h, j)] = 1.0
            ones[0, _fox_bias_lane(h, FOX_SPLIT + j)] = 1.0
        for f in range(rows):
            eye[j * rows + f, f] = 1.0
    fixed = lambda j: (0, 0)
    k_bias, cf_row, g_rows = pl.pallas_call(
        functools.partial(_gate_kernel, nb=bsz),
        grid=(nblk,),
        in_specs=[pl.BlockSpec((bsz * rows, lg), lambda j: (0, j)),
                  pl.BlockSpec((bsz * rows, 1), fixed), pl.BlockSpec((lg, lg), fixed),
                  pl.BlockSpec(place.shape, fixed), pl.BlockSpec((1, gw), fixed), pl.BlockSpec(eye.shape, fixed)],
        out_specs=[pl.BlockSpec((bsz, lg, gw), lambda j: (0, j, 0)),
                   pl.BlockSpec((bsz, SUBLANES, lg), lambda j: (0, 0, j)),
                   pl.BlockSpec((bsz, lg, GATE_W), lambda j: (0, j, 0))],
        out_shape=[jax.ShapeDtypeStruct((bsz, s, gw), BF16), jax.ShapeDtypeStruct((bsz, SUBLANES, s), F32),
                   jax.ShapeDtypeStruct((bsz, s, GATE_W), F32)],
        scratch_shapes=[pltpu.VMEM((bsz * rows, 1), F32)],
        compiler_params=_cparams("arbitrary"),
        name="gate_cumsums",
    )(gt, bias, triu, jnp.asarray(place, BF16), jnp.asarray(ones), jnp.asarray(eye, BF16))
    return k_bias.reshape(bsz * s, gw), cf_row, g_rows


def _fox_kernel(qt_ref, k_ref, kb_ref, vt_ref, crow_ref, gain_ref, o_ref, kaug_ref, *, tq, tk):
    qi = pl.program_id(1)
    gw = GROUP_W
    seq = k_ref.shape[0]
    log2e = math.log2(math.e)
    fill_rows = min(seq, 512)

    @pl.when(qi == 0)
    def _():
        lane_head = lax.broadcasted_iota(jnp.int32, (fill_rows, gw), 1) >> HEAD_SHIFT

        def fill(i, carry):
            r0 = pl.multiple_of(i * fill_rows, fill_rows)
            kk = k_ref[pl.ds(r0, fill_rows), :]
            kb = kb_ref[pl.ds(r0, fill_rows), :]
            for h in range(HEADS):
                kaug_ref[h, pl.ds(r0, fill_rows), :] = jnp.where(lane_head == h, kk, kb)
            return carry

        lax.fori_loop(0, seq // fill_rows, fill, 0)

    qt = qt_ref[...] * (DH ** -0.5 * log2e)
    row = lax.broadcasted_iota(jnp.int32, (gw, tq), 0)
    head_row = row >> HEAD_SHIFT
    q_bias = jnp.where((row & (DH - 1)) < FOX_SPLIT, 1.0, 0.0)
    for h in range(HEADS):
        for j, part in enumerate(_split(crow_ref[0, h:h + 1, :] * log2e, FOX_SPLIT)):
            q_bias = jnp.where(row == _fox_bias_lane(h, FOX_SPLIT + j), part.astype(F32), q_bias)
    q_heads = [jnp.where(head_row == h, qt, jnp.where(head_row == (h ^ 1), q_bias, 0.0)).astype(BF16)
               for h in range(HEADS)]
    key_i = lax.broadcasted_iota(jnp.int32, (tk, tq), 0)
    qry_i = lax.broadcasted_iota(jnp.int32, (tk, tq), 1)
    kv_per_q = tq // tk
    ones_rows = jnp.ones((BF16_ROWS, tk), BF16)

    def run_tiles(tiles, state):
        state = list(state)

        def scores(j, diag_offset, h):
            r0 = pl.multiple_of(j * tk, tk)
            u = jnp.dot(kaug_ref[h, pl.ds(r0, tk), :], q_heads[h], preferred_element_type=F32)
            if diag_offset is not None:
                u = jnp.where(key_i + diag_offset <= qry_i, u, NEG_BIG)
            m_new = jnp.maximum(state[h][0], jnp.max(u, axis=0, keepdims=True))
            return u, m_new

        def accumulate(j, h, u, m_new):
            r0 = pl.multiple_of(j * tk, tk)
            m_old, l_old, acc_old = state[h]
            alpha = jnp.exp2(m_old - m_new)
            vt1 = jnp.concatenate([vt_ref[h * DH:(h + 1) * DH, pl.ds(r0, tk)], ones_rows], axis=0)
            pv = None
            for k0 in range(0, tk, MXU_TILE):
                p = jnp.exp2(u[k0:k0 + MXU_TILE] - m_new).astype(BF16)
                t = jnp.dot(vt1[:, k0:k0 + MXU_TILE], p, preferred_element_type=F32)
                pv = t if pv is None else pv + t
            l_new = alpha * l_old + pv[DH:DH + 1]
            state[h] = (m_new, l_new, alpha * acc_old + pv[0:DH])

        items = [(j, off, h) for j, off in tiles for h in range(HEADS)]
        ahead = min(FOX_LOOKAHEAD, HEADS - 1)
        queue = [scores(*it) for it in items[:ahead]]
        for n, (j, off, h) in enumerate(items):
            if n + ahead < len(items):
                queue.append(scores(*items[n + ahead]))
            accumulate(j, h, *queue.pop(0))
        return tuple(state)

    init = tuple((jnp.full((1, tq), NEG_BIG, F32), jnp.zeros((1, tq), F32), jnp.zeros((DH, tq), F32))
                 for _ in range(HEADS))
    n_full = qi * kv_per_q
    state = lax.fori_loop(0, n_full // 2, lambda i, s: run_tiles([(2 * i, None), (2 * i + 1, None)], s), init)
    diag = [(n_full + d, d * tk) for d in range(kv_per_q)]
    state = lax.cond(n_full % 2 == 1,
                     lambda s: run_tiles([(n_full - 1, None)] + diag, s),
                     lambda s: run_tiles(diag, s), state)

    outs = []
    for h in range(HEADS):
        _, l_fin, acc_fin = state[h]
        o = acc_fin / l_fin
        ms = jnp.mean(o * o, axis=0, keepdims=True)
        outs.append(o * lax.rsqrt(ms + EPS))
    out = jnp.concatenate(outs, axis=0) * gain_ref[...]
    o_ref[...] = out.T.astype(o_ref.dtype)


def _fox_mixer(fqt, fk, k_bias, fvt, cf_row, gain, bsz, tq, tk):
    gw, t_rows = fqt.shape
    s = t_rows // bsz
    nq = s // tq
    return pl.pallas_call(
        functools.partial(_fox_kernel, tq=tq, tk=tk),
        grid=(bsz, nq),
        in_specs=[pl.BlockSpec((gw, tq), lambda i, qi: (0, i * nq + qi)),
                  pl.BlockSpec((s, gw), lambda i, qi: (i, 0)),
                  pl.BlockSpec((s, gw), lambda i, qi: (i, 0)),
                  pl.BlockSpec((gw, s), lambda i, qi: (0, i)),
                  pl.BlockSpec((1, SUBLANES, tq), lambda i, qi: (i, 0, qi)),
                  pl.BlockSpec((gw, 1), lambda i, qi: (0, 0))],
        out_specs=pl.BlockSpec((tq, gw), lambda i, qi: (i * nq + qi, 0)),
        out_shape=jax.ShapeDtypeStruct((t_rows, gw), BF16),
        scratch_shapes=[pltpu.VMEM((HEADS, s, gw), BF16)],
        compiler_params=_cparams("arbitrary", "arbitrary"),
        name="fox_mixer",
    )(fqt, fk, k_bias, fvt, cf_row, gain[:, None])


def _head_lane_max(x):
    rows = x.shape[0]
    parts = [jnp.broadcast_to(jnp.max(x[:, h * DH:(h + 1) * DH], axis=-1, keepdims=True), (rows, DH))
             for h in range(HEADS)]
    return jnp.concatenate(parts, axis=-1)


def _mlstm_kernel(x_ref, gt_ref, cw_ref, gain_ref, tri_ref, eb_ref, ei_ref, mh_ref, o_ref,
                  cbuf, ct_ref, m_ref, *, nsub, lg):
    L = CHUNK
    gw = GROUP_W
    tail = SUBLANES

    @pl.when(pl.program_id(1) == 0)
    def _():
        cbuf[0:tail, :] = jnp.zeros((tail, 2 * gw), F32)
        ct_ref[...] = jnp.zeros_like(ct_ref)
        m_ref[...] = jnp.zeros_like(m_ref)

    cbuf[tail:tail + lg, :] = x_ref[0, :, 0:2 * gw]
    acc = None
    for j in range(ML_CONV):
        term = cbuf[pl.ds(tail - (ML_CONV - 1) + j, lg), :] * cw_ref[j:j + 1, :]
        acc = term if acc is None else acc + term
    cbuf[0:tail, :] = cbuf[lg:lg + tail, :]
    qk = acc * _sigmoid(acc)
    q = qk[:, 0:gw]
    k = qk[:, gw:2 * gw] * (DH ** -0.5)
    v = x_ref[0, :, 2 * gw:3 * gw]
    og = x_ref[0, :, 3 * gw:4 * gw]
    g = gt_ref[0]

    bdp = _block_diag_mask(LANES)
    bdp2 = jnp.concatenate([bdp, bdp], axis=1)
    row = lax.broadcasted_iota(jnp.int32, (L, gw), 0)
    col = lax.broadcasted_iota(jnp.int32, (L, gw), 1) & (DH - 1)
    causal = col <= row
    diag = col == row
    ones = jnp.ones((L, LANES), F32)
    ones_blocks = jnp.where(bdp, 1.0, 0.0).astype(BF16)
    chunks = [slice(c * L, (c + 1) * L) for c in range(nsub)]

    def regroup(parts):
        return jnp.concatenate([t[:, 0:LANES] for t in parts] + [t[:, LANES:2 * LANES] for t in parts], axis=1)

    lsg = _log_sigmoid(g)
    cs = jnp.concatenate([_sel_dot(tri_ref[...], lsg[c]) for c in chunks], axis=0)
    b_exp = _dot_sel(cs, eb_ref[...])
    imb = _dot_sel(g, ei_ref[...]) - b_exp

    m_loc, nd_loc, b_last, m_src, d_ct = [], [], [], [], []
    for c in chunks:
        imb_row = jnp.sum(jnp.where(diag, imb[c], 0.0), axis=0, keepdims=True)
        d_log = jnp.where(causal, b_exp[c] + imb_row, NEG_BIG)
        ml = _head_lane_max(d_log)
        qk_loc = _heads_nt(q[c], k[c], bdp) * jnp.exp(d_log - ml)
        nd_loc.append(regroup([_dot(qp, jnp.concatenate([_pair_tile(vp, bdp), ones_blocks], axis=1))
                               for qp, vp in zip(_pairs(qk_loc), _pairs(v[c]))]))
        m_loc.append(ml)
        bl = b_exp[c][L - 1:L]
        src = bl + imb[c]
        ms = jnp.max(src, axis=0, keepdims=True)
        kw = k[c] * jnp.exp(src - ms)
        d_ct.append([jnp.where(bdp2, _dot_tn(kp, jnp.concatenate([vp, ones], axis=1)), 0.0)
                     for kp, vp in zip(_pairs(kw), _pairs(v[c]))])
        b_last.append(bl)
        m_src.append(ms)

    ct = [ct_ref[p] for p in range(PAIRS)]
    m_prev = m_ref[...]
    hs = []
    for n, c in enumerate(chunks):
        inter = b_exp[c] + m_prev
        m_t = jnp.maximum(inter, m_loc[n])
        w_inter = jnp.exp(inter - m_t)
        w_loc = jnp.exp(m_loc[n] - m_t)
        q_ct = regroup([_dot(qp, ct[p]) for p, qp in enumerate(_pairs(q[c]))])
        nd = (jnp.concatenate([w_inter, w_inter], axis=1) * q_ct
              + jnp.concatenate([w_loc, w_loc], axis=1) * nd_loc[n])
        hs.append(nd[:, 0:gw] / jnp.maximum(jnp.abs(nd[:, gw:2 * gw]), jnp.exp(-m_t)))
        m_new = jnp.maximum(b_last[n] + m_prev, m_src[n])
        decay = _pairs(jnp.exp(b_last[n] + m_prev - m_new))
        w_src = _pairs(jnp.exp(m_src[n] - m_new))
        ct = [ct[p] * jnp.concatenate([decay[p], decay[p]], axis=1)
              + d_ct[n][p] * jnp.concatenate([w_src[p], w_src[p]], axis=1) for p in range(PAIRS)]
        m_prev = m_new
    for p in range(PAIRS):
        ct_ref[p] = ct[p]
    m_ref[...] = m_prev
    hh = jnp.concatenate(hs, axis=0)
    o_ref[0] = (_head_rms(hh, mh_ref[...], gain_ref[...]) * _sigmoid(og)).astype(o_ref.dtype)


def _mlstm_mixer(ml, gt, conv_w, gain, lg):
    b, s, w = ml.shape
    gw = GROUP_W
    assert lg % CHUNK == 0
    tri = jnp.asarray(np.tril(np.ones((CHUNK, CHUNK), np.float32)), BF16)
    eb = jnp.asarray(_gate_expand(2 * HEADS), BF16)
    ei = jnp.asarray(_gate_expand(HEADS), BF16)
    mh = jnp.asarray(_head_block(1.0 / DH), BF16)
    fixed = lambda i, j: (0, 0)
    blk = lambda i, j: (i, j, 0)
    return pl.pallas_call(
        functools.partial(_mlstm_kernel, nsub=lg // CHUNK, lg=lg),
        grid=(b, s // lg),
        in_specs=[pl.BlockSpec((1, lg, w), blk), pl.BlockSpec((1, lg, GATE_W), blk),
                  pl.BlockSpec((ML_CONV, 2 * gw), fixed), pl.BlockSpec((1, gw), fixed),
                  pl.BlockSpec((CHUNK, CHUNK), fixed), pl.BlockSpec((GATE_W, gw), fixed),
                  pl.BlockSpec((GATE_W, gw), fixed), pl.BlockSpec((gw, gw), fixed)],
        out_specs=pl.BlockSpec((1, lg, gw), blk),
        out_shape=jax.ShapeDtypeStruct((b, s, gw), BF16),
        scratch_shapes=[pltpu.VMEM((lg + SUBLANES, 2 * gw), F32), pltpu.VMEM((PAIRS, LANES, 2 * LANES), F32),
                        pltpu.VMEM((1, gw), F32)],
        compiler_params=_cparams("parallel", "arbitrary"),
        name="mlstm_mixer",
    )(ml, gt, conv_w, gain[None, :], tri, eb, ei, mh)


def _post_kernel(ya_ref, yb_ref, yc_ref, yd_ref, h_ref, wo_ref, gpost_ref, gpre_ref, wg_ref, wu_ref, wd_ref,
                 gffn_ref, o_ref, *, ff_chunk):
    gw = GROUP_W
    tm = h_ref.shape[0]
    for r0 in range(0, tm, 2 * POST_ROWS):
        _post_rows(ya_ref, yb_ref, yc_ref, yd_ref, h_ref, wo_ref, gpost_ref, gpre_ref, wg_ref, wu_ref, wd_ref,
                   gffn_ref, o_ref, ff_chunk, [slice(r0, r0 + POST_ROWS), slice(r0 + POST_ROWS, r0 + 2 * POST_ROWS)])


def _post_rows(ya_ref, yb_ref, yc_ref, yd_ref, h_ref, wo_ref, gpost_ref, gpre_ref, wg_ref, wu_ref, wd_ref,
               gffn_ref, o_ref, ff_chunk, halves):
    gw = GROUP_W
    h1, a = [], []
    for r in halves:
        mix = None
        for i, y_ref in enumerate((ya_ref, yb_ref, yc_ref, yd_ref)):
            t = jnp.dot(y_ref[r, :], wo_ref[i * gw:(i + 1) * gw, :], preferred_element_type=F32)
            mix = t if mix is None else mix + t
        h1.append(h_ref[r, :] + _rms(mix, gpost_ref[...]))
    for n in range(len(halves)):
        a.append(_rms(h1[n], gpre_ref[...]).astype(BF16))
    d_ff = wg_ref.shape[1]
    ff = [None] * len(halves)
    for c0 in range(0, d_ff, ff_chunk):
        c1 = min(c0 + ff_chunk, d_ff)
        gu = [(jnp.dot(a[n], wg_ref[:, c0:c1], preferred_element_type=F32),
               jnp.dot(a[n], wu_ref[:, c0:c1], preferred_element_type=F32)) for n in range(len(halves))]
        for n, (g, u) in enumerate(gu):
            act = (g * _sigmoid(g) * u).astype(BF16)
            t = jnp.dot(act, wd_ref[c0:c1, :], preferred_element_type=F32)
            ff[n] = t if ff[n] is None else ff[n] + t
    for n, r in enumerate(halves):
        o_ref[r, :] = h1[n] + _rms(ff[n], gffn_ref[...])


def _post(ya, yb, yc, yd, h, w_out, g_post, g_pre, w_gate, w_up, w_down, g_ffn, tm):
    t_rows, d = h.shape
    gw = GROUP_W
    seq = ya.shape[0]
    assert seq % tm == 0
    d_ff = w_gate.shape[1]
    ff_chunk = min(d_ff, -(-d_ff // (2 * MXU_TILE)) * MXU_TILE)
    row = lambda i: (i, 0)
    fixed = lambda i: (0, 0)
    once = pl.Buffered(1)
    wspec = lambda shape: pl.BlockSpec(shape, fixed, pipeline_mode=once)
    gspec = pl.BlockSpec((1, d), fixed)
    return pl.pallas_call(
        functools.partial(_post_kernel, ff_chunk=ff_chunk),
        grid=(t_rows // tm,),
        in_specs=[pl.BlockSpec((tm, gw), _time_major_map(seq // tm))] + [pl.BlockSpec((tm, gw), row)] * 3
        + [pl.BlockSpec((tm, d), row), wspec((d, d)), gspec, gspec,
           wspec((d, d_ff)), wspec((d, d_ff)), wspec((d_ff, d)), gspec],
        out_specs=pl.BlockSpec((tm, d), row),
        out_shape=jax.ShapeDtypeStruct((t_rows, d), F32),
        compiler_params=_cparams("parallel"),
        name="out_proj_ffn",
    )(ya, yb, yc, yd, h, w_out.astype(BF16), g_post[None, :], g_pre[None, :],
      w_gate.astype(BF16), w_up.astype(BF16), w_down.astype(BF16), g_ffn[None, :])


def kernel(x, w_in, gate_bias, s5_lambda_re, s5_lambda_im, s5_b_re, s5_b_im, s5_c_re, s5_c_im, s5_d, s5_log_dt,
           s5_w_glu, hgrn_lb_logits, mlstm_conv_w, mix_gain, w_out, ln_mix_pre, ln_mix_post, ln_ffn_pre,
           ln_ffn_post, w_ffn_gate, w_ffn_up, w_ffn_down):
    bsz, seq, d = x.shape
    depth = w_in.shape[0]
    gw = GROUP_W
    tm = min(512, seq)
    tm_in = min(1024, seq)
    lg = min(512, seq)
    s5_sub = 64
    s5_nsub = max(1, min(4, seq // s5_sub))
    s5_lb = s5_sub * s5_nsub
    gate_lg = min(1024, seq)
    fox_tq = min(512, seq)
    fox_tk = fox_tq

    lb_all = pl.pallas_call(_lb_kernel, out_shape=jax.ShapeDtypeStruct(hgrn_lb_logits.shape, F32),
                            name="hgrn_lower_bounds")(hgrn_lb_logits)

    h = x.reshape(bsz * seq, d)
    for l in range(depth):
        gain = mix_gain[l]
        u5, hg, fk, ml, fqt, fvt, gt = _in_proj(h, ln_mix_pre[l], w_in[l], tm_in, seq)
        lam, wb, cm = _s5_params(s5_lambda_re[l], s5_lambda_im[l], s5_b_re[l], s5_b_im[l],
                                 s5_c_re[l], s5_c_im[l], s5_log_dt[l])
        ya = _s5_mixer(u5, lam, wb, cm, s5_d[l], s5_w_glu[l], gain[0:gw], s5_lb, bsz, s5_nsub)
        yb = _hgrn_mixer(hg.reshape(bsz, seq, 4 * gw), lb_all[l], gain[gw:2 * gw], min(2 * lg, seq),
                         nsub=min(2 * lg, seq) // CHUNK)
        k_bias, cf_row, gt3 = _gates(gt, gate_bias[l], bsz, gate_lg)
        yc = _fox_mixer(fqt, fk, k_bias, fvt, cf_row, gain[2 * gw:3 * gw], bsz, fox_tq, fox_tk)
        yd = _mlstm_mixer(ml.reshape(bsz, seq, 4 * gw), gt3, mlstm_conv_w[l], gain[3 * gw:4 * gw],
                          min(2 * lg, seq))
        h = _post(ya, yb.reshape(bsz * seq, gw), yc, yd.reshape(bsz * seq, gw), h,
                  w_out[l], ln_mix_post[l], ln_ffn_pre[l], w_ffn_gate[l], w_ffn_up[l], w_ffn_down[l],
                  ln_ffn_post[l], tm)
    return h.reshape(bsz, seq, d)
```

```python
import functools
import math

import numpy as np
import jax
import jax.numpy as jnp
from jax import lax
from jax.experimental import pallas as pl
from jax.experimental.pallas import tpu as pltpu

F32 = jnp.float32
BF16 = jnp.bfloat16

EPS = 1e-6
NEG_BIG = -1e30
EXP_CLIP = 60.0

GROUP_W = 256
HEADS = 4
DH = GROUP_W // HEADS
HEAD_SHIFT = DH.bit_length() - 1
S5_G, S5_P, S5_N = 16, 16, 64
ML_CONV = 4
CHUNK = 64
HG_LEVELS = (32, 16, 8, 4, 2, 1)
HG_SMALL_LEVELS = (4, 2, 1)
GATE_W = 128
FOX_LOOKAHEAD = 2
POST_ROWS = 256

VMEM_LIMIT_BYTES = 56 * 1024 * 1024
MXU_TILE = 256
SUBLANES = 8
BF16_ROWS = 16


def _cparams(*sem):
    return pltpu.CompilerParams(dimension_semantics=sem, vmem_limit_bytes=VMEM_LIMIT_BYTES)


def _dot(a, b):
    return jnp.dot(a.astype(BF16), b.astype(BF16), preferred_element_type=F32)


def _dot_nt(a, b):
    return lax.dot_general(a.astype(BF16), b.astype(BF16), (((1,), (1,)), ((), ())),
                           preferred_element_type=F32)


def _dot_tn(a, b):
    return lax.dot_general(a.astype(BF16), b.astype(BF16), (((0,), (0,)), ((), ())),
                           preferred_element_type=F32)


def _split(x, n):
    parts, r = [], x
    for i in range(n):
        p = r.astype(BF16)
        parts.append(p)
        if i + 1 < n:
            r = r - p.astype(F32)
    return parts


def _sel_dot(m01, x, n=3):
    out = None
    for p in _split(x, n):
        t = jnp.dot(m01, p, preferred_element_type=F32)
        out = t if out is None else out + t
    return out


def _dot_sel(x, m01, n=3):
    mm = _halves_dot if (x.shape[0] >= 2 * MXU_TILE and m01.shape[1] <= MXU_TILE) else (
        lambda a, b: jnp.dot(a, b, preferred_element_type=F32))
    out = None
    for p in _split(x, n):
        t = mm(p, m01)
        out = t if out is None else out + t
    return out


def _halves_dot(a, b):
    half = a.shape[0] // 2
    return jnp.concatenate([jnp.dot(a[0:half], b, preferred_element_type=F32),
                            jnp.dot(a[half:], b, preferred_element_type=F32)], axis=0)


def _log_sigmoid(z):
    return jnp.minimum(z, 0.0) - jnp.log(1.0 + jnp.exp(-jnp.abs(z)))


def _sigmoid(z):
    return 1.0 / (1.0 + jnp.exp(-z))


def _rms(x, gain):
    ms = jnp.mean(x * x, axis=-1, keepdims=True)
    return x * lax.rsqrt(ms + EPS) * gain


def _head_rms(o, mh, gain):
    ms = _dot_sel(o * o, mh, 2)
    return o * lax.rsqrt(ms + EPS) * gain


def _block_diag_mask(n):
    r = lax.broadcasted_iota(jnp.int32, (n, n), 0)
    c = lax.broadcasted_iota(jnp.int32, (n, n), 1)
    return (r >> HEAD_SHIFT) == (c >> HEAD_SHIFT)


LANES = 128
PAIRS = GROUP_W // LANES


def _pair_tile(x, bdp):
    return jnp.where(bdp, jnp.concatenate([x, x], axis=0), 0.0).astype(BF16)


def _pairs(x):
    return [x[:, p * LANES:(p + 1) * LANES] for p in range(PAIRS)]


def _heads_nt(a, x, bdp):
    return jnp.concatenate([_dot_nt(ap, _pair_tile(xp, bdp)) for ap, xp in zip(_pairs(a), _pairs(x))], axis=1)


def _heads_nn(a, x, bdp):
    return jnp.concatenate([_dot(ap, _pair_tile(xp, bdp)) for ap, xp in zip(_pairs(a), _pairs(x))], axis=1)


def _hgrn_level_mats():
    L = CHUNK
    t = np.arange(L)[:, None]
    j = np.arange(L)[None, :]
    blocks = [j <= t]
    for m in HG_SMALL_LEVELS:
        ref = (t // (2 * m)) * 2 * m + m - 1
        blocks.append(j <= ref)
    return np.concatenate(blocks, axis=0).astype(np.float32)


def _head_block(value):
    i = np.arange(GROUP_W)
    return np.where((i[:, None] // DH) == (i[None, :] // DH), value, 0.0).astype(np.float32)


def _gate_expand(col0):
    e = np.zeros((GATE_W, GROUP_W), np.float32)
    for h in range(HEADS):
        e[col0 + h, h * DH:(h + 1) * DH] = 1.0
    return e


def _lb_kernel(logit_ref, o_ref):
    x = logit_ref[...]
    depth = x.shape[0]
    m = x[0:1]
    for l in range(1, depth):
        m = jnp.maximum(m, x[l:l + 1])
    e = [jnp.exp(x[l:l + 1] - m) for l in range(depth)]
    tot = e[0]
    for l in range(1, depth):
        tot = tot + e[l]
    p = [el / tot for el in e]
    c = None
    for l in range(depth):
        c = p[l] if c is None else c + p[l]
        o_ref[l:l + 1, :] = jnp.maximum(c - p[0], 0.0)


def _s5_param_kernel(lr_ref, li_ref, ldt_ref, bre_ref, bim_ref, abr_ref, abi_ref, bbr_ref, bbi_ref):
    lr = jnp.minimum(lr_ref[...], -1e-4)
    li = li_ref[...]
    dt = jnp.exp(ldt_ref[...])
    mag = jnp.exp(lr * dt)
    ab_re = mag * jnp.cos(li * dt)
    ab_im = mag * jnp.sin(li * dt)
    den = lr * lr + li * li
    cf_re = ((ab_re - 1.0) * lr + ab_im * li) / den
    cf_im = (ab_im * lr - (ab_re - 1.0) * li) / den
    bre = bre_ref[...]
    bim = bim_ref[...]
    abr_ref[...] = ab_re
    abi_ref[...] = ab_im
    bbr_ref[...] = cf_re * bre - cf_im * bim
    bbi_ref[...] = cf_re * bim + cf_im * bre


def _s5_params(lam_re, lam_im, b_re, b_im, c_re, c_im, log_dt):
    gp = S5_G * S5_P
    rep = lambda a: jnp.repeat(a, S5_P, axis=0)
    ldt = jnp.broadcast_to(rep(log_dt[:, None]), (gp, S5_N))
    bt = lambda a: a.transpose(0, 2, 1).reshape(gp, S5_N)
    shp = jax.ShapeDtypeStruct((gp, S5_N), F32)
    ab_re, ab_im, bb_re, bb_im = pl.pallas_call(
        _s5_param_kernel, out_shape=(shp, shp, shp, shp), name="s5_params",
    )(rep(lam_re), rep(lam_im), ldt, bt(b_re), bt(b_im))
    lam = jnp.stack([ab_re[::S5_P].reshape(-1), ab_im[::S5_P].reshape(-1)])
    own_gp = jnp.asarray(np.eye(S5_G, dtype=np.float32).repeat(S5_P, axis=0))
    own_gn = jnp.asarray(np.eye(S5_G, dtype=np.float32).repeat(S5_N, axis=0))
    wide = lambda bb: (bb[:, None, :] * own_gp[:, :, None]).reshape(gp, S5_G * S5_N)
    wb = jnp.concatenate([wide(bb_re), wide(bb_im)], axis=1).astype(BF16)
    tall = lambda c: (c.transpose(0, 2, 1).reshape(S5_G * S5_N, S5_P)[:, None, :]
                      * own_gn[:, :, None]).reshape(S5_G * S5_N, gp)
    cm = jnp.concatenate([tall(c_re), -tall(c_im)], axis=0).astype(BF16)
    return lam, wb, cm


IN_COLS = (GROUP_W, 4 * GROUP_W, GROUP_W, 4 * GROUP_W)
IN_DTYPES = (F32, F32, BF16, F32)
GATE_ROWS = 16


def _in_proj_kernel(x_ref, g_ref, w_ref, wt_ref, o_u5, o_hg, o_fk, o_ml, o_fqt, o_fvt, o_gt):
    gw = GROUP_W
    a = _rms(x_ref[...], g_ref[...]).astype(BF16)
    outs = (o_u5, o_hg, o_fk, o_ml)
    c0 = 0
    for o_ref, width in zip(outs, IN_COLS):
        o_ref[...] = jnp.dot(a, w_ref[:, c0:c0 + width], preferred_element_type=F32).astype(o_ref.dtype)
        c0 += width
    t = lax.dot_general(wt_ref[...], a, (((1,), (1,)), ((), ())), preferred_element_type=F32)
    o_fqt[...] = t[0:gw]
    o_fvt[...] = t[gw:2 * gw].astype(o_fvt.dtype)
    o_gt[...] = t[2 * gw:2 * gw + GATE_ROWS]


def _time_major_map(tiles_per_seq):
    return lambda i: (i % tiles_per_seq, i // tiles_per_seq)


def _in_proj(h, gain, w_in_l, tm, seq):
    t_rows, d = h.shape
    assert seq % tm == 0 and t_rows % seq == 0
    gw = GROUP_W
    o_fox_f = 8 * gw
    o_ml = o_fox_f + HEADS
    o_ml_i = o_ml + 4 * gw
    o_ml_f = o_ml_i + HEADS
    gates = jnp.concatenate([w_in_l[:, o_fox_f:o_fox_f + HEADS], w_in_l[:, o_ml_i:o_ml_i + HEADS],
                             w_in_l[:, o_ml_f:o_ml_f + HEADS],
                             jnp.zeros((d, GATE_ROWS - 3 * HEADS), w_in_l.dtype)], axis=1)
    w = jnp.concatenate([w_in_l[:, :5 * gw], w_in_l[:, 6 * gw:7 * gw], w_in_l[:, o_ml:o_ml + 4 * gw]],
                        axis=1).astype(BF16)
    wt = jnp.concatenate([w_in_l[:, 5 * gw:6 * gw], w_in_l[:, 7 * gw:8 * gw], gates], axis=1).T.astype(BF16)
    n_tot = sum(IN_COLS)
    row = lambda i: (i, 0)
    colb = lambda i: (0, i)
    fixed = lambda i: (0, 0)
    return pl.pallas_call(
        _in_proj_kernel,
        grid=(t_rows // tm,),
        in_specs=[pl.BlockSpec((tm, d), row), pl.BlockSpec((1, d), fixed),
                  pl.BlockSpec((d, n_tot), fixed, pipeline_mode=pl.Buffered(1)),
                  pl.BlockSpec((2 * gw + GATE_ROWS, d), fixed, pipeline_mode=pl.Buffered(1))],
        out_specs=[pl.BlockSpec((tm, gw), _time_major_map(seq // tm))]
        + [pl.BlockSpec((tm, c), row) for c in IN_COLS[1:]]
        + [pl.BlockSpec((gw, tm), colb), pl.BlockSpec((gw, tm), colb),
           pl.BlockSpec((GATE_ROWS, tm), lambda i: (i // (seq // tm), i % (seq // tm)))],
        out_shape=[jax.ShapeDtypeStruct((seq, (t_rows // seq) * gw), F32)]
        + [jax.ShapeDtypeStruct((t_rows, c), dt) for c, dt in zip(IN_COLS[1:], IN_DTYPES[1:])]
        + [jax.ShapeDtypeStruct((gw, t_rows), F32), jax.ShapeDtypeStruct((gw, t_rows), BF16),
           jax.ShapeDtypeStruct(((t_rows // seq) * GATE_ROWS, seq), F32)],
        compiler_params=_cparams("parallel"),
        name="in_proj",
    )(h, gain[None, :], w, wt)


def _s5_kernel(u_ref, perm_ref, permt_ref, wb_ref, lam_ref, cm_ref, d_ref, wglu_ref, gain_ref, o_ref,
               *scratch, hb, nb):
    gw = GROUP_W
    ns = S5_G * S5_N
    xs_refs, st_ref = scratch[:-1], scratch[-1]
    nsub = len(xs_refs)

    @pl.when(pl.program_id(0) == 0)
    def _():
        st_ref[...] = jnp.zeros_like(st_ref)

    ar = jnp.broadcast_to(lam_ref[0:1, :], (nb, ns))
    ai = jnp.broadcast_to(lam_ref[1:2, :], (nb, ns))

    def front(k):
        u = jnp.concatenate([u_ref[k * hb:(k + 1) * hb, b * gw:(b + 1) * gw] for b in range(nb)], axis=0)
        u_tb = _halves_dot(perm_ref[...], u.astype(BF16)).astype(BF16)
        xs_refs[k][...] = jnp.dot(u_tb, wb_ref[...], preferred_element_type=F32)
        return u

    def scan(k, xr, xi):
        xs = xs_refs[k]
        for t in range(hb):
            r = slice(t * nb, (t + 1) * nb)
            nr = ar * xr - ai * xi + xs[r, 0:ns]
            ni = ar * xi + ai * xr + xs[r, ns:2 * ns]
            xs[r, 0:ns] = nr
            xs[r, ns:2 * ns] = ni
            xr, xi = nr, ni
        return xr, xi

    def back(k, u):
        cx_tb = _halves_dot(xs_refs[k][...].astype(BF16), cm_ref[...])
        cx = None
        for part in _split(cx_tb, 2):
            t = _halves_dot(permt_ref[...], part)
            cx = t if cx is None else cx + t
        y = cx + d_ref[...] * u
        g = jax.nn.gelu(y)
        y = g * _sigmoid(_halves_dot(g.astype(BF16), wglu_ref[...]))
        out = _rms(y, gain_ref[...]).astype(o_ref.dtype)
        for b in range(nb):
            o_ref[k * hb:(k + 1) * hb, b * gw:(b + 1) * gw] = out[b * hb:(b + 1) * hb]

    xr, xi = st_ref[:, 0:ns], st_ref[:, ns:2 * ns]
    us = {0: front(0)}
    for k in range(nsub):
        if k + 1 < nsub:
            us[k + 1] = front(k + 1)
        xr, xi = scan(k, xr, xi)
        if k + 1 == nsub:
            st_ref[:, 0:ns] = xr
            st_ref[:, ns:2 * ns] = xi
        back(k, us.pop(k))


def _s5_mixer(u2d, lam, wb, cm, d_skip, w_glu, gain, lb, nb, nsub):
    s, w = u2d.shape
    gw = GROUP_W
    assert w == nb * gw
    ns2 = 2 * S5_G * S5_N
    assert lb % nsub == 0 and s % lb == 0
    hb = lb // nsub
    idx = np.arange(hb * nb)
    perm = np.zeros((hb * nb, hb * nb), np.float32)
    perm[idx, (idx % nb) * hb + idx // nb] = 1.0
    fixed = lambda i: (0, 0)
    return pl.pallas_call(
        functools.partial(_s5_kernel, hb=hb, nb=nb),
        grid=(s // lb,),
        in_specs=[pl.BlockSpec((lb, w), lambda i: (i, 0)),
                  pl.BlockSpec(perm.shape, fixed), pl.BlockSpec(perm.shape, fixed),
                  pl.BlockSpec((gw, ns2), fixed), pl.BlockSpec((2, ns2 // 2), fixed),
                  pl.BlockSpec((ns2, gw), fixed), pl.BlockSpec((1, gw), fixed),
                  pl.BlockSpec((gw, gw), fixed), pl.BlockSpec((1, gw), fixed)],
        out_specs=pl.BlockSpec((lb, w), lambda i: (i, 0)),
        out_shape=jax.ShapeDtypeStruct((s, w), BF16),
        scratch_shapes=[pltpu.VMEM((hb * nb, ns2), F32)] * nsub + [pltpu.VMEM((nb, ns2), F32)],
        compiler_params=_cparams("arbitrary"),
        name="s5_mixer",
    )(u2d, jnp.asarray(perm, BF16), jnp.asarray(perm.T, BF16), wb, lam, cm, d_skip[None, :],
      w_glu.astype(BF16), gain[None, :])


def _hgrn_kernel(x_ref, lb_ref, gain_ref, mall_ref, mh_ref, o_ref, st_ref, *, ngroup, nsub):
    L = CHUNK
    gw = GROUP_W
    R = nsub * L

    @pl.when(pl.program_id(1) == 0)
    def _():
        st_ref[...] = jnp.zeros_like(st_ref)

    lb = lb_ref[...]
    gain = gain_ref[...]
    bdp = _block_diag_mask(LANES)
    row = lax.broadcasted_iota(jnp.int32, (R, gw), 0) & (L - 1)
    row_c = lax.broadcasted_iota(jnp.int32, (L, gw), 0)
    col_c = lax.broadcasted_iota(jnp.int32, (L, gw), 1) & (DH - 1)
    chunks = [slice(c * L, (c + 1) * L) for c in range(nsub)]

    def group(gi, carry):
        r0 = pl.multiple_of(gi * R, R)
        q = x_ref[0, pl.ds(r0, R), 0:gw]
        z = x_ref[0, pl.ds(r0, R), gw:2 * gw]
        v = x_ref[0, pl.ds(r0, R), 2 * gw:3 * gw]
        gg = x_ref[0, pl.ds(r0, R), 3 * gw:4 * gw]
        logf = _log_sigmoid(z) + jnp.log(1.0 + lb * jnp.exp(jnp.minimum(-z, EXP_CLIP)))
        kk = (1.0 - lb) * _sigmoid(-z)
        cums = [_sel_dot(mall_ref[...], logf[c]) for c in chunks]
        bs = [cm[0:L] for cm in cums]
        b = jnp.concatenate(bs, axis=0)
        a = [jnp.where(row_c == col_c, _heads_nt(q[c], kk[c], bdp), 0.0) for c in chunks]
        for m in HG_LEVELS:
            upper = (row_c & m) != 0
            sh = int(math.log2(2 * m))
            same = (row_c >> sh) == (col_c >> sh)
            for n, c in enumerate(chunks):
                if m in HG_SMALL_LEVELS:
                    i = 1 + HG_SMALL_LEVELS.index(m)
                    b_ref = cums[n][i * L:(i + 1) * L]
                else:
                    b_ref = jnp.concatenate([jnp.broadcast_to(bs[n][c0 + m - 1:c0 + m], (2 * m, gw))
                                             for c0 in range(0, L, 2 * m)], axis=0)
                w = jnp.where(upper, q[c], kk[c]) * jnp.exp(-jnp.abs(bs[n] - b_ref))
                ql = jnp.where(upper, w, 0.0)
                kl = jnp.where(upper, 0.0, w)
                a[n] = a[n] + jnp.where(same, _heads_nt(ql, kl, bdp), 0.0)
        o_intra = [_heads_nn(a[n], v[c], bdp) for n, c in enumerate(chunks)]
        b_last = [b[(n + 1) * L - 1:(n + 1) * L] for n in range(nsub)]
        kdec = kk * jnp.exp(jnp.concatenate([jnp.broadcast_to(bl, (L, gw)) for bl in b_last], axis=0) - b)
        d_st = [[jnp.where(bdp, _dot_tn(vp, kp), 0.0) for vp, kp in zip(_pairs(v[c]), _pairs(kdec[c]))]
                for c in chunks]
        qe = q * jnp.exp(b)
        st = [st_ref[p] for p in range(PAIRS)]
        outs = []
        for n, c in enumerate(chunks):
            o_inter = jnp.concatenate([_dot_nt(qp, st[p]) for p, qp in enumerate(_pairs(qe[c]))], axis=1)
            outs.append(o_intra[n] + o_inter)
            decay = _pairs(jnp.exp(b_last[n]))
            st = [st[p] * decay[p] + d_st[n][p] for p in range(PAIRS)]
        for p in range(PAIRS):
            st_ref[p] = st[p]
        o = jnp.concatenate(outs, axis=0)
        out = _head_rms(o, mh_ref[...], gain) * (gg * _sigmoid(gg))
        o_ref[0, pl.ds(r0, R), :] = out.astype(o_ref.dtype)
        return carry

    lax.fori_loop(0, ngroup, group, 0)


def _hgrn_mixer(hg, lb, gain, lg, nsub=8):
    b, s, w = hg.shape
    gw = GROUP_W
    assert lg % (CHUNK * nsub) == 0
    mall = jnp.asarray(_hgrn_level_mats(), BF16)
    mh = jnp.asarray(_head_block(1.0 / DH), BF16)
    fixed = lambda i, j: (0, 0)
    return pl.pallas_call(
        functools.partial(_hgrn_kernel, ngroup=lg // (CHUNK * nsub), nsub=nsub),
        grid=(b, s // lg),
        in_specs=[pl.BlockSpec((1, lg, w), lambda i, j: (i, j, 0)),
                  pl.BlockSpec((1, gw), fixed), pl.BlockSpec((1, gw), fixed),
                  pl.BlockSpec(mall.shape, fixed), pl.BlockSpec((gw, gw), fixed)],
        out_specs=pl.BlockSpec((1, lg, gw), lambda i, j: (i, j, 0)),
        out_shape=jax.ShapeDtypeStruct((b, s, gw), BF16),
        scratch_shapes=[pltpu.VMEM((PAIRS, LANES, LANES), F32)],
        compiler_params=_cparams("parallel", "arbitrary"),
        name="hgrn2_mixer",
    )(hg, lb[None, :], gain[None, :], mall, mh)


FOX_SPLIT = 3


def _fox_bias_lane(h, j):
    return (h ^ 1) * DH + j


def _gate_kernel(gt_ref, bias_ref, triu_ref, place_ref, ones_ref, eye_ref, kb_ref, rowo_ref, grow_ref, carry_ref,
                 *, nb):
    rows = GATE_ROWS

    @pl.when(pl.program_id(0) == 0)
    def _():
        carry_ref[...] = jnp.zeros_like(carry_ref)

    g = gt_ref[...] + bias_ref[...]
    cs = _dot_sel(_log_sigmoid(g), triu_ref[...]) + carry_ref[...]
    lg = cs.shape[1]
    carry_ref[...] = cs[:, lg - 1:lg]
    ck = cs * (-math.log2(math.e))
    for b in range(nb):
        r = slice(b * rows, (b + 1) * rows)
        rowo_ref[b] = cs[b * rows:b * rows + SUBLANES]
        grow_ref[b] = _dot_tn(jnp.concatenate(_split(g[r], FOX_SPLIT), axis=0), eye_ref[...])
        kb = _dot_tn(jnp.concatenate(_split(ck[r], FOX_SPLIT), axis=0), place_ref[...]) + ones_ref[...]
        kb_ref[b] = kb.astype(kb_ref.dtype)


def _gates(gt, gate_bias_l, bsz, lg):
    rows = GATE_ROWS
    s = gt.shape[1]
    assert gt.shape[0] == bsz * rows and s % lg == 0
    nblk = s // lg
    gw = GROUP_W
    triu = jnp.asarray(np.triu(np.ones((lg, lg), np.float32)), BF16)
    bias = jnp.tile(jnp.concatenate([gate_bias_l, jnp.zeros((rows - gate_bias_l.shape[0],), F32)]), bsz)[:, None]
    place = np.zeros((FOX_SPLIT * rows, gw), np.float32)
    eye = np.zeros((FOX_SPLIT * rows, GATE_W), np.float32)
    ones = np.zeros((1, gw), np.float32)
    for j in range(FOX_SPLIT):
        for h in range(HEADS):
            place[j * rows + h, _fox_bias_lane(h, j)] = 1.0
            ones[0, _fox_bias_lane(h, FOX_SPLIT + j)] = 1.0
        for f in range(rows):
            eye[j * rows + f, f] = 1.0
    fixed = lambda j: (0, 0)
    k_bias, cf_row, g_rows = pl.pallas_call(
        functools.partial(_gate_kernel, nb=bsz),
        grid=(nblk,),
        in_specs=[pl.BlockSpec((bsz * rows, lg), lambda j: (0, j)),
                  pl.BlockSpec((bsz * rows, 1), fixed), pl.BlockSpec((lg, lg), fixed),
                  pl.BlockSpec(place.shape, fixed), pl.BlockSpec((1, gw), fixed), pl.BlockSpec(eye.shape, fixed)],
        out_specs=[pl.BlockSpec((bsz, lg, gw), lambda j: (0, j, 0)),
                   pl.BlockSpec((bsz, SUBLANES, lg), lambda j: (0, 0, j)),
                   pl.BlockSpec((bsz, lg, GATE_W), lambda j: (0, j, 0))],
        out_shape=[jax.ShapeDtypeStruct((bsz, s, gw), BF16), jax.ShapeDtypeStruct((bsz, SUBLANES, s), F32),
                   jax.ShapeDtypeStruct((bsz, s, GATE_W), F32)],
        scratch_shapes=[pltpu.VMEM((bsz * rows, 1), F32)],
        compiler_params=_cparams("arbitrary"),
        name="gate_cumsums",
    )(gt, bias, triu, jnp.asarray(place, BF16), jnp.asarray(ones), jnp.asarray(eye, BF16))
    return k_bias.reshape(bsz * s, gw), cf_row, g_rows


def _fox_kernel(qt_ref, k_ref, kb_ref, vt_ref, crow_ref, gain_ref, o_ref, kaug_ref, *, tq, tk):
    qi = pl.program_id(1)
    gw = GROUP_W
    seq = k_ref.shape[0]
    log2e = math.log2(math.e)
    fill_rows = min(seq, 512)

    @pl.when(qi == 0)
    def _():
        lane_head = lax.broadcasted_iota(jnp.int32, (fill_rows, gw), 1) >> HEAD_SHIFT

        def fill(i, carry):
            r0 = pl.multiple_of(i * fill_rows, fill_rows)
            kk = k_ref[pl.ds(r0, fill_rows), :]
            kb = kb_ref[pl.ds(r0, fill_rows), :]
            for h in range(HEADS):
                kaug_ref[h, pl.ds(r0, fill_rows), :] = jnp.where(lane_head == h, kk, kb)
            return carry

        lax.fori_loop(0, seq // fill_rows, fill, 0)

    qt = qt_ref[...] * (DH ** -0.5 * log2e)
    row = lax.broadcasted_iota(jnp.int32, (gw, tq), 0)
    head_row = row >> HEAD_SHIFT
    q_bias = jnp.where((row & (DH - 1)) < FOX_SPLIT, 1.0, 0.0)
    for h in range(HEADS):
        for j, part in enumerate(_split(crow_ref[0, h:h + 1, :] * log2e, FOX_SPLIT)):
            q_bias = jnp.where(row == _fox_bias_lane(h, FOX_SPLIT + j), part.astype(F32), q_bias)
    q_heads = [jnp.where(head_row == h, qt, jnp.where(head_row == (h ^ 1), q_bias, 0.0)).astype(BF16)
               for h in range(HEADS)]
    key_i = lax.broadcasted_iota(jnp.int32, (tk, tq), 0)
    qry_i = lax.broadcasted_iota(jnp.int32, (tk, tq), 1)
    kv_per_q = tq // tk
    ones_rows = jnp.ones((BF16_ROWS, tk), BF16)

    def run_tiles(tiles, state):
        state = list(state)

        def scores(j, diag_offset, h):
            r0 = pl.multiple_of(j * tk, tk)
            q0 = diag_offset or 0
            u = jnp.dot(kaug_ref[h, pl.ds(r0, tk), :], q_heads[h][:, q0:], preferred_element_type=F32)
            if diag_offset is not None:
                kpos = lax.broadcasted_iota(jnp.int32, u.shape, 0) + diag_offset
                qpos = lax.broadcasted_iota(jnp.int32, u.shape, 1) + q0
                u = jnp.where(kpos <= qpos, u, NEG_BIG)
            m_new = jnp.maximum(state[h][0][:, q0:], jnp.max(u, axis=0, keepdims=True))
            return u, m_new

        def accumulate(j, diag_offset, h, u, m_new):
            r0 = pl.multiple_of(j * tk, tk)
            q0 = diag_offset or 0
            m_old, l_old, acc_old = (t[:, q0:] for t in state[h])
            alpha = jnp.exp2(m_old - m_new)
            vt1 = jnp.concatenate([vt_ref[h * DH:(h + 1) * DH, pl.ds(r0, tk)], ones_rows], axis=0)
            pv = None
            for k0 in range(0, tk, MXU_TILE):
                p = jnp.exp2(u[k0:k0 + MXU_TILE] - m_new[0:1]).astype(BF16)
                t = jnp.dot(vt1[:, k0:k0 + MXU_TILE], p, preferred_element_type=F32)
                pv = t if pv is None else pv + t
            new = (m_new, alpha * l_old + pv[DH:DH + 1], alpha[0:1] * acc_old + pv[0:DH])
            if q0:
                new = tuple(jnp.concatenate([full[:, 0:q0], part], axis=1) for full, part in zip(state[h], new))
            state[h] = new

        items = [(j, off, h) for j, off in tiles for h in range(HEADS)]
        ahead = min(FOX_LOOKAHEAD, HEADS - 1)
        queue = [scores(*it) for it in items[:ahead]]
        for n, (j, off, h) in enumerate(items):
            if n + ahead < len(items):
                queue.append(scores(*items[n + ahead]))
            accumulate(j, off, h, *queue.pop(0))
        return tuple(state)

    init = tuple((jnp.full((SUBLANES, tq), NEG_BIG, F32), jnp.zeros((SUBLANES, tq), F32),
                  jnp.zeros((DH, tq), F32)) for _ in range(HEADS))
    n_full = qi * kv_per_q
    state = lax.fori_loop(0, n_full // 2, lambda i, s: run_tiles([(2 * i, None), (2 * i + 1, None)], s), init)
    diag = [(n_full + d, d * tk) for d in range(kv_per_q)]
    state = lax.cond(n_full % 2 == 1,
                     lambda s: run_tiles([(n_full - 1, None)] + diag, s),
                     lambda s: run_tiles(diag, s), state)

    outs = []
    for h in range(HEADS):
        _, l_fin, acc_fin = state[h]
        o = acc_fin / l_fin[0:1]
        ms = jnp.mean(o * o, axis=0, keepdims=True)
        outs.append(o * lax.rsqrt(ms + EPS))
    out = jnp.concatenate(outs, axis=0) * gain_ref[...]
    o_ref[...] = out.T.astype(o_ref.dtype)


def _fox_mixer(fqt, fk, k_bias, fvt, cf_row, gain, bsz, tq, tk):
    gw, t_rows = fqt.shape
    s = t_rows // bsz
    nq = s // tq
    return pl.pallas_call(
        functools.partial(_fox_kernel, tq=tq, tk=tk),
        grid=(bsz, nq),
        in_specs=[pl.BlockSpec((gw, tq), lambda i, qi: (0, i * nq + qi)),
                  pl.BlockSpec((s, gw), lambda i, qi: (i, 0)),
                  pl.BlockSpec((s, gw), lambda i, qi: (i, 0)),
                  pl.BlockSpec((gw, s), lambda i, qi: (0, i)),
                  pl.BlockSpec((1, SUBLANES, tq), lambda i, qi: (i, 0, qi)),
                  pl.BlockSpec((gw, 1), lambda i, qi: (0, 0))],
        out_specs=pl.BlockSpec((tq, gw), lambda i, qi: (i * nq + qi, 0)),
        out_shape=jax.ShapeDtypeStruct((t_rows, gw), BF16),
        scratch_shapes=[pltpu.VMEM((HEADS, s, gw), BF16)],
        compiler_params=_cparams("arbitrary", "arbitrary"),
        name="fox_mixer",
    )(fqt, fk, k_bias, fvt, cf_row, gain[:, None])


def _head_lane_max(x):
    rows = x.shape[0]
    parts = [jnp.broadcast_to(jnp.max(x[:, h * DH:(h + 1) * DH], axis=-1, keepdims=True), (rows, DH))
             for h in range(HEADS)]
    return jnp.concatenate(parts, axis=-1)


def _mlstm_kernel(x_ref, gt_ref, cw_ref, gain_ref, tri_ref, eb_ref, ei_ref, mh_ref, o_ref,
                  cbuf, ct_ref, m_ref, *, nsub, lg):
    L = CHUNK
    gw = GROUP_W
    tail = SUBLANES

    @pl.when(pl.program_id(1) == 0)
    def _():
        cbuf[0:tail, :] = jnp.zeros((tail, 2 * gw), F32)
        ct_ref[...] = jnp.zeros_like(ct_ref)
        m_ref[...] = jnp.zeros_like(m_ref)

    cbuf[tail:tail + lg, :] = x_ref[0, :, 0:2 * gw]
    acc = None
    for j in range(ML_CONV):
        term = cbuf[pl.ds(tail - (ML_CONV - 1) + j, lg), :] * cw_ref[j:j + 1, :]
        acc = term if acc is None else acc + term
    cbuf[0:tail, :] = cbuf[lg:lg + tail, :]
    qk = acc * _sigmoid(acc)
    q = qk[:, 0:gw]
    k = qk[:, gw:2 * gw] * (DH ** -0.5)
    v = x_ref[0, :, 2 * gw:3 * gw]
    og = x_ref[0, :, 3 * gw:4 * gw]
    g = gt_ref[0]

    bdp = _block_diag_mask(LANES)
    bdp2 = jnp.concatenate([bdp, bdp], axis=1)
    row = lax.broadcasted_iota(jnp.int32, (L, gw), 0)
    col = lax.broadcasted_iota(jnp.int32, (L, gw), 1) & (DH - 1)
    causal = col <= row
    diag = col == row
    ones = jnp.ones((L, LANES), F32)
    ones_blocks = jnp.where(bdp, 1.0, 0.0).astype(BF16)
    chunks = [slice(c * L, (c + 1) * L) for c in range(nsub)]

    def regroup(parts):
        return jnp.concatenate([t[:, 0:LANES] for t in parts] + [t[:, LANES:2 * LANES] for t in parts], axis=1)

    lsg = _log_sigmoid(g)
    cs = jnp.concatenate([_sel_dot(tri_ref[...], lsg[c]) for c in chunks], axis=0)
    b_exp = _dot_sel(cs, eb_ref[...])
    imb = _dot_sel(g, ei_ref[...]) - b_exp

    m_loc, nd_loc, b_last, m_src, d_ct = [], [], [], [], []
    for c in chunks:
        imb_row = jnp.sum(jnp.where(diag, imb[c], 0.0), axis=0, keepdims=True)
        d_log = jnp.where(causal, b_exp[c] + imb_row, NEG_BIG)
        ml = _head_lane_max(d_log)
        qk_loc = _heads_nt(q[c], k[c], bdp) * jnp.exp(d_log - ml)
        nd_loc.append(regroup([_dot(qp, jnp.concatenate([_pair_tile(vp, bdp), ones_blocks], axis=1))
                               for qp, vp in zip(_pairs(qk_loc), _pairs(v[c]))]))
        m_loc.append(ml)
        bl = b_exp[c][L - 1:L]
        src = bl + imb[c]
        ms = jnp.max(src, axis=0, keepdims=True)
        kw = k[c] * jnp.exp(src - ms)
        d_ct.append([jnp.where(bdp2, _dot_tn(kp, jnp.concatenate([vp, ones], axis=1)), 0.0)
                     for kp, vp in zip(_pairs(kw), _pairs(v[c]))])
        b_last.append(bl)
        m_src.append(ms)

    ct = [ct_ref[p] for p in range(PAIRS)]
    m_prev = m_ref[...]
    hs = []
    for n, c in enumerate(chunks):
        inter = b_exp[c] + m_prev
        m_t = jnp.maximum(inter, m_loc[n])
        w_inter = jnp.exp(inter - m_t)
        w_loc = jnp.exp(m_loc[n] - m_t)
        q_ct = regroup([_dot(qp, ct[p]) for p, qp in enumerate(_pairs(q[c]))])
        nd = (jnp.concatenate([w_inter, w_inter], axis=1) * q_ct
              + jnp.concatenate([w_loc, w_loc], axis=1) * nd_loc[n])
        hs.append(nd[:, 0:gw] / jnp.maximum(jnp.abs(nd[:, gw:2 * gw]), jnp.exp(-m_t)))
        m_new = jnp.maximum(b_last[n] + m_prev, m_src[n])
        decay = _pairs(jnp.exp(b_last[n] + m_prev - m_new))
        w_src = _pairs(jnp.exp(m_src[n] - m_new))
        ct = [ct[p] * jnp.concatenate([decay[p], decay[p]], axis=1)
              + d_ct[n][p] * jnp.concatenate([w_src[p], w_src[p]], axis=1) for p in range(PAIRS)]
        m_prev = m_new
    for p in range(PAIRS):
        ct_ref[p] = ct[p]
    m_ref[...] = m_prev
    hh = jnp.concatenate(hs, axis=0)
    o_ref[0] = (_head_rms(hh, mh_ref[...], gain_ref[...]) * _sigmoid(og)).astype(o_ref.dtype)


def _mlstm_mixer(ml, gt, conv_w, gain, lg):
    b, s, w = ml.shape
    gw = GROUP_W
    assert lg % CHUNK == 0
    tri = jnp.asarray(np.tril(np.ones((CHUNK, CHUNK), np.float32)), BF16)
    eb = jnp.asarray(_gate_expand(2 * HEADS), BF16)
    ei = jnp.asarray(_gate_expand(HEADS), BF16)
    mh = jnp.asarray(_head_block(1.0 / DH), BF16)
    fixed = lambda i, j: (0, 0)
    blk = lambda i, j: (i, j, 0)
    return pl.pallas_call(
        functools.partial(_mlstm_kernel, nsub=lg // CHUNK, lg=lg),
        grid=(b, s // lg),
        in_specs=[pl.BlockSpec((1, lg, w), blk), pl.BlockSpec((1, lg, GATE_W), blk),
                  pl.BlockSpec((ML_CONV, 2 * gw), fixed), pl.BlockSpec((1, gw), fixed),
                  pl.BlockSpec((CHUNK, CHUNK), fixed), pl.BlockSpec((GATE_W, gw), fixed),
                  pl.BlockSpec((GATE_W, gw), fixed), pl.BlockSpec((gw, gw), fixed)],
        out_specs=pl.BlockSpec((1, lg, gw), blk),
        out_shape=jax.ShapeDtypeStruct((b, s, gw), BF16),
        scratch_shapes=[pltpu.VMEM((lg + SUBLANES, 2 * gw), F32), pltpu.VMEM((PAIRS, LANES, 2 * LANES), F32),
                        pltpu.VMEM((1, gw), F32)],
        compiler_params=_cparams("parallel", "arbitrary"),
        name="mlstm_mixer",
    )(ml, gt, conv_w, gain[None, :], tri, eb, ei, mh)


def _post_kernel(ya_ref, yb_ref, yc_ref, yd_ref, h_ref, wo_ref, gpost_ref, gpre_ref, wg_ref, wu_ref, wd_ref,
                 gffn_ref, o_ref, *, ff_chunk):
    gw = GROUP_W
    tm = h_ref.shape[0]
    for r0 in range(0, tm, 2 * POST_ROWS):
        _post_rows(ya_ref, yb_ref, yc_ref, yd_ref, h_ref, wo_ref, gpost_ref, gpre_ref, wg_ref, wu_ref, wd_ref,
                   gffn_ref, o_ref, ff_chunk, [slice(r0, r0 + POST_ROWS), slice(r0 + POST_ROWS, r0 + 2 * POST_ROWS)])


def _post_rows(ya_ref, yb_ref, yc_ref, yd_ref, h_ref, wo_ref, gpost_ref, gpre_ref, wg_ref, wu_ref, wd_ref,
               gffn_ref, o_ref, ff_chunk, halves):
    gw = GROUP_W
    h1, a = [], []
    for r in halves:
        mix = None
        for i, y_ref in enumerate((ya_ref, yb_ref, yc_ref, yd_ref)):
            t = jnp.dot(y_ref[r, :], wo_ref[i * gw:(i + 1) * gw, :], preferred_element_type=F32)
            mix = t if mix is None else mix + t
        h1.append(h_ref[r, :] + _rms(mix, gpost_ref[...]))
    for n in range(len(halves)):
        a.append(_rms(h1[n], gpre_ref[...]).astype(BF16))
    d_ff = wg_ref.shape[1]
    ff = [None] * len(halves)
    for c0 in range(0, d_ff, ff_chunk):
        c1 = min(c0 + ff_chunk, d_ff)
        gu = [(jnp.dot(a[n], wg_ref[:, c0:c1], preferred_element_type=F32),
               jnp.dot(a[n], wu_ref[:, c0:c1], preferred_element_type=F32)) for n in range(len(halves))]
        for n, (g, u) in enumerate(gu):
            act = (g * _sigmoid(g) * u).astype(BF16)
            t = jnp.dot(act, wd_ref[c0:c1, :], preferred_element_type=F32)
            ff[n] = t if ff[n] is None else ff[n] + t
    for n, r in enumerate(halves):
        o_ref[r, :] = h1[n] + _rms(ff[n], gffn_ref[...])


def _post(ya, yb, yc, yd, h, w_out, g_post, g_pre, w_gate, w_up, w_down, g_ffn, tm):
    t_rows, d = h.shape
    gw = GROUP_W
    seq = ya.shape[0]
    assert seq % tm == 0
    d_ff = w_gate.shape[1]
    ff_chunk = min(d_ff, -(-d_ff // (2 * MXU_TILE)) * MXU_TILE)
    row = lambda i: (i, 0)
    fixed = lambda i: (0, 0)
    once = pl.Buffered(1)
    wspec = lambda shape: pl.BlockSpec(shape, fixed, pipeline_mode=once)
    gspec = pl.BlockSpec((1, d), fixed)
    return pl.pallas_call(
        functools.partial(_post_kernel, ff_chunk=ff_chunk),
        grid=(t_rows // tm,),
        in_specs=[pl.BlockSpec((tm, gw), _time_major_map(seq // tm))] + [pl.BlockSpec((tm, gw), row)] * 3
        + [pl.BlockSpec((tm, d), row), wspec((d, d)), gspec, gspec,
           wspec((d, d_ff)), wspec((d, d_ff)), wspec((d_ff, d)), gspec],
        out_specs=pl.BlockSpec((tm, d), row),
        out_shape=jax.ShapeDtypeStruct((t_rows, d), F32),
        compiler_params=_cparams("parallel"),
        name="out_proj_ffn",
    )(ya, yb, yc, yd, h, w_out.astype(BF16), g_post[None, :], g_pre[None, :],
      w_gate.astype(BF16), w_up.astype(BF16), w_down.astype(BF16), g_ffn[None, :])


def kernel(x, w_in, gate_bias, s5_lambda_re, s5_lambda_im, s5_b_re, s5_b_im, s5_c_re, s5_c_im, s5_d, s5_log_dt,
           s5_w_glu, hgrn_lb_logits, mlstm_conv_w, mix_gain, w_out, ln_mix_pre, ln_mix_post, ln_ffn_pre,
           ln_ffn_post, w_ffn_gate, w_ffn_up, w_ffn_down):
    bsz, seq, d = x.shape
    depth = w_in.shape[0]
    gw = GROUP_W
    tm = min(512, seq)
    tm_in = min(1024, seq)
    lg = min(512, seq)
    s5_sub = 64
    s5_nsub = max(1, min(4, seq // s5_sub))
    s5_lb = s5_sub * s5_nsub
    gate_lg = min(1024, seq)
    fox_tq = min(1024, seq)
    fox_tk = fox_tq // 2

    lb_all = pl.pallas_call(_lb_kernel, out_shape=jax.ShapeDtypeStruct(hgrn_lb_logits.shape, F32),
                            name="hgrn_lower_bounds")(hgrn_lb_logits)

    h = x.reshape(bsz * seq, d)
    for l in range(depth):
        gain = mix_gain[l]
        u5, hg, fk, ml, fqt, fvt, gt = _in_proj(h, ln_mix_pre[l], w_in[l], tm_in, seq)
        lam, wb, cm = _s5_params(s5_lambda_re[l], s5_lambda_im[l], s5_b_re[l], s5_b_im[l],
                                 s5_c_re[l], s5_c_im[l], s5_log_dt[l])
        ya = _s5_mixer(u5, lam, wb, cm, s5_d[l], s5_w_glu[l], gain[0:gw], s5_lb, bsz, s5_nsub)
        yb = _hgrn_mixer(hg.reshape(bsz, seq, 4 * gw), lb_all[l], gain[gw:2 * gw], min(2 * lg, seq),
                         nsub=min(2 * lg, seq) // CHUNK)
        k_bias, cf_row, gt3 = _gates(gt, gate_bias[l], bsz, gate_lg)
        yc = _fox_mixer(fqt, fk, k_bias, fvt, cf_row, gain[2 * gw:3 * gw], bsz, fox_tq, fox_tk)
        yd = _mlstm_mixer(ml.reshape(bsz, seq, 4 * gw), gt3, mlstm_conv_w[l], gain[3 * gw:4 * gw],
                          min(2 * lg, seq))
        h = _post(ya, yb.reshape(bsz * seq, gw), yc, yd.reshape(bsz * seq, gw), h,
                  w_out[l], ln_mix_post[l], ln_ffn_pre[l], w_ffn_gate[l], w_ffn_up[l], w_ffn_down[l],
                  ln_ffn_post[l], tm)
    return h.reshape(bsz, seq, d)
```
